```python
import jax
import jax.numpy as jnp
from jax import lax
import numpy as np

D_MODEL = 1024
BATCH = 2
SEQ = 8192
DEPTH = 2
DEC_BATCH = 128
DEC_SEQ = 8
PAST_LEN = 16384
PAGE_SIZE = 128

HEAD_DIM = 64
N_HEADS = D_MODEL // HEAD_DIM
N_KV_HEADS = N_HEADS // 4
Q_REP = N_HEADS // N_KV_HEADS
ROT_DIM = HEAD_DIM // 4
ROPE_THETA = 500000.0
WINDOW = 128
ATTN_BLOCK = 128
CONF_WIDTH = D_MODEL // 2
CONF_CONV_W = 31
SC_WIDTH = D_MODEL // 2
SC_CONV_W = 3
GMLP_WIDTH = D_MODEL // 2
GMLP_CHUNK = 128
GMLP_GROUPS = 8
GMLP_GROUP_DIM = GMLP_WIDTH // GMLP_GROUPS
N_BRANCHES = 4
D_FF = 4 * D_MODEL
EPS = 1e-6

COL_SIZES = (N_HEADS * HEAD_DIM, N_KV_HEADS * HEAD_DIM, N_KV_HEADS * HEAD_DIM,
             2 * CONF_WIDTH, 3 * SC_WIDTH, 2 * GMLP_WIDTH, N_BRANCHES * D_MODEL)
IN_COLS = sum(COL_SIZES)

kernel_name = 'hybrid_gated_branch_decoder'


def rms_norm(x, g):
    xf = x.astype(jnp.float32)
    y = xf * lax.rsqrt(jnp.mean(xf * xf, axis=-1, keepdims=True) + EPS)
    return (y * g.astype(jnp.float32)).astype(x.dtype)


def layer_norm(x, g, b):
    xf = x.astype(jnp.float32)
    mu = jnp.mean(xf, axis=-1, keepdims=True)
    xc = xf - mu
    y = xc * lax.rsqrt(jnp.mean(xc * xc, axis=-1, keepdims=True) + EPS)
    return (y * g.astype(jnp.float32) + b.astype(jnp.float32)).astype(x.dtype)


def split_columns(proj):
    idx = []
    acc = 0
    for s in COL_SIZES[:-1]:
        acc += s
        idx.append(acc)
    return jnp.split(proj, idx, axis=-1)


def rope_partial(x, pos):
    half = ROT_DIM // 2
    inv_freq = jnp.power(ROPE_THETA, -jnp.arange(half, dtype=jnp.float32) * 2.0 / ROT_DIM)
    ang = pos.astype(jnp.float32)[:, None] * inv_freq[None, :]
    cos = jnp.cos(ang)[:, None, :]
    sin = jnp.sin(ang)[:, None, :]
    xr = x[..., :ROT_DIM].astype(jnp.float32)
    x1, x2 = xr[..., :half], xr[..., half:]
    rot = jnp.concatenate([x1 * cos - x2 * sin, x2 * cos + x1 * sin], axis=-1).astype(x.dtype)
    return jnp.concatenate([rot, x[..., ROT_DIM:]], axis=-1)


def sink_attention(q, k, v, mask, sinks):
    s = jnp.einsum('...qgrd,...kgd->...grqk', q, k,
                   preferred_element_type=jnp.float32) * (HEAD_DIM ** -0.5)
    s = jnp.where(mask, s, -1e30)
    sink = jnp.broadcast_to(sinks.astype(jnp.float32).reshape(N_KV_HEADS, Q_REP, 1, 1),
                            s.shape[:-1] + (1,))
    p = jax.nn.softmax(jnp.concatenate([s, sink], axis=-1), axis=-1)[..., :-1]
    return jnp.einsum('...grqk,...kgd->...qgrd', p.astype(v.dtype), v)


def attn_prompt(q, k, v, sinks):
    bsz, seq = q.shape[0], q.shape[1]
    nb = seq // ATTN_BLOCK
    qb = q.reshape(bsz, nb, ATTN_BLOCK, N_KV_HEADS, Q_REP, HEAD_DIM)

    def band(t):
        tp = jnp.pad(t, ((0, 0), (ATTN_BLOCK, 0), (0, 0), (0, 0)))
        tp = tp.reshape(bsz, nb + 1, ATTN_BLOCK, N_KV_HEADS, HEAD_DIM)
        return jnp.concatenate([tp[:, :-1], tp[:, 1:]], axis=2)

    kb, vb = band(k), band(v)
    qpos = jnp.arange(seq).reshape(nb, ATTN_BLOCK)
    kpos = jnp.arange(-ATTN_BLOCK, seq).reshape(nb + 1, ATTN_BLOCK)
    kpos = jnp.concatenate([kpos[:-1], kpos[1:]], axis=1)
    d = qpos[:, :, None] - kpos[:, None, :]
    mask = (d >= 0) & (d <= WINDOW) & (kpos[:, None, :] >= 0)
    o = sink_attention(qb, kb, vb, mask[:, None, None], sinks)
    return o.reshape(bsz, seq, N_HEADS * HEAD_DIM)


def attn_sample(q, k, v, k_buf, v_buf, sinks):
    bsz, seq = q.shape[0], q.shape[1]
    wlen = k_buf.shape[1]
    kc = jnp.concatenate([k_buf.astype(k.dtype), k], axis=1)
    vc = jnp.concatenate([v_buf.astype(v.dtype), v], axis=1)
    qpos = PAST_LEN + jnp.arange(seq)
    kpos = PAST_LEN - wlen + jnp.arange(wlen + seq)
    d = qpos[:, None] - kpos[None, :]
    mask = (d >= 0) & (d <= WINDOW)
    o = sink_attention(q.reshape(bsz, seq, N_KV_HEADS, Q_REP, HEAD_DIM), kc, vc, mask, sinks)
    return o.reshape(bsz, seq, N_HEADS * HEAD_DIM), kc[:, -wlen:], vc[:, -wlen:]


def causal_dwconv(u, buf, w):
    ch = u.shape[-1]
    wlen = w.shape[0]
    cat = jnp.concatenate([buf.astype(u.dtype), u], axis=1)
    out = lax.conv_general_dilated(cat, w[:, None, :].astype(u.dtype), (1,), 'VALID',
                                   dimension_numbers=('NWC', 'WIO', 'NWC'),
                                   feature_group_count=ch)
    return out, cat[:, -(wlen - 1):]


def chunk_spatial_gate(u, v, w_s, b_s):
    bsz, seq = v.shape[0], v.shape[1]
    pad = (-seq) % GMLP_CHUNK
    nc = (seq + pad) // GMLP_CHUNK
    vp = jnp.pad(v, ((0, 0), (0, pad), (0, 0)))
    vp = vp.reshape(bsz, nc, GMLP_CHUNK, GMLP_GROUPS, GMLP_GROUP_DIM)
    causal = jnp.tril(jnp.ones((GMLP_CHUNK, GMLP_CHUNK), dtype=bool))
    wm = jnp.where(causal, w_s, 0).astype(v.dtype)
    z = jnp.einsum('gts,bcsgd->bctgd', wm, vp) + b_s.T[:, :, None].astype(v.dtype)
    z = z.reshape(bsz, nc * GMLP_CHUNK, GMLP_WIDTH)[:, :seq]
    return u * z


def trunk_layer(x, c, pos, conv_a_buf, conv_b_buf, kv_buf, lp):
    bsz, seq = x.shape[0], x.shape[1]
    mod = jax.nn.silu(c) @ lp['w_ada'] + lp['b_ada']
    shift1, scale1, gate1, shift2, scale2, gate2 = jnp.split(mod[:, None, :], 6, axis=-1)
    h = rms_norm(x, lp['norm1_g']) * (1 + scale1) + shift1
    q, k, v, a_in, b_in, c_in, g_in = split_columns(h @ lp['w_in'])

    q = rope_partial(rms_norm(q.reshape(bsz, seq, N_HEADS, HEAD_DIM), lp['q_norm_g']), pos)
    k = rope_partial(rms_norm(k.reshape(bsz, seq, N_KV_HEADS, HEAD_DIM), lp['k_norm_g']), pos)
    v = v.reshape(bsz, seq, N_KV_HEADS, HEAD_DIM)
    if kv_buf is None:
        attn = attn_prompt(q, k, v, lp['attn_sinks'])
        new_k, new_v = k[:, -WINDOW:], v[:, -WINDOW:]
    else:
        attn, new_k, new_v = attn_sample(q, k, v, kv_buf[0], kv_buf[1], lp['attn_sinks'])
    attn_out = attn @ lp['w_attn_o']

    a_val, a_gate = jnp.split(a_in, 2, axis=-1)
    glu = a_val * jax.nn.sigmoid(a_gate)
    a_conv, new_conv_a = causal_dwconv(glu, conv_a_buf, lp['conv_a_w'])
    a_conv = a_conv + lp['conv_a_b']
    a_out = jax.nn.silu(layer_norm(a_conv, lp['ln_a_g'], lp['ln_a_b'])) @ lp['w_a_out']

    b_gate, c_gate, h_sc = jnp.split(b_in, 3, axis=-1)
    sc, new_conv_b = causal_dwconv(c_gate * h_sc, conv_b_buf, lp['conv_b_w'])
    b_out = (b_gate * sc) @ lp['w_b_out']

    zc = jax.nn.gelu(c_in)
    u, vv = jnp.split(zc, 2, axis=-1)
    vv = layer_norm(vv, lp['ln_c_g'], lp['ln_c_b'])
    c_out = chunk_spatial_gate(u, vv, lp['w_spatial'], lp['b_spatial']) @ lp['w_c_out']

    g = jax.nn.sigmoid(g_in).reshape(bsz, seq, N_BRANCHES, D_MODEL)
    merged = (g[:, :, 0] * attn_out + g[:, :, 1] * a_out
              + g[:, :, 2] * b_out + g[:, :, 3] * c_out)
    x = x + gate1 * (merged @ lp['w_o'])

    h2 = rms_norm(x, lp['norm2_g']) * (1 + scale2) + shift2
    x = x + gate2 * (jnp.square(jax.nn.relu(h2 @ lp['w_up'])) @ lp['w_down'])
    return x, (new_conv_a, new_conv_b, new_k, new_v, vv)


def setup_inputs(seed: int = 0) -> dict:
    key = jax.random.key(seed)
    ks = jax.random.split(key, 40)
    f = jnp.float32

    def nrm(i, shape, scale=1.0):
        return jax.random.normal(ks[i], shape, f) * scale

    L = DEPTH
    return {
        'x_prompt': nrm(0, (BATCH, SEQ, D_MODEL)),
        'x_sample': nrm(1, (DEC_BATCH, DEC_SEQ, D_MODEL)),
        'c_prompt': nrm(2, (BATCH, D_MODEL)),
        'c_sample': nrm(3, (DEC_BATCH, D_MODEL)),
        'state_conv_a': nrm(4, (L, DEC_BATCH, CONF_CONV_W - 1, CONF_WIDTH), 0.5),
        'state_conv_b': nrm(5, (L, DEC_BATCH, SC_CONV_W - 1, SC_WIDTH)),
        'cache_k_win': nrm(6, (L, DEC_BATCH, WINDOW, N_KV_HEADS, HEAD_DIM)),
        'cache_v_win': nrm(7, (L, DEC_BATCH, WINDOW, N_KV_HEADS, HEAD_DIM)),
        'w_ada': nrm(8, (L, D_MODEL, 6 * D_MODEL), D_MODEL ** -0.5),
        'b_ada': nrm(9, (L, 6 * D_MODEL), 0.02),
        'norm1_g': 1.0 + nrm(10, (L, D_MODEL), 0.05),
        'norm2_g': 1.0 + nrm(11, (L, D_MODEL), 0.05),
        'w_in': nrm(12, (L, D_MODEL, IN_COLS), D_MODEL ** -0.5),
        'q_norm_g': 1.0 + nrm(13, (L, HEAD_DIM), 0.05),
        'k_norm_g': 1.0 + nrm(14, (L, HEAD_DIM), 0.05),
        'attn_sinks': nrm(15, (L, N_HEADS), 0.5),
        'w_attn_o': nrm(16, (L, N_HEADS * HEAD_DIM, D_MODEL), (N_HEADS * HEAD_DIM) ** -0.5),
        'conv_a_w': nrm(17, (L, CONF_CONV_W, CONF_WIDTH), CONF_CONV_W ** -0.5),
        'conv_a_b': nrm(18, (L, CONF_WIDTH), 0.02),
        'ln_a_g': 1.0 + nrm(19, (L, CONF_WIDTH), 0.05),
        'ln_a_b': nrm(20, (L, CONF_WIDTH), 0.02),
        'w_a_out': nrm(21, (L, CONF_WIDTH, D_MODEL), CONF_WIDTH ** -0.5),
        'conv_b_w': nrm(22, (L, SC_CONV_W, SC_WIDTH), SC_CONV_W ** -0.5),
        'w_b_out': nrm(23, (L, SC_WIDTH, D_MODEL), SC_WIDTH ** -0.5),
        'ln_c_g': 1.0 + nrm(24, (L, GMLP_WIDTH), 0.05),
        'ln_c_b': nrm(25, (L, GMLP_WIDTH), 0.02),
        'w_spatial': nrm(26, (L, GMLP_GROUPS, GMLP_CHUNK, GMLP_CHUNK), GMLP_CHUNK ** -0.5),
        'b_spatial': 1.0 + nrm(27, (L, GMLP_GROUPS, GMLP_CHUNK), 0.05),
        'w_c_out': nrm(28, (L, GMLP_WIDTH, D_MODEL), GMLP_WIDTH ** -0.5),
        'w_o': nrm(29, (L, D_MODEL, D_MODEL), D_MODEL ** -0.5),
        'w_up': nrm(30, (L, D_MODEL, D_FF), D_MODEL ** -0.5),
        'w_down': nrm(31, (L, D_FF, D_MODEL), D_FF ** -0.5),
    }


def reference(x_prompt, x_sample, c_prompt, c_sample, state_conv_a, state_conv_b,
              cache_k_win, cache_v_win, w_ada, b_ada, norm1_g, norm2_g, w_in,
              q_norm_g, k_norm_g, attn_sinks, w_attn_o, conv_a_w, conv_a_b, ln_a_g, ln_a_b,
              w_a_out, conv_b_w, w_b_out, ln_c_g, ln_c_b, w_spatial, b_spatial, w_c_out,
              w_o, w_up, w_down):
    pos_p = jnp.arange(x_prompt.shape[1])
    pos_s = PAST_LEN + jnp.arange(x_sample.shape[1])
    bp = x_prompt.shape[0]
    xp, xs = x_prompt, x_sample
    pa_l, sa_l, pb_l, sb_l, pk_l, sk_l, pv_l, sv_l, sg_l = [], [], [], [], [], [], [], [], []
    for l in range(DEPTH):
        lp = {'w_ada': w_ada[l], 'b_ada': b_ada[l], 'norm1_g': norm1_g[l], 'norm2_g': norm2_g[l],
              'w_in': w_in[l], 'q_norm_g': q_norm_g[l], 'k_norm_g': k_norm_g[l],
              'attn_sinks': attn_sinks[l], 'w_attn_o': w_attn_o[l], 'conv_a_w': conv_a_w[l],
              'conv_a_b': conv_a_b[l], 'ln_a_g': ln_a_g[l], 'ln_a_b': ln_a_b[l],
              'w_a_out': w_a_out[l], 'conv_b_w': conv_b_w[l], 'w_b_out': w_b_out[l],
              'ln_c_g': ln_c_g[l], 'ln_c_b': ln_c_b[l], 'w_spatial': w_spatial[l],
              'b_spatial': b_spatial[l], 'w_c_out': w_c_out[l], 'w_o': w_o[l],
              'w_up': w_up[l], 'w_down': w_down[l]}
        zero_a = jnp.zeros((bp, CONF_CONV_W - 1, CONF_WIDTH), xp.dtype)
        zero_b = jnp.zeros((bp, SC_CONV_W - 1, SC_WIDTH), xp.dtype)
        xp, (pa, pb, pk, pv, _) = trunk_layer(xp, c_prompt, pos_p, zero_a, zero_b, None, lp)
        xs, (sa, sb, sk, sv, sg) = trunk_layer(xs, c_sample, pos_s, state_conv_a[l],
                                               state_conv_b[l], (cache_k_win[l], cache_v_win[l]), lp)
        pa_l.append(pa); sa_l.append(sa); pb_l.append(pb); sb_l.append(sb)
        pk_l.append(pk); sk_l.append(sk); pv_l.append(pv); sv_l.append(sv); sg_l.append(sg)
    return (xp, xs, jnp.stack(pa_l), jnp.stack(sa_l), jnp.stack(pb_l), jnp.stack(sb_l),
            jnp.stack(pk_l), jnp.stack(sk_l), jnp.stack(pv_l), jnp.stack(sv_l), jnp.stack(sg_l))
```

```python
import functools

import jax
import jax.numpy as jnp
from jax import lax
from jax.experimental import pallas as pl
from jax.experimental.pallas import tpu as pltpu

F32 = jnp.float32
BF16 = jnp.bfloat16

D_MODEL = 1024
HEAD_DIM = 64
N_HEADS = 16
N_KV_HEADS = 4
Q_REP = 4
KV_DIM = N_KV_HEADS * HEAD_DIM
ROT_DIM = 16
ROPE_THETA = 500000.0
WINDOW = 128
ATTN_BLOCK = 128
KEY_SPAN = 2 * ATTN_BLOCK
CONV_W = 512
CONF_CONV_W = 31
SC_CONV_W = 3
GMLP_CHUNK = 128
GMLP_GROUPS = 8
GMLP_GROUP_DIM = 64
D_FF = 4 * D_MODEL
EPS = 1e-6
PAST_LEN = 16384
DEC_SEQ = 8
NEG = -1e30

Q0 = 0
K0 = Q0 + D_MODEL
V0 = K0 + KV_DIM
A0 = V0 + KV_DIM
B0 = A0 + 2 * CONV_W
C0 = B0 + 3 * CONV_W
G0 = C0 + 2 * CONV_W
QKV_COLS = A0

LANES = 128
SUBLANES = 8
VMEM_BYTES = 64 * 1024 * 1024

PROMPT_TILE = 256
MLP_TILE = 512
SAMPLE_SEQS = 32
SAMPLE_ATTN_SEQS = 16
CONV_ROWS = 32
A_PAD = 32
B_PAD = 8
MOD_ROWS = 136


def _dot(a, b):
    return jnp.dot(a, b, preferred_element_type=F32)


def _rms_mod(x, g, scale, shift):
    ms = jnp.mean(x * x, axis=-1, keepdims=True)
    y = x * lax.rsqrt(ms + EPS)
    return (y * g) * (1.0 + scale) + shift


def _layer_norm(x, g, b):
    mu = jnp.mean(x, axis=-1, keepdims=True)
    xc = x - mu
    var = jnp.mean(xc * xc, axis=-1, keepdims=True)
    return xc * lax.rsqrt(var + EPS) * g + b


def _head_norm_rope(xc, gain, seg_mean, cos_t, sin_a, sin_b):
    ms = _dot((xc * xc).astype(BF16), seg_mean)
    y = xc * lax.rsqrt(ms + EPS) * gain
    return (y * cos_t + pltpu.roll(y, LANES - ROT_DIM // 2, 1) * sin_a
            + pltpu.roll(y, ROT_DIM // 2, 1) * sin_b)


def _qkv_heads(qkv, qkg_ref, seg_mean, cos_t, sin_a, sin_b):
    def chunk(c, gain):
        return _head_norm_rope(qkv[:, c * LANES:(c + 1) * LANES], gain, seg_mean,
                               cos_t, sin_a, sin_b)
    q = jnp.concatenate([chunk(c, qkg_ref[0:1, :]) for c in range(D_MODEL // LANES)], axis=1)
    k = jnp.concatenate([chunk(K0 // LANES + c, qkg_ref[1:2, :])
                         for c in range(KV_DIM // LANES)], axis=1)
    return q, k, qkv[:, V0:V0 + KV_DIM]


def _kv_head_masks(scale):
    lane = lax.broadcasted_iota(jnp.int32, (1, KV_DIM), 1)
    return [jnp.where((lane >= g * HEAD_DIM) & (lane < (g + 1) * HEAD_DIM), scale, 0.0).astype(F32)
            for g in range(N_KV_HEADS)]


def _softmax_with_sink(s, sink):
    m = jnp.maximum(jnp.max(s, axis=-1, keepdims=True), sink)
    e = jnp.exp(s - m)
    den = jnp.sum(e, axis=-1, keepdims=True) + jnp.exp(sink - m)
    return e * (1.0 / den)


def _mod_kernel(c_ref, w_ref, b_ref, o_ref):
    c = c_ref[...]
    s = (c * jax.nn.sigmoid(c)).astype(BF16)
    o_ref[...] = _dot(s, w_ref[...].astype(BF16)) + b_ref[...]


def _modulation(c_all, w_ada, b_ada):
    depth = w_ada.shape[0]
    n_chunks = w_ada.shape[2] // D_MODEL
    return pl.pallas_call(
        _mod_kernel,
        out_shape=jax.ShapeDtypeStruct((depth, MOD_ROWS, n_chunks * D_MODEL), F32),
        grid=(depth, n_chunks),
        in_specs=[
            pl.BlockSpec((MOD_ROWS, D_MODEL), lambda l, j: (0, 0)),
            pl.BlockSpec((None, D_MODEL, D_MODEL), lambda l, j: (l, 0, j)),
            pl.BlockSpec((None, 1, D_MODEL), lambda l, j: (l, 0, j)),
        ],
        out_specs=pl.BlockSpec((None, MOD_ROWS, D_MODEL), lambda l, j: (l, 0, j)),
        compiler_params=pltpu.CompilerParams(
            dimension_semantics=("arbitrary", "arbitrary"),
            vmem_limit_bytes=32 * 1024 * 1024),
        name="adaln_modulation",
    )(c_all, w_ada, b_ada.reshape(depth, 1, -1))


def _prompt_attention_block(q_blk, k_win, v_win, bias, sink_ref, cm_scaled, cm_plain):
    tq = q_blk.shape[0]
    pieces = [q_blk[:, r * KV_DIM:(r + 1) * KV_DIM] * cm_scaled[g]
              for r in range(Q_REP) for g in range(N_KV_HEADS)]
    qbd = jnp.concatenate(pieces, axis=0).astype(BF16)
    s = lax.dot_general(qbd, k_win, (((1,), (1,)), ((), ())), preferred_element_type=F32)
    probs = []
    for r in range(Q_REP):
        for g in range(N_KV_HEADS):
            i = r * N_KV_HEADS + g
            sp = s[i * tq:(i + 1) * tq] + bias
            probs.append(_softmax_with_sink(sp, sink_ref[g * Q_REP + r]).astype(BF16))
    o = _dot(jnp.concatenate(probs, axis=0), v_win)
    slabs = []
    for r in range(Q_REP):
        acc = None
        for g in range(N_KV_HEADS):
            i = r * N_KV_HEADS + g
            part = o[i * tq:(i + 1) * tq] * cm_plain[g]
            acc = part if acc is None else acc + part
        slabs.append(acc)
    return jnp.concatenate(slabs, axis=1)


def _mixer_prompt_kernel(x_ref, mod_ref, rope_ref, sink_ref, n1g_ref, qkg_ref, caw_ref, v512_ref,
                         wpair_ref, bsp_ref, segm_ref,
                         w_in_ref, w_ao_ref, w_aout_ref, w_bout_ref, w_cout_ref, w_o_ref,
                         x1_ref, ca_ref, cb_ref, ko_ref, vo_ref,
                         kcat_s, vcat_s, acat_s, bcat_s):
    tile = x_ref.shape[0]
    j = pl.program_id(1)
    last = j == pl.num_programs(1) - 1

    @pl.when(j == 0)
    def _():
        kcat_s[0:ATTN_BLOCK, :] = jnp.zeros((ATTN_BLOCK, KV_DIM), BF16)
        vcat_s[0:ATTN_BLOCK, :] = jnp.zeros((ATTN_BLOCK, KV_DIM), BF16)
        acat_s[0:A_PAD, :] = jnp.zeros((A_PAD, CONV_W), F32)
        bcat_s[0:B_PAD, :] = jnp.zeros((B_PAD, CONV_W), F32)

    x = x_ref[...]
    shift1, scale1, gate1 = mod_ref[0:1, :], mod_ref[1:2, :], mod_ref[2:3, :]
    h = _rms_mod(x, n1g_ref[...], scale1, shift1).astype(BF16)

    qkv = _dot(h, w_in_ref[:, Q0:QKV_COLS])
    q, k, v = _qkv_heads(qkv, qkg_ref, segm_ref[...], rope_ref[0], rope_ref[1], rope_ref[2])
    kcat_s[ATTN_BLOCK:ATTN_BLOCK + tile, :] = k.astype(BF16)
    vcat_s[ATTN_BLOCK:ATTN_BLOCK + tile, :] = v.astype(BF16)

    @pl.when(last)
    def _():
        ko_ref[...] = k[tile - WINDOW:tile]
        vo_ref[...] = v[tile - WINDOW:tile]

    tt = lax.broadcasted_iota(jnp.int32, (ATTN_BLOCK, KEY_SPAN), 0)
    jj = lax.broadcasted_iota(jnp.int32, (ATTN_BLOCK, KEY_SPAN), 1)
    band = (jj >= tt) & (jj <= tt + WINDOW)
    bias_rest = jnp.where(band, 0.0, NEG).astype(F32)
    first_key = jnp.where(j == 0, ATTN_BLOCK, 0)
    bias_first = jnp.where(band & (jj >= first_key), 0.0, NEG).astype(F32)
    cm_scaled = _kv_head_masks(HEAD_DIM ** -0.5)
    cm_plain = _kv_head_masks(1.0)
    blocks = []
    for i in range(tile // ATTN_BLOCK):
        lo = i * ATTN_BLOCK
        blocks.append(_prompt_attention_block(
            q[lo:lo + ATTN_BLOCK], kcat_s[lo:lo + KEY_SPAN, :], vcat_s[lo:lo + KEY_SPAN, :],
            bias_first if i == 0 else bias_rest, sink_ref, cm_scaled, cm_plain))
    attn = jnp.concatenate(blocks, axis=0)
    kcat_s[0:ATTN_BLOCK, :] = kcat_s[tile:tile + ATTN_BLOCK, :]
    vcat_s[0:ATTN_BLOCK, :] = vcat_s[tile:tile + ATTN_BLOCK, :]
    attn_out = _dot(attn.astype(BF16), w_ao_ref[...])
    merged = jax.nn.sigmoid(_dot(h, w_in_ref[:, G0:G0 + D_MODEL])) * attn_out

    a_in = _dot(h, w_in_ref[:, A0:A0 + 2 * CONV_W])
    glu = a_in[:, :CONV_W] * jax.nn.sigmoid(a_in[:, CONV_W:])
    acat_s[A_PAD:A_PAD + tile, :] = glu
    base = A_PAD - (CONF_CONV_W - 1)
    conv_rows = []
    for c in range(tile // CONV_ROWS):
        acc = jnp.broadcast_to(v512_ref[0:1, :], (CONV_ROWS, CONV_W))
        for t in range(CONF_CONV_W):
            lo = base + c * CONV_ROWS + t
            acc = acc + caw_ref[t:t + 1, :] * acat_s[lo:lo + CONV_ROWS, :]
        conv_rows.append(acc)
    a_conv = jnp.concatenate(conv_rows, axis=0)

    @pl.when(last)
    def _():
        ca_ref[...] = acat_s[base + tile:A_PAD + tile, :]

    acat_s[0:A_PAD, :] = acat_s[tile:tile + A_PAD, :]
    a_act = jax.nn.silu(_layer_norm(a_conv, v512_ref[1:2, :], v512_ref[2:3, :]))
    a_out = _dot(a_act.astype(BF16), w_aout_ref[...])
    merged = merged + jax.nn.sigmoid(_dot(h, w_in_ref[:, G0 + D_MODEL:G0 + 2 * D_MODEL])) * a_out

    b_in = _dot(h, w_in_ref[:, B0:B0 + 3 * CONV_W])
    bcat_s[B_PAD:B_PAD + tile, :] = b_in[:, CONV_W:2 * CONV_W] * b_in[:, 2 * CONV_W:]
    base_b = B_PAD - (SC_CONV_W - 1)
    sc = None
    for t in range(SC_CONV_W):
        term = v512_ref[5 + t:6 + t, :] * bcat_s[base_b + t:base_b + t + tile, :]
        sc = term if sc is None else sc + term

    @pl.when(last)
    def _():
        cb_ref[...] = bcat_s[base_b + tile:B_PAD + tile, :]

    bcat_s[0:B_PAD, :] = bcat_s[tile:tile + B_PAD, :]
    b_out = _dot((b_in[:, :CONV_W] * sc).astype(BF16), w_bout_ref[...])
    merged = merged + jax.nn.sigmoid(_dot(h, w_in_ref[:, G0 + 2 * D_MODEL:G0 + 3 * D_MODEL])) * b_out

    zc = jax.nn.gelu(_dot(h, w_in_ref[:, C0:C0 + 2 * CONV_W]))
    u = zc[:, :CONV_W]
    vv = _layer_norm(zc[:, CONV_W:], v512_ref[3:4, :], v512_ref[4:5, :])
    row = lax.broadcasted_iota(jnp.int32, (GMLP_CHUNK, 2 * GMLP_CHUNK), 0)
    col = lax.broadcasted_iota(jnp.int32, (GMLP_CHUNK, 2 * GMLP_CHUNK), 1)
    causal = (col & (GMLP_CHUNK - 1)) <= row
    n_pairs = GMLP_GROUPS // 2
    wpair = [jnp.where(causal, wpair_ref[p], 0.0).astype(BF16) for p in range(n_pairs)]
    low = lax.broadcasted_iota(jnp.int32, (1, LANES), 1) < GMLP_GROUP_DIM
    gated = []
    for c in range(tile // GMLP_CHUNK):
        vch = vv[c * GMLP_CHUNK:(c + 1) * GMLP_CHUNK]
        zs = []
        for p in range(n_pairs):
            blk = vch[:, p * LANES:(p + 1) * LANES]
            rhs = jnp.concatenate([jnp.where(low, blk, 0.0), jnp.where(low, 0.0, blk)], axis=0)
            zs.append(_dot(wpair[p], rhs.astype(BF16)))
        z = jnp.concatenate(zs, axis=1) + bsp_ref[...]
        gated.append(u[c * GMLP_CHUNK:(c + 1) * GMLP_CHUNK] * z)
    c_out = _dot(jnp.concatenate(gated, axis=0).astype(BF16), w_cout_ref[...])
    merged = merged + jax.nn.sigmoid(_dot(h, w_in_ref[:, G0 + 3 * D_MODEL:G0 + 4 * D_MODEL])) * c_out

    x1_ref[...] = x + gate1 * _dot(merged.astype(BF16), w_o_ref[...])


def _const_spec(shape):
    zeros = (0,) * len(shape)
    return pl.BlockSpec(shape, lambda *_: zeros, pipeline_mode=pl.Buffered(1))


def _mixer_prompt(x, mod, rope, lw):
    bsz, seq, _ = x.shape
    tile = PROMPT_TILE
    small = [lw["n1g"], lw["qkg"], lw["caw"], lw["v512"], lw["wpair"], lw["bsp"], lw["segm"]]
    weights = [lw["w_in"], lw["w_ao"], lw["w_aout"], lw["w_bout"], lw["w_cout"], lw["w_o"]]
    in_specs = (
        [pl.BlockSpec((None, tile, D_MODEL), lambda b, j: (b, j, 0)),
         pl.BlockSpec((None, 6, D_MODEL), lambda b, j: (b, 0, 0)),
         pl.BlockSpec((3, tile, LANES), lambda b, j: (0, j, 0)),
         pl.BlockSpec(memory_space=pltpu.SMEM)]
        + [_const_spec(a.shape) for a in small]
        + [_const_spec(a.shape) for a in weights])
    out_shape = (
        jax.ShapeDtypeStruct((bsz, seq, D_MODEL), F32),
        jax.ShapeDtypeStruct((bsz, CONF_CONV_W - 1, CONV_W), F32),
        jax.ShapeDtypeStruct((bsz, SC_CONV_W - 1, CONV_W), F32),
        jax.ShapeDtypeStruct((bsz, WINDOW, KV_DIM), F32),
        jax.ShapeDtypeStruct((bsz, WINDOW, KV_DIM), F32),
    )
    out_specs = (
        pl.BlockSpec((None, tile, D_MODEL), lambda b, j: (b, j, 0)),
        pl.BlockSpec((None, CONF_CONV_W - 1, CONV_W), lambda b, j: (b, 0, 0)),
        pl.BlockSpec((None, SC_CONV_W - 1, CONV_W), lambda b, j: (b, 0, 0)),
        pl.BlockSpec((None, WINDOW, KV_DIM), lambda b, j: (b, 0, 0)),
        pl.BlockSpec((None, WINDOW, KV_DIM), lambda b, j: (b, 0, 0)),
    )
    return pl.pallas_call(
        _mixer_prompt_kernel,
        out_shape=out_shape,
        grid=(bsz, seq // tile),
        in_specs=in_specs,
        out_specs=out_specs,
        scratch_shapes=[
            pltpu.VMEM((ATTN_BLOCK + tile, KV_DIM), BF16),
            pltpu.VMEM((ATTN_BLOCK + tile, KV_DIM), BF16),
            pltpu.VMEM((A_PAD + tile, CONV_W), F32),
            pltpu.VMEM((B_PAD + tile, CONV_W), F32),
        ],
        compiler_params=pltpu.CompilerParams(
            dimension_semantics=("arbitrary", "arbitrary"),
            vmem_limit_bytes=VMEM_BYTES - 8 * 1024 * 1024),
        name="mixer_prompt",
    )(x, mod, rope, lw["sinks"], *small, *weights)


SAMPLE_NEW_ROWS = 16


def _sample_attn_kernel(x_ref, mod_ref, rope_ref, sink_ref, n1g_ref, qkg_ref, segm_ref,
                        ck_ref, cv_ref, w_qkv_ref, w_ao_ref,
                        ao_ref, nk_ref, nv_ref, kcat_s, vcat_s):
    nseq, tq, _ = x_ref.shape
    rows = nseq * tq
    cache_lo = SAMPLE_NEW_ROWS
    cache_hi = cache_lo + WINDOW

    x = x_ref[...]
    shift1, scale1 = mod_ref[:, 0:1, :], mod_ref[:, 1:2, :]
    h = _rms_mod(x, n1g_ref[...], scale1, shift1).reshape(rows, D_MODEL).astype(BF16)

    qkv = _dot(h, w_qkv_ref[...])
    q, k, v = _qkv_heads(qkv, qkg_ref, segm_ref[...], rope_ref[0], rope_ref[1], rope_ref[2])
    k3 = k.reshape(nseq, tq, KV_DIM)
    v3 = v.reshape(nseq, tq, KV_DIM)
    keep = WINDOW - tq
    nk_ref[:, 0:keep, :] = ck_ref[:, tq:WINDOW, :]
    nk_ref[:, keep:WINDOW, :] = k3
    nv_ref[:, 0:keep, :] = cv_ref[:, tq:WINDOW, :]
    nv_ref[:, keep:WINDOW, :] = v3
    pad = jnp.zeros((nseq, SAMPLE_NEW_ROWS - tq, KV_DIM), F32)
    kcat_s[:, 0:cache_lo, :] = jnp.concatenate([k3, pad], axis=1).astype(BF16)
    vcat_s[:, 0:cache_lo, :] = jnp.concatenate([v3, pad], axis=1).astype(BF16)
    kcat_s[:, cache_lo:cache_hi, :] = ck_ref[...].astype(BF16)
    vcat_s[:, cache_lo:cache_hi, :] = cv_ref[...].astype(BF16)
    tail = jnp.zeros((nseq, KEY_SPAN - cache_hi, KV_DIM), BF16)
    kcat_s[:, cache_hi:KEY_SPAN, :] = tail
    vcat_s[:, cache_hi:KEY_SPAN, :] = tail

    n_q = N_HEADS * tq
    tt = lax.broadcasted_iota(jnp.int32, (n_q, KEY_SPAN), 0) & (tq - 1)
    ii = lax.broadcasted_iota(jnp.int32, (n_q, KEY_SPAN), 1)
    visible = ((ii < tq) & (ii <= tt)) | ((ii >= cache_lo + tt) & (ii < cache_hi))
    bias = jnp.where(visible, 0.0, NEG).astype(F32)
    cm_scaled = _kv_head_masks(HEAD_DIM ** -0.5)
    cm_plain = _kv_head_masks(1.0)
    q3 = q.reshape(nseq, tq, D_MODEL)
    qbd = jnp.concatenate([q3[:, :, r * KV_DIM:(r + 1) * KV_DIM] * cm_scaled[g]
                           for r in range(Q_REP) for g in range(N_KV_HEADS)], axis=1).astype(BF16)
    s = jnp.einsum("bqc,bkc->bqk", qbd, kcat_s[...], preferred_element_type=F32) + bias
    sink = jnp.concatenate([jnp.full((tq, 1), sink_ref[g * Q_REP + r], F32)
                            for r in range(Q_REP) for g in range(N_KV_HEADS)], axis=0)
    p = _softmax_with_sink(s, sink).astype(BF16)
    o = jnp.einsum("bqk,bkc->bqc", p, vcat_s[...], preferred_element_type=F32)
    slabs = []
    for r in range(Q_REP):
        acc = None
        for g in range(N_KV_HEADS):
            i = r * N_KV_HEADS + g
            part = o[:, i * tq:(i + 1) * tq, :] * cm_plain[g]
            acc = part if acc is None else acc + part
        slabs.append(acc)
    attn = jnp.concatenate(slabs, axis=2).reshape(rows, D_MODEL)
    ao_ref[...] = _dot(attn.astype(BF16), w_ao_ref[...]).reshape(nseq, tq, D_MODEL)


def _sample_rest_kernel(x_ref, mod_ref, ao_ref, n1g_ref, caw_ref, v512_ref, wc_ref, bsp_ref,
                        sta_ref, stb_ref, w_in_ref, w_aout_ref, w_bout_ref, w_cout_ref, w_o_ref,
                        x1_ref, na_ref, nb_ref, vv_ref, acat_s, bcat_s):
    nseq, tq, _ = x_ref.shape
    rows = nseq * tq

    x = x_ref[...]
    shift1, scale1, gate1 = mod_ref[:, 0:1, :], mod_ref[:, 1:2, :], mod_ref[:, 2:3, :]
    h = _rms_mod(x, n1g_ref[...], scale1, shift1).reshape(rows, D_MODEL).astype(BF16)
    attn_out = ao_ref[...].reshape(rows, D_MODEL)
    merged = jax.nn.sigmoid(_dot(h, w_in_ref[:, G0:G0 + D_MODEL])) * attn_out

    a_in = _dot(h, w_in_ref[:, A0:A0 + 2 * CONV_W])
    glu = a_in[:, :CONV_W] * jax.nn.sigmoid(a_in[:, CONV_W:])
    hist_a = CONF_CONV_W - 1
    base = A_PAD - hist_a
    acat_s[:, base:A_PAD, :] = sta_ref[...]
    acat_s[:, A_PAD:A_PAD + tq, :] = glu.reshape(nseq, tq, CONV_W)
    a_conv = jnp.broadcast_to(v512_ref[0:1, :], (nseq, tq, CONV_W))
    for t in range(CONF_CONV_W):
        a_conv = a_conv + caw_ref[t:t + 1, :] * acat_s[:, base + t:base + t + tq, :]
    na_ref[...] = acat_s[:, base + tq:A_PAD + tq, :]
    a_act = jax.nn.silu(_layer_norm(a_conv, v512_ref[1:2, :], v512_ref[2:3, :]))
    a_out = _dot(a_act.reshape(rows, CONV_W).astype(BF16), w_aout_ref[...])
    merged = merged + jax.nn.sigmoid(_dot(h, w_in_ref[:, G0 + D_MODEL:G0 + 2 * D_MODEL])) * a_out

    b_in = _dot(h, w_in_ref[:, B0:B0 + 3 * CONV_W])
    hist_b = SC_CONV_W - 1
    base_b = B_PAD - hist_b
    bcat_s[:, base_b:B_PAD, :] = stb_ref[...]
    bcat_s[:, B_PAD:B_PAD + tq, :] = (b_in[:, CONV_W:2 * CONV_W]
                                      * b_in[:, 2 * CONV_W:]).reshape(nseq, tq, CONV_W)
    sc = None
    for t in range(SC_CONV_W):
        term = v512_ref[5 + t:6 + t, :] * bcat_s[:, base_b + t:base_b + t + tq, :]
        sc = term if sc is None else sc + term
    nb_ref[...] = bcat_s[:, base_b + tq:B_PAD + tq, :]
    b_out = _dot((b_in[:, :CONV_W] * sc.reshape(rows, CONV_W)).astype(BF16), w_bout_ref[...])
    merged = merged + jax.nn.sigmoid(_dot(h, w_in_ref[:, G0 + 2 * D_MODEL:G0 + 3 * D_MODEL])) * b_out

    zc = jax.nn.gelu(_dot(h, w_in_ref[:, C0:C0 + 2 * CONV_W]))
    u3 = zc[:, :CONV_W].reshape(nseq, tq, CONV_W)
    vv3 = _layer_norm(zc[:, CONV_W:], v512_ref[3:4, :], v512_ref[4:5, :]).reshape(nseq, tq, CONV_W)
    vv_ref[...] = vv3
    t_idx = lax.broadcasted_iota(jnp.int32, (tq, CONV_W), 0)
    z = jnp.broadcast_to(bsp_ref[...], (nseq, tq, CONV_W))
    for src in range(tq):
        coef = jnp.where(t_idx >= src, wc_ref[src], 0.0)
        z = z + coef * vv3[:, src:src + 1, :]
    c_out = _dot((u3 * z).reshape(rows, CONV_W).astype(BF16), w_cout_ref[...])
    merged = merged + jax.nn.sigmoid(_dot(h, w_in_ref[:, G0 + 3 * D_MODEL:G0 + 4 * D_MODEL])) * c_out

    y = _dot(merged.astype(BF16), w_o_ref[...]).reshape(nseq, tq, D_MODEL)
    x1_ref[...] = x + gate1 * y


def _seq_spec(nseq, r, c):
    return pl.BlockSpec((nseq, r, c), lambda i: (i, 0, 0))


def _sample_attention(x, mod, rope, cache_k, cache_v, lw):
    nbatch, tq, _ = x.shape
    nseq = SAMPLE_ATTN_SEQS
    small = [lw["n1g"], lw["qkg"], lw["segm"]]
    in_specs = (
        [_seq_spec(nseq, tq, D_MODEL), _seq_spec(nseq, 6, D_MODEL),
         _const_spec(rope.shape), pl.BlockSpec(memory_space=pltpu.SMEM)]
        + [_const_spec(a.shape) for a in small]
        + [_seq_spec(nseq, WINDOW, KV_DIM), _seq_spec(nseq, WINDOW, KV_DIM),
           _const_spec((D_MODEL, QKV_COLS)), _const_spec(lw["w_ao"].shape)])
    out_shape = (
        jax.ShapeDtypeStruct((nbatch, tq, D_MODEL), F32),
        jax.ShapeDtypeStruct((nbatch, WINDOW, KV_DIM), F32),
        jax.ShapeDtypeStruct((nbatch, WINDOW, KV_DIM), F32),
    )
    out_specs = (_seq_spec(nseq, tq, D_MODEL), _seq_spec(nseq, WINDOW, KV_DIM),
                 _seq_spec(nseq, WINDOW, KV_DIM))
    return pl.pallas_call(
        _sample_attn_kernel,
        out_shape=out_shape,
        grid=(nbatch // nseq,),
        in_specs=in_specs,
        out_specs=out_specs,
        scratch_shapes=[
            pltpu.VMEM((nseq, KEY_SPAN, KV_DIM), BF16),
            pltpu.VMEM((nseq, KEY_SPAN, KV_DIM), BF16),
        ],
        compiler_params=pltpu.CompilerParams(
            dimension_semantics=("arbitrary",),
            vmem_limit_bytes=VMEM_BYTES - 16 * 1024 * 1024),
        name="sample_attention",
    )(x, mod, rope, lw["sinks"], *small, cache_k, cache_v, lw["w_in"], lw["w_ao"])


def _sample_rest(x, mod, attn_out, state_a, state_b, lw):
    nbatch, tq, _ = x.shape
    nseq = SAMPLE_SEQS
    small = [lw["n1g"], lw["caw"], lw["v512"], lw["wc8"], lw["bsp8"]]
    weights = [lw["w_in"], lw["w_aout"], lw["w_bout"], lw["w_cout"], lw["w_o"]]
    in_specs = (
        [_seq_spec(nseq, tq, D_MODEL), _seq_spec(nseq, 6, D_MODEL), _seq_spec(nseq, tq, D_MODEL)]
        + [_const_spec(a.shape) for a in small]
        + [_seq_spec(nseq, CONF_CONV_W - 1, CONV_W), _seq_spec(nseq, SC_CONV_W - 1, CONV_W)]
        + [_const_spec(a.shape) for a in weights])
    out_shape = (
        jax.ShapeDtypeStruct((nbatch, tq, D_MODEL), F32),
        jax.ShapeDtypeStruct((nbatch, CONF_CONV_W - 1, CONV_W), F32),
        jax.ShapeDtypeStruct((nbatch, SC_CONV_W - 1, CONV_W), F32),
        jax.ShapeDtypeStruct((nbatch, tq, CONV_W), F32),
    )
    out_specs = (_seq_spec(nseq, tq, D_MODEL), _seq_spec(nseq, CONF_CONV_W - 1, CONV_W),
                 _seq_spec(nseq, SC_CONV_W - 1, CONV_W), _seq_spec(nseq, tq, CONV_W))
    return pl.pallas_call(
        _sample_rest_kernel,
        out_shape=out_shape,
        grid=(nbatch // nseq,),
        in_specs=in_specs,
        out_specs=out_specs,
        scratch_shapes=[
            pltpu.VMEM((nseq, A_PAD + tq, CONV_W), F32),
            pltpu.VMEM((nseq, B_PAD + tq, CONV_W), F32),
        ],
        compiler_params=pltpu.CompilerParams(
            dimension_semantics=("arbitrary",),
            vmem_limit_bytes=VMEM_BYTES - 8 * 1024 * 1024),
        name="sample_rest",
    )(x, mod, attn_out, *small, state_a, state_b, *weights)


def _mlp_kernel(x_ref, mod_ref, g_ref, w_up_ref, w_down_ref, o_ref):
    x = x_ref[...]
    if x.ndim == 3:
        shift2, scale2, gate2 = mod_ref[:, 3:4, :], mod_ref[:, 4:5, :], mod_ref[:, 5:6, :]
    else:
        shift2, scale2, gate2 = mod_ref[3:4, :], mod_ref[4:5, :], mod_ref[5:6, :]
    h2 = _rms_mod(x, g_ref[...], scale2, shift2)
    up = _dot(h2.reshape(-1, D_MODEL).astype(BF16), w_up_ref[...])
    act = jnp.square(jnp.maximum(up, 0.0)).astype(BF16)
    y = _dot(act, w_down_ref[...]).reshape(x.shape)
    o_ref[...] = x + gate2 * y


def _mlp(x, mod, lw, per_sequence):
    if per_sequence:
        nbatch, tq, _ = x.shape
        nseq = MLP_TILE // tq
        grid = (nbatch // nseq,)
        x_spec = pl.BlockSpec((nseq, tq, D_MODEL), lambda i: (i, 0, 0))
        mod_spec = pl.BlockSpec((nseq, 6, D_MODEL), lambda i: (i, 0, 0))
    else:
        bsz, seq, _ = x.shape
        grid = (bsz, seq // MLP_TILE)
        x_spec = pl.BlockSpec((None, MLP_TILE, D_MODEL), lambda b, j: (b, j, 0))
        mod_spec = pl.BlockSpec((None, 6, D_MODEL), lambda b, j: (b, 0, 0))
    return pl.pallas_call(
        _mlp_kernel,
        out_shape=jax.ShapeDtypeStruct(x.shape, F32),
        grid=grid,
        in_specs=[x_spec, mod_spec, _const_spec(lw["n2g"].shape),
                  _const_spec(lw["w_up"].shape), _const_spec(lw["w_down"].shape)],
        out_specs=x_spec,
        compiler_params=pltpu.CompilerParams(
            dimension_semantics=("arbitrary",) * len(grid),
            vmem_limit_bytes=VMEM_BYTES - 16 * 1024 * 1024),
        name="mlp_sample" if per_sequence else "mlp_prompt",
    )(x, mod, lw["n2g"], lw["w_up"], lw["w_down"])


def _rope_tables(pos):
    half = ROT_DIM // 2
    inv_freq = jnp.power(ROPE_THETA, -jnp.arange(half, dtype=F32) * 2.0 / ROT_DIM)
    ang = pos.astype(F32)[:, None] * inv_freq[None, :]
    cos, sin = jnp.cos(ang), jnp.sin(ang)
    lane = jnp.arange(LANES) % HEAD_DIM
    f = lane % half
    cos_t = jnp.where(lane < ROT_DIM, cos[:, f], 1.0)
    sin_a = jnp.where(lane < half, -sin[:, f], 0.0)
    sin_b = jnp.where((lane >= half) & (lane < ROT_DIM), sin[:, f], 0.0)
    return jnp.stack([cos_t, sin_a, sin_b]).astype(F32)


def _layer_weights(l, p):
    w_in = p["w_in"][l]
    wq = w_in[:, :D_MODEL].reshape(D_MODEL, N_KV_HEADS, Q_REP, HEAD_DIM)
    wq = wq.transpose(0, 2, 1, 3).reshape(D_MODEL, D_MODEL)
    w_in = jnp.concatenate([wq, w_in[:, D_MODEL:]], axis=1).astype(BF16)
    w_ao = p["w_attn_o"][l].reshape(N_KV_HEADS, Q_REP, HEAD_DIM, D_MODEL)
    w_ao = w_ao.transpose(1, 0, 2, 3).reshape(D_MODEL, D_MODEL).astype(BF16)
    ws = p["w_spatial"][l]
    wpair = ws.reshape(GMLP_GROUPS // 2, 2, GMLP_CHUNK, GMLP_CHUNK).transpose(0, 2, 1, 3)
    wpair = wpair.reshape(GMLP_GROUPS // 2, GMLP_CHUNK, 2 * GMLP_CHUNK)
    bsp = jnp.repeat(p["b_spatial"][l].T, GMLP_GROUP_DIM, axis=1)
    wc8 = jnp.repeat(ws[:, :DEC_SEQ, :DEC_SEQ].transpose(2, 1, 0), GMLP_GROUP_DIM, axis=2)
    lane_head = jnp.arange(LANES) // HEAD_DIM
    segm = jnp.where(lane_head[:, None] == lane_head[None, :], 1.0 / HEAD_DIM, 0.0).astype(BF16)
    caw = jnp.concatenate([p["conv_a_w"][l], jnp.zeros((1, CONV_W), F32)], axis=0)
    v512 = jnp.concatenate([p["conv_a_b"][l][None], p["ln_a_g"][l][None], p["ln_a_b"][l][None],
                            p["ln_c_g"][l][None], p["ln_c_b"][l][None], p["conv_b_w"][l]], axis=0)
    return {
        "w_in": w_in, "w_ao": w_ao,
        "w_aout": p["w_a_out"][l].astype(BF16), "w_bout": p["w_b_out"][l].astype(BF16),
        "w_cout": p["w_c_out"][l].astype(BF16), "w_o": p["w_o"][l].astype(BF16),
        "w_up": p["w_up"][l].astype(BF16), "w_down": p["w_down"][l].astype(BF16),
        "n1g": p["norm1_g"][l][None], "n2g": p["norm2_g"][l][None],
        "qkg": jnp.stack([jnp.tile(p["q_norm_g"][l], LANES // HEAD_DIM),
                          jnp.tile(p["k_norm_g"][l], LANES // HEAD_DIM)]),
        "caw": caw, "v512": v512, "wpair": wpair, "bsp": bsp, "wc8": wc8, "bsp8": bsp[:DEC_SEQ],
        "segm": segm, "sinks": p["attn_sinks"][l],
    }


def kernel(x_prompt, x_sample, c_prompt, c_sample, state_conv_a, state_conv_b, cache_k_win, cache_v_win, w_ada, b_ada, norm1_g, norm2_g, w_in, q_norm_g, k_norm_g, attn_sinks, w_attn_o, conv_a_w, conv_a_b, ln_a_g, ln_a_b, w_a_out, conv_b_w, w_b_out, ln_c_g, ln_c_b, w_spatial, b_spatial, w_c_out, w_o, w_up, w_down):
    params = dict(w_in=w_in, w_attn_o=w_attn_o, w_spatial=w_spatial, b_spatial=b_spatial,
                  conv_a_w=conv_a_w, conv_a_b=conv_a_b, ln_a_g=ln_a_g, ln_a_b=ln_a_b,
                  ln_c_g=ln_c_g, ln_c_b=ln_c_b, conv_b_w=conv_b_w, w_a_out=w_a_out,
                  w_b_out=w_b_out, w_c_out=w_c_out, w_o=w_o, w_up=w_up, w_down=w_down,
                  norm1_g=norm1_g, norm2_g=norm2_g, q_norm_g=q_norm_g, k_norm_g=k_norm_g,
                  attn_sinks=attn_sinks)
    depth = w_in.shape[0]
    bp, seq, _ = x_prompt.shape
    bs, tq, _ = x_sample.shape

    c_all = jnp.concatenate([c_prompt, c_sample,
                             jnp.zeros((MOD_ROWS - bp - bs, D_MODEL), F32)], axis=0)
    mod_all = _modulation(c_all, w_ada, b_ada)
    rope_p = _rope_tables(jnp.arange(seq))
    rope_s = jnp.tile(_rope_tables(PAST_LEN + jnp.arange(tq)), (1, SAMPLE_ATTN_SEQS, 1))
    cache_k = cache_k_win.reshape(depth, bs, WINDOW, KV_DIM)
    cache_v = cache_v_win.reshape(depth, bs, WINDOW, KV_DIM)

    xp, xs = x_prompt, x_sample
    outs = [[] for _ in range(9)]
    for l in range(depth):
        lw = _layer_weights(l, params)
        mod_p = mod_all[l, :bp].reshape(bp, 6, D_MODEL)
        mod_s = mod_all[l, bp:bp + bs].reshape(bs, 6, D_MODEL)
        xp, pa, pb, pk, pv = _mixer_prompt(xp, mod_p, rope_p, lw)
        xp = _mlp(xp, mod_p, lw, per_sequence=False)
        s_attn, sk, sv = _sample_attention(xs, mod_s, rope_s, cache_k[l], cache_v[l], lw)
        xs, sa, sb, sg = _sample_rest(xs, mod_s, s_attn, state_conv_a[l], state_conv_b[l], lw)
        xs = _mlp(xs, mod_s, lw, per_sequence=True)
        kv_shape_p = (bp, WINDOW, N_KV_HEADS, HEAD_DIM)
        kv_shape_s = (bs, WINDOW, N_KV_HEADS, HEAD_DIM)
        for lst, val in zip(outs, (pa, sa, pb, sb, pk.reshape(kv_shape_p), sk.reshape(kv_shape_s),
                                   pv.reshape(kv_shape_p), sv.reshape(kv_shape_s), sg)):
            lst.append(val)
    return (xp, xs) + tuple(jnp.stack(o) for o in outs)
```

```python
import functools

import jax
import jax.numpy as jnp
from jax import lax
from jax.experimental import pallas as pl
from jax.experimental.pallas import tpu as pltpu

F32 = jnp.float32
BF16 = jnp.bfloat16

D_MODEL = 1024
HEAD_DIM = 64
N_HEADS = 16
N_KV_HEADS = 4
Q_REP = 4
KV_DIM = N_KV_HEADS * HEAD_DIM
ROT_DIM = 16
ROPE_THETA = 500000.0
WINDOW = 128
ATTN_BLOCK = 128
KEY_SPAN = 2 * ATTN_BLOCK
CONV_W = 512
CONF_CONV_W = 31
SC_CONV_W = 3
GMLP_CHUNK = 128
GMLP_GROUPS = 8
GMLP_GROUP_DIM = 64
D_FF = 4 * D_MODEL
EPS = 1e-6
PAST_LEN = 16384
DEC_SEQ = 8
NEG = -1e30

Q0 = 0
K0 = Q0 + D_MODEL
V0 = K0 + KV_DIM
A0 = V0 + KV_DIM
B0 = A0 + 2 * CONV_W
C0 = B0 + 3 * CONV_W
G0 = C0 + 2 * CONV_W
QKV_COLS = A0

LANES = 128
SUBLANES = 8
VMEM_BYTES = 64 * 1024 * 1024

PROMPT_TILE = 256
MLP_TILE = 512
SAMPLE_SEQS = 32
SAMPLE_ATTN_SEQS = 16
CONV_ROWS = 32
A_PAD = 32
B_PAD = 8
MOD_ROWS = 136


def _dot(a, b):
    return jnp.dot(a, b, preferred_element_type=F32)


def _rms_mod(x, g, scale, shift):
    ms = jnp.mean(x * x, axis=-1, keepdims=True)
    y = x * lax.rsqrt(ms + EPS)
    return (y * g) * (1.0 + scale) + shift


def _layer_norm(x, g, b):
    mu = jnp.mean(x, axis=-1, keepdims=True)
    xc = x - mu
    var = jnp.mean(xc * xc, axis=-1, keepdims=True)
    return xc * lax.rsqrt(var + EPS) * g + b


def _head_norm_rope(xc, gain, seg_mean, cos_t, sin_a, sin_b):
    ms = _dot((xc * xc).astype(BF16), seg_mean)
    y = xc * lax.rsqrt(ms + EPS) * gain
    return (y * cos_t + pltpu.roll(y, LANES - ROT_DIM // 2, 1) * sin_a
            + pltpu.roll(y, ROT_DIM // 2, 1) * sin_b)


def _qkv_heads(qkv, qkg_ref, seg_mean, cos_t, sin_a, sin_b):
    def chunk(c, gain):
        return _head_norm_rope(qkv[:, c * LANES:(c + 1) * LANES], gain, seg_mean,
                               cos_t, sin_a, sin_b)
    q = jnp.concatenate([chunk(c, qkg_ref[0:1, :]) for c in range(D_MODEL // LANES)], axis=1)
    k = jnp.concatenate([chunk(K0 // LANES + c, qkg_ref[1:2, :])
                         for c in range(KV_DIM // LANES)], axis=1)
    return q, k, qkv[:, V0:V0 + KV_DIM]


def _kv_head_masks(scale):
    lane = lax.broadcasted_iota(jnp.int32, (1, KV_DIM), 1)
    return [jnp.where((lane >= g * HEAD_DIM) & (lane < (g + 1) * HEAD_DIM), scale, 0.0).astype(F32)
            for g in range(N_KV_HEADS)]


def _softmax_with_sink(s, sink):
    m = jnp.maximum(jnp.max(s, axis=-1, keepdims=True), sink)
    e = jnp.exp(s - m)
    den = jnp.sum(e, axis=-1, keepdims=True) + jnp.exp(sink - m)
    return e * (1.0 / den)


def _mod_kernel(c_ref, w_ref, b_ref, o_ref):
    c = c_ref[...]
    s = (c * jax.nn.sigmoid(c)).astype(BF16)
    o_ref[...] = _dot(s, w_ref[...].astype(BF16)) + b_ref[...]


def _modulation(c_all, w_ada, b_ada):
    depth = w_ada.shape[0]
    n_chunks = w_ada.shape[2] // D_MODEL
    return pl.pallas_call(
        _mod_kernel,
        out_shape=jax.ShapeDtypeStruct((depth, MOD_ROWS, n_chunks * D_MODEL), F32),
        grid=(depth, n_chunks),
        in_specs=[
            pl.BlockSpec((MOD_ROWS, D_MODEL), lambda l, j: (0, 0)),
            pl.BlockSpec((None, D_MODEL, D_MODEL), lambda l, j: (l, 0, j)),
            pl.BlockSpec((None, 1, D_MODEL), lambda l, j: (l, 0, j)),
        ],
        out_specs=pl.BlockSpec((None, MOD_ROWS, D_MODEL), lambda l, j: (l, 0, j)),
        compiler_params=pltpu.CompilerParams(
            dimension_semantics=("arbitrary", "arbitrary"),
            vmem_limit_bytes=32 * 1024 * 1024),
        name="adaln_modulation",
    )(c_all, w_ada, b_ada.reshape(depth, 1, -1))


def _prompt_attention_block(q_blk, k_win, v_win, bias, sink_ref, cm_scaled, cm_plain):
    tq = q_blk.shape[0]
    pieces = [q_blk[:, r * KV_DIM:(r + 1) * KV_DIM] * cm_scaled[g]
              for r in range(Q_REP) for g in range(N_KV_HEADS)]
    qbd = jnp.concatenate(pieces, axis=0).astype(BF16)
    s = lax.dot_general(qbd, k_win, (((1,), (1,)), ((), ())), preferred_element_type=F32)
    probs = []
    for r in range(Q_REP):
        for g in range(N_KV_HEADS):
            i = r * N_KV_HEADS + g
            sp = s[i * tq:(i + 1) * tq] + bias
            probs.append(_softmax_with_sink(sp, sink_ref[g * Q_REP + r]).astype(BF16))
    o = _dot(jnp.concatenate(probs, axis=0), v_win)
    slabs = []
    for r in range(Q_REP):
        acc = None
        for g in range(N_KV_HEADS):
            i = r * N_KV_HEADS + g
            part = o[i * tq:(i + 1) * tq] * cm_plain[g]
            acc = part if acc is None else acc + part
        slabs.append(acc)
    return jnp.concatenate(slabs, axis=1)


def _mixer_prompt_kernel(x_ref, mod_ref, rope_ref, sink_ref, n1g_ref, qkg_ref, caw_ref, v512_ref,
                         wpair_ref, bsp_ref, segm_ref,
                         w_in_ref, w_ao_ref, w_aout_ref, w_bout_ref, w_cout_ref, w_o_ref,
                         x1_ref, ca_ref, cb_ref, ko_ref, vo_ref,
                         kprev_s, vprev_s, aprev_s, bprev_s, acat_s, bcat_s, bias_s):
    tile = x_ref.shape[0]
    j = pl.program_id(1)
    last = j == pl.num_programs(1) - 1

    @pl.when(j == 0)
    def _():
        kprev_s[...] = jnp.zeros((ATTN_BLOCK, KV_DIM), BF16)
        vprev_s[...] = jnp.zeros((ATTN_BLOCK, KV_DIM), BF16)
        aprev_s[...] = jnp.zeros((A_PAD, CONV_W), F32)
        bprev_s[...] = jnp.zeros((B_PAD, CONV_W), F32)

    x = x_ref[...]
    shift1, scale1, gate1 = mod_ref[0:1, :], mod_ref[1:2, :], mod_ref[2:3, :]
    h = _rms_mod(x, n1g_ref[...], scale1, shift1).astype(BF16)

    qkv = _dot(h, w_in_ref[:, Q0:QKV_COLS])
    q, k, v = _qkv_heads(qkv, qkg_ref, segm_ref[...], rope_ref[0], rope_ref[1], rope_ref[2])
    kcat = jnp.concatenate([kprev_s[...], k.astype(BF16)], axis=0)
    vcat = jnp.concatenate([vprev_s[...], v.astype(BF16)], axis=0)

    @pl.when(last)
    def _():
        ko_ref[...] = k[tile - WINDOW:tile]
        vo_ref[...] = v[tile - WINDOW:tile]

    tt = lax.broadcasted_iota(jnp.int32, (ATTN_BLOCK, KEY_SPAN), 0)
    jj = lax.broadcasted_iota(jnp.int32, (ATTN_BLOCK, KEY_SPAN), 1)
    dist = jj - tt
    bias_rest = jnp.where((dist >= 0) & (dist <= WINDOW), 0.0, NEG).astype(F32)

    @pl.when(j == 0)
    def _():
        bias_s[...] = jnp.where(jj < ATTN_BLOCK, NEG, bias_rest)

    @pl.when(j == 1)
    def _():
        bias_s[...] = bias_rest

    bias_first = bias_s[...]
    cm_scaled = _kv_head_masks(HEAD_DIM ** -0.5)
    cm_plain = _kv_head_masks(1.0)
    blocks = []
    for i in range(tile // ATTN_BLOCK):
        lo = i * ATTN_BLOCK
        blocks.append(_prompt_attention_block(
            q[lo:lo + ATTN_BLOCK], kcat[lo:lo + KEY_SPAN], vcat[lo:lo + KEY_SPAN],
            bias_first if i == 0 else bias_rest, sink_ref, cm_scaled, cm_plain))
    attn = jnp.concatenate(blocks, axis=0)
    kprev_s[...] = kcat[tile:tile + ATTN_BLOCK]
    vprev_s[...] = vcat[tile:tile + ATTN_BLOCK]
    attn_out = _dot(attn.astype(BF16), w_ao_ref[...])
    merged = jax.nn.sigmoid(_dot(h, w_in_ref[:, G0:G0 + D_MODEL])) * attn_out

    a_in = _dot(h, w_in_ref[:, A0:A0 + 2 * CONV_W])
    glu = a_in[:, :CONV_W] * jax.nn.sigmoid(a_in[:, CONV_W:])
    acat_s[0:A_PAD, :] = aprev_s[...]
    acat_s[A_PAD:A_PAD + tile, :] = glu
    aprev_s[...] = glu[tile - A_PAD:tile]
    base = A_PAD - (CONF_CONV_W - 1)
    conv_rows = []
    for c in range(tile // CONV_ROWS):
        acc = jnp.broadcast_to(v512_ref[0:1, :], (CONV_ROWS, CONV_W))
        for t in range(CONF_CONV_W):
            lo = base + c * CONV_ROWS + t
            acc = acc + caw_ref[t:t + 1, :] * acat_s[lo:lo + CONV_ROWS, :]
        conv_rows.append(acc)
    a_conv = jnp.concatenate(conv_rows, axis=0)

    @pl.when(last)
    def _():
        ca_ref[...] = acat_s[base + tile:A_PAD + tile, :]

    a_act = jax.nn.silu(_layer_norm(a_conv, v512_ref[1:2, :], v512_ref[2:3, :]))
    a_out = _dot(a_act.astype(BF16), w_aout_ref[...])
    merged = merged + jax.nn.sigmoid(_dot(h, w_in_ref[:, G0 + D_MODEL:G0 + 2 * D_MODEL])) * a_out

    b_in = _dot(h, w_in_ref[:, B0:B0 + 3 * CONV_W])
    ch = b_in[:, CONV_W:2 * CONV_W] * b_in[:, 2 * CONV_W:]
    bcat_s[0:B_PAD, :] = bprev_s[...]
    bcat_s[B_PAD:B_PAD + tile, :] = ch
    bprev_s[...] = ch[tile - B_PAD:tile]
    base_b = B_PAD - (SC_CONV_W - 1)
    sc = None
    for t in range(SC_CONV_W):
        term = v512_ref[5 + t:6 + t, :] * bcat_s[base_b + t:base_b + t + tile, :]
        sc = term if sc is None else sc + term

    @pl.when(last)
    def _():
        cb_ref[...] = bcat_s[base_b + tile:B_PAD + tile, :]

    b_out = _dot((b_in[:, :CONV_W] * sc).astype(BF16), w_bout_ref[...])
    merged = merged + jax.nn.sigmoid(_dot(h, w_in_ref[:, G0 + 2 * D_MODEL:G0 + 3 * D_MODEL])) * b_out

    zc = jax.nn.gelu(_dot(h, w_in_ref[:, C0:C0 + 2 * CONV_W]))
    u = zc[:, :CONV_W]
    vv = _layer_norm(zc[:, CONV_W:], v512_ref[3:4, :], v512_ref[4:5, :])
    row = lax.broadcasted_iota(jnp.int32, (GMLP_CHUNK, 2 * GMLP_CHUNK), 0)
    col = lax.broadcasted_iota(jnp.int32, (GMLP_CHUNK, 2 * GMLP_CHUNK), 1)
    causal = (col & (GMLP_CHUNK - 1)) <= row
    n_pairs = GMLP_GROUPS // 2
    wpair = [jnp.where(causal, wpair_ref[p], 0.0).astype(BF16) for p in range(n_pairs)]
    low = lax.broadcasted_iota(jnp.int32, (1, LANES), 1) < GMLP_GROUP_DIM
    gated = []
    for c in range(tile // GMLP_CHUNK):
        vch = vv[c * GMLP_CHUNK:(c + 1) * GMLP_CHUNK]
        zs = []
        for p in range(n_pairs):
            blk = vch[:, p * LANES:(p + 1) * LANES]
            rhs = jnp.concatenate([jnp.where(low, blk, 0.0), jnp.where(low, 0.0, blk)], axis=0)
            zs.append(_dot(wpair[p], rhs.astype(BF16)))
        z = jnp.concatenate(zs, axis=1) + bsp_ref[...]
        gated.append(u[c * GMLP_CHUNK:(c + 1) * GMLP_CHUNK] * z)
    c_out = _dot(jnp.concatenate(gated, axis=0).astype(BF16), w_cout_ref[...])
    merged = merged + jax.nn.sigmoid(_dot(h, w_in_ref[:, G0 + 3 * D_MODEL:G0 + 4 * D_MODEL])) * c_out

    x1_ref[...] = x + gate1 * _dot(merged.astype(BF16), w_o_ref[...])


def _const_spec(shape):
    zeros = (0,) * len(shape)
    return pl.BlockSpec(shape, lambda *_: zeros, pipeline_mode=pl.Buffered(1))


def _mixer_prompt(x, mod, rope, lw):
    bsz, seq, _ = x.shape
    tile = PROMPT_TILE
    small = [lw["n1g"], lw["qkg"], lw["caw"], lw["v512"], lw["wpair"], lw["bsp"], lw["segm"]]
    weights = [lw["w_in"], lw["w_ao"], lw["w_aout"], lw["w_bout"], lw["w_cout"], lw["w_o"]]
    in_specs = (
        [pl.BlockSpec((None, tile, D_MODEL), lambda b, j: (b, j, 0)),
         pl.BlockSpec((None, 6, D_MODEL), lambda b, j: (b, 0, 0)),
         pl.BlockSpec((3, tile, LANES), lambda b, j: (0, j, 0)),
         pl.BlockSpec(memory_space=pltpu.SMEM)]
        + [_const_spec(a.shape) for a in small]
        + [_const_spec(a.shape) for a in weights])
    out_shape = (
        jax.ShapeDtypeStruct((bsz, seq, D_MODEL), F32),
        jax.ShapeDtypeStruct((bsz, CONF_CONV_W - 1, CONV_W), F32),
        jax.ShapeDtypeStruct((bsz, SC_CONV_W - 1, CONV_W), F32),
        jax.ShapeDtypeStruct((bsz, WINDOW, KV_DIM), F32),
        jax.ShapeDtypeStruct((bsz, WINDOW, KV_DIM), F32),
    )
    out_specs = (
        pl.BlockSpec((None, tile, D_MODEL), lambda b, j: (b, j, 0)),
        pl.BlockSpec((None, CONF_CONV_W - 1, CONV_W), lambda b, j: (b, 0, 0)),
        pl.BlockSpec((None, SC_CONV_W - 1, CONV_W), lambda b, j: (b, 0, 0)),
        pl.BlockSpec((None, WINDOW, KV_DIM), lambda b, j: (b, 0, 0)),
        pl.BlockSpec((None, WINDOW, KV_DIM), lambda b, j: (b, 0, 0)),
    )
    return pl.pallas_call(
        _mixer_prompt_kernel,
        out_shape=out_shape,
        grid=(bsz, seq // tile),
        in_specs=in_specs,
        out_specs=out_specs,
        scratch_shapes=[
            pltpu.VMEM((ATTN_BLOCK, KV_DIM), BF16),
            pltpu.VMEM((ATTN_BLOCK, KV_DIM), BF16),
            pltpu.VMEM((A_PAD, CONV_W), F32),
            pltpu.VMEM((B_PAD, CONV_W), F32),
            pltpu.VMEM((A_PAD + tile, CONV_W), F32),
            pltpu.VMEM((B_PAD + tile, CONV_W), F32),
            pltpu.VMEM((ATTN_BLOCK, KEY_SPAN), F32),
        ],
        compiler_params=pltpu.CompilerParams(
            dimension_semantics=("arbitrary", "arbitrary"),
            vmem_limit_bytes=VMEM_BYTES - 8 * 1024 * 1024),
        name="mixer_prompt",
    )(x, mod, rope, lw["sinks"], *small, *weights)


SAMPLE_NEW_ROWS = 16


def _sample_attn_kernel(x_ref, mod_ref, rope_ref, sink_ref, n1g_ref, qkg_ref, segm_ref,
                        ck_ref, cv_ref, w_qkv_ref, w_ao_ref,
                        ao_ref, nk_ref, nv_ref, kcat_s, vcat_s):
    nseq, tq, _ = x_ref.shape
    rows = nseq * tq
    cache_lo = SAMPLE_NEW_ROWS
    cache_hi = cache_lo + WINDOW

    x = x_ref[...]
    shift1, scale1 = mod_ref[:, 0:1, :], mod_ref[:, 1:2, :]
    h = _rms_mod(x, n1g_ref[...], scale1, shift1).reshape(rows, D_MODEL).astype(BF16)

    qkv = _dot(h, w_qkv_ref[...])
    q, k, v = _qkv_heads(qkv, qkg_ref, segm_ref[...], rope_ref[0], rope_ref[1], rope_ref[2])
    k3 = k.reshape(nseq, tq, KV_DIM)
    v3 = v.reshape(nseq, tq, KV_DIM)
    keep = WINDOW - tq
    nk_ref[:, 0:keep, :] = ck_ref[:, tq:WINDOW, :]
    nk_ref[:, keep:WINDOW, :] = k3
    nv_ref[:, 0:keep, :] = cv_ref[:, tq:WINDOW, :]
    nv_ref[:, keep:WINDOW, :] = v3
    pad = jnp.zeros((nseq, SAMPLE_NEW_ROWS - tq, KV_DIM), F32)
    kcat_s[:, 0:cache_lo, :] = jnp.concatenate([k3, pad], axis=1).astype(BF16)
    vcat_s[:, 0:cache_lo, :] = jnp.concatenate([v3, pad], axis=1).astype(BF16)
    kcat_s[:, cache_lo:cache_hi, :] = ck_ref[...].astype(BF16)
    vcat_s[:, cache_lo:cache_hi, :] = cv_ref[...].astype(BF16)
    tail = jnp.zeros((nseq, KEY_SPAN - cache_hi, KV_DIM), BF16)
    kcat_s[:, cache_hi:KEY_SPAN, :] = tail
    vcat_s[:, cache_hi:KEY_SPAN, :] = tail

    n_q = N_HEADS * tq
    tt = lax.broadcasted_iota(jnp.int32, (n_q, KEY_SPAN), 0) & (tq - 1)
    ii = lax.broadcasted_iota(jnp.int32, (n_q, KEY_SPAN), 1)
    visible = ((ii < tq) & (ii <= tt)) | ((ii >= cache_lo + tt) & (ii < cache_hi))
    bias = jnp.where(visible, 0.0, NEG).astype(F32)
    cm_scaled = _kv_head_masks(HEAD_DIM ** -0.5)
    cm_plain = _kv_head_masks(1.0)
    q3 = q.reshape(nseq, tq, D_MODEL)
    qbd = jnp.concatenate([q3[:, :, r * KV_DIM:(r + 1) * KV_DIM] * cm_scaled[g]
                           for r in range(Q_REP) for g in range(N_KV_HEADS)], axis=1).astype(BF16)
    s = jnp.einsum("bqc,bkc->bqk", qbd, kcat_s[...], preferred_element_type=F32) + bias
    sink = jnp.concatenate([jnp.full((tq, 1), sink_ref[g * Q_REP + r], F32)
                            for r in range(Q_REP) for g in range(N_KV_HEADS)], axis=0)
    p = _softmax_with_sink(s, sink).astype(BF16)
    o = jnp.einsum("bqk,bkc->bqc", p, vcat_s[...], preferred_element_type=F32)
    slabs = []
    for r in range(Q_REP):
        acc = None
        for g in range(N_KV_HEADS):
            i = r * N_KV_HEADS + g
            part = o[:, i * tq:(i + 1) * tq, :] * cm_plain[g]
            acc = part if acc is None else acc + part
        slabs.append(acc)
    attn = jnp.concatenate(slabs, axis=2).reshape(rows, D_MODEL)
    ao_ref[...] = _dot(attn.astype(BF16), w_ao_ref[...]).reshape(nseq, tq, D_MODEL)


def _sample_rest_kernel(x_ref, mod_ref, ao_ref, n1g_ref, caw_ref, v512_ref, wc_ref, bsp_ref,
                        sta_ref, stb_ref, w_in_ref, w_aout_ref, w_bout_ref, w_cout_ref, w_o_ref,
                        x1_ref, na_ref, nb_ref, vv_ref, acat_s, bcat_s):
    nseq, tq, _ = x_ref.shape
    rows = nseq * tq

    x = x_ref[...]
    shift1, scale1, gate1 = mod_ref[:, 0:1, :], mod_ref[:, 1:2, :], mod_ref[:, 2:3, :]
    h = _rms_mod(x, n1g_ref[...], scale1, shift1).reshape(rows, D_MODEL).astype(BF16)
    attn_out = ao_ref[...].reshape(rows, D_MODEL)
    merged = jax.nn.sigmoid(_dot(h, w_in_ref[:, G0:G0 + D_MODEL])) * attn_out

    a_in = _dot(h, w_in_ref[:, A0:A0 + 2 * CONV_W])
    glu = a_in[:, :CONV_W] * jax.nn.sigmoid(a_in[:, CONV_W:])
    hist_a = CONF_CONV_W - 1
    base = A_PAD - hist_a
    acat_s[:, base:A_PAD, :] = sta_ref[...]
    acat_s[:, A_PAD:A_PAD + tq, :] = glu.reshape(nseq, tq, CONV_W)
    a_conv = jnp.broadcast_to(v512_ref[0:1, :], (nseq, tq, CONV_W))
    for t in range(CONF_CONV_W):
        a_conv = a_conv + caw_ref[t:t + 1, :] * acat_s[:, base + t:base + t + tq, :]
    na_ref[...] = acat_s[:, base + tq:A_PAD + tq, :]
    a_act = jax.nn.silu(_layer_norm(a_conv, v512_ref[1:2, :], v512_ref[2:3, :]))
    a_out = _dot(a_act.reshape(rows, CONV_W).astype(BF16), w_aout_ref[...])
    merged = merged + jax.nn.sigmoid(_dot(h, w_in_ref[:, G0 + D_MODEL:G0 + 2 * D_MODEL])) * a_out

    b_in = _dot(h, w_in_ref[:, B0:B0 + 3 * CONV_W])
    hist_b = SC_CONV_W - 1
    base_b = B_PAD - hist_b
    bcat_s[:, base_b:B_PAD, :] = stb_ref[...]
    bcat_s[:, B_PAD:B_PAD + tq, :] = (b_in[:, CONV_W:2 * CONV_W]
                                      * b_in[:, 2 * CONV_W:]).reshape(nseq, tq, CONV_W)
    sc = None
    for t in range(SC_CONV_W):
        term = v512_ref[5 + t:6 + t, :] * bcat_s[:, base_b + t:base_b + t + tq, :]
        sc = term if sc is None else sc + term
    nb_ref[...] = bcat_s[:, base_b + tq:B_PAD + tq, :]
    b_out = _dot((b_in[:, :CONV_W] * sc.reshape(rows, CONV_W)).astype(BF16), w_bout_ref[...])
    merged = merged + jax.nn.sigmoid(_dot(h, w_in_ref[:, G0 + 2 * D_MODEL:G0 + 3 * D_MODEL])) * b_out

    zc = jax.nn.gelu(_dot(h, w_in_ref[:, C0:C0 + 2 * CONV_W]))
    u3 = zc[:, :CONV_W].reshape(nseq, tq, CONV_W)
    vv3 = _layer_norm(zc[:, CONV_W:], v512_ref[3:4, :], v512_ref[4:5, :]).reshape(nseq, tq, CONV_W)
    vv_ref[...] = vv3
    t_idx = lax.broadcasted_iota(jnp.int32, (tq, CONV_W), 0)
    z = jnp.broadcast_to(bsp_ref[...], (nseq, tq, CONV_W))
    for src in range(tq):
        coef = jnp.where(t_idx >= src, wc_ref[src], 0.0)
        z = z + coef * vv3[:, src:src + 1, :]
    c_out = _dot((u3 * z).reshape(rows, CONV_W).astype(BF16), w_cout_ref[...])
    merged = merged + jax.nn.sigmoid(_dot(h, w_in_ref[:, G0 + 3 * D_MODEL:G0 + 4 * D_MODEL])) * c_out

    y = _dot(merged.astype(BF16), w_o_ref[...]).reshape(nseq, tq, D_MODEL)
    x1_ref[...] = x + gate1 * y


def _seq_spec(nseq, r, c):
    return pl.BlockSpec((nseq, r, c), lambda i: (i, 0, 0))


def _sample_attention(x, mod, rope, cache_k, cache_v, lw):
    nbatch, tq, _ = x.shape
    nseq = SAMPLE_ATTN_SEQS
    small = [lw["n1g"], lw["qkg"], lw["segm"]]
    in_specs = (
        [_seq_spec(nseq, tq, D_MODEL), _seq_spec(nseq, 6, D_MODEL),
         _const_spec(rope.shape), pl.BlockSpec(memory_space=pltpu.SMEM)]
        + [_const_spec(a.shape) for a in small]
        + [_seq_spec(nseq, WINDOW, KV_DIM), _seq_spec(nseq, WINDOW, KV_DIM),
           _const_spec((D_MODEL, QKV_COLS)), _const_spec(lw["w_ao"].shape)])
    out_shape = (
        jax.ShapeDtypeStruct((nbatch, tq, D_MODEL), F32),
        jax.ShapeDtypeStruct((nbatch, WINDOW, KV_DIM), F32),
        jax.ShapeDtypeStruct((nbatch, WINDOW, KV_DIM), F32),
    )
    out_specs = (_seq_spec(nseq, tq, D_MODEL), _seq_spec(nseq, WINDOW, KV_DIM),
                 _seq_spec(nseq, WINDOW, KV_DIM))
    return pl.pallas_call(
        _sample_attn_kernel,
        out_shape=out_shape,
        grid=(nbatch // nseq,),
        in_specs=in_specs,
        out_specs=out_specs,
        scratch_shapes=[
            pltpu.VMEM((nseq, KEY_SPAN, KV_DIM), BF16),
            pltpu.VMEM((nseq, KEY_SPAN, KV_DIM), BF16),
        ],
        compiler_params=pltpu.CompilerParams(
            dimension_semantics=("arbitrary",),
            vmem_limit_bytes=VMEM_BYTES - 16 * 1024 * 1024),
        name="sample_attention",
    )(x, mod, rope, lw["sinks"], *small, cache_k, cache_v, lw["w_in"], lw["w_ao"])


def _sample_rest(x, mod, attn_out, state_a, state_b, lw):
    nbatch, tq, _ = x.shape
    nseq = SAMPLE_SEQS
    small = [lw["n1g"], lw["caw"], lw["v512"], lw["wc8"], lw["bsp8"]]
    weights = [lw["w_in"], lw["w_aout"], lw["w_bout"], lw["w_cout"], lw["w_o"]]
    in_specs = (
        [_seq_spec(nseq, tq, D_MODEL), _seq_spec(nseq, 6, D_MODEL), _seq_spec(nseq, tq, D_MODEL)]
        + [_const_spec(a.shape) for a in small]
        + [_seq_spec(nseq, CONF_CONV_W - 1, CONV_W), _seq_spec(nseq, SC_CONV_W - 1, CONV_W)]
        + [_const_spec(a.shape) for a in weights])
    out_shape = (
        jax.ShapeDtypeStruct((nbatch, tq, D_MODEL), F32),
        jax.ShapeDtypeStruct((nbatch, CONF_CONV_W - 1, CONV_W), F32),
        jax.ShapeDtypeStruct((nbatch, SC_CONV_W - 1, CONV_W), F32),
        jax.ShapeDtypeStruct((nbatch, tq, CONV_W), F32),
    )
    out_specs = (_seq_spec(nseq, tq, D_MODEL), _seq_spec(nseq, CONF_CONV_W - 1, CONV_W),
                 _seq_spec(nseq, SC_CONV_W - 1, CONV_W), _seq_spec(nseq, tq, CONV_W))
    return pl.pallas_call(
        _sample_rest_kernel,
        out_shape=out_shape,
        grid=(nbatch // nseq,),
        in_specs=in_specs,
        out_specs=out_specs,
        scratch_shapes=[
            pltpu.VMEM((nseq, A_PAD + tq, CONV_W), F32),
            pltpu.VMEM((nseq, B_PAD + tq, CONV_W), F32),
        ],
        compiler_params=pltpu.CompilerParams(
            dimension_semantics=("arbitrary",),
            vmem_limit_bytes=VMEM_BYTES - 8 * 1024 * 1024),
        name="sample_rest",
    )(x, mod, attn_out, *small, state_a, state_b, *weights)


def _mlp_kernel(x_ref, mod_ref, g_ref, w_up_ref, w_down_ref, o_ref):
    x = x_ref[...]
    if x.ndim == 3:
        shift2, scale2, gate2 = mod_ref[:, 3:4, :], mod_ref[:, 4:5, :], mod_ref[:, 5:6, :]
    else:
        shift2, scale2, gate2 = mod_ref[3:4, :], mod_ref[4:5, :], mod_ref[5:6, :]
    h2 = _rms_mod(x, g_ref[...], scale2, shift2)
    up = _dot(h2.reshape(-1, D_MODEL).astype(BF16), w_up_ref[...])
    act = jnp.square(jnp.maximum(up, 0.0)).astype(BF16)
    y = _dot(act, w_down_ref[...]).reshape(x.shape)
    o_ref[...] = x + gate2 * y


def _mlp(x, mod, lw, per_sequence):
    if per_sequence:
        nbatch, tq, _ = x.shape
        nseq = MLP_TILE // tq
        grid = (nbatch // nseq,)
        x_spec = pl.BlockSpec((nseq, tq, D_MODEL), lambda i: (i, 0, 0))
        mod_spec = pl.BlockSpec((nseq, 6, D_MODEL), lambda i: (i, 0, 0))
    else:
        bsz, seq, _ = x.shape
        grid = (bsz, seq // MLP_TILE)
        x_spec = pl.BlockSpec((None, MLP_TILE, D_MODEL), lambda b, j: (b, j, 0))
        mod_spec = pl.BlockSpec((None, 6, D_MODEL), lambda b, j: (b, 0, 0))
    return pl.pallas_call(
        _mlp_kernel,
        out_shape=jax.ShapeDtypeStruct(x.shape, F32),
        grid=grid,
        in_specs=[x_spec, mod_spec, _const_spec(lw["n2g"].shape),
                  _const_spec(lw["w_up"].shape), _const_spec(lw["w_down"].shape)],
        out_specs=x_spec,
        compiler_params=pltpu.CompilerParams(
            dimension_semantics=("arbitrary",) * len(grid),
            vmem_limit_bytes=VMEM_BYTES - 16 * 1024 * 1024),
        name="mlp_sample" if per_sequence else "mlp_prompt",
    )(x, mod, lw["n2g"], lw["w_up"], lw["w_down"])


def _rope_tables(pos):
    half = ROT_DIM // 2
    inv_freq = jnp.power(ROPE_THETA, -jnp.arange(half, dtype=F32) * 2.0 / ROT_DIM)
    ang = pos.astype(F32)[:, None] * inv_freq[None, :]
    cos, sin = jnp.cos(ang), jnp.sin(ang)
    lane = jnp.arange(LANES) % HEAD_DIM
    f = lane % half
    cos_t = jnp.where(lane < ROT_DIM, cos[:, f], 1.0)
    sin_a = jnp.where(lane < half, -sin[:, f], 0.0)
    sin_b = jnp.where((lane >= half) & (lane < ROT_DIM), sin[:, f], 0.0)
    return jnp.stack([cos_t, sin_a, sin_b]).astype(F32)


def _layer_weights(l, p):
    w_in = p["w_in"][l]
    wq = w_in[:, :D_MODEL].reshape(D_MODEL, N_KV_HEADS, Q_REP, HEAD_DIM)
    wq = wq.transpose(0, 2, 1, 3).reshape(D_MODEL, D_MODEL)
    w_in = jnp.concatenate([wq, w_in[:, D_MODEL:]], axis=1).astype(BF16)
    w_ao = p["w_attn_o"][l].reshape(N_KV_HEADS, Q_REP, HEAD_DIM, D_MODEL)
    w_ao = w_ao.transpose(1, 0, 2, 3).reshape(D_MODEL, D_MODEL).astype(BF16)
    ws = p["w_spatial"][l]
    wpair = ws.reshape(GMLP_GROUPS // 2, 2, GMLP_CHUNK, GMLP_CHUNK).transpose(0, 2, 1, 3)
    wpair = wpair.reshape(GMLP_GROUPS // 2, GMLP_CHUNK, 2 * GMLP_CHUNK)
    bsp = jnp.repeat(p["b_spatial"][l].T, GMLP_GROUP_DIM, axis=1)
    wc8 = jnp.repeat(ws[:, :DEC_SEQ, :DEC_SEQ].transpose(2, 1, 0), GMLP_GROUP_DIM, axis=2)
    lane_head = jnp.arange(LANES) // HEAD_DIM
    segm = jnp.where(lane_head[:, None] == lane_head[None, :], 1.0 / HEAD_DIM, 0.0).astype(BF16)
    caw = jnp.concatenate([p["conv_a_w"][l], jnp.zeros((1, CONV_W), F32)], axis=0)
    v512 = jnp.concatenate([p["conv_a_b"][l][None], p["ln_a_g"][l][None], p["ln_a_b"][l][None],
                            p["ln_c_g"][l][None], p["ln_c_b"][l][None], p["conv_b_w"][l]], axis=0)
    return {
        "w_in": w_in, "w_ao": w_ao,
        "w_aout": p["w_a_out"][l].astype(BF16), "w_bout": p["w_b_out"][l].astype(BF16),
        "w_cout": p["w_c_out"][l].astype(BF16), "w_o": p["w_o"][l].astype(BF16),
        "w_up": p["w_up"][l].astype(BF16), "w_down": p["w_down"][l].astype(BF16),
        "n1g": p["norm1_g"][l][None], "n2g": p["norm2_g"][l][None],
        "qkg": jnp.stack([jnp.tile(p["q_norm_g"][l], LANES // HEAD_DIM),
                          jnp.tile(p["k_norm_g"][l], LANES // HEAD_DIM)]),
        "caw": caw, "v512": v512, "wpair": wpair, "bsp": bsp, "wc8": wc8, "bsp8": bsp[:DEC_SEQ],
        "segm": segm, "sinks": p["attn_sinks"][l],
    }


def kernel(x_prompt, x_sample, c_prompt, c_sample, state_conv_a, state_conv_b, cache_k_win, cache_v_win, w_ada, b_ada, norm1_g, norm2_g, w_in, q_norm_g, k_norm_g, attn_sinks, w_attn_o, conv_a_w, conv_a_b, ln_a_g, ln_a_b, w_a_out, conv_b_w, w_b_out, ln_c_g, ln_c_b, w_spatial, b_spatial, w_c_out, w_o, w_up, w_down):
    params = dict(w_in=w_in, w_attn_o=w_attn_o, w_spatial=w_spatial, b_spatial=b_spatial,
                  conv_a_w=conv_a_w, conv_a_b=conv_a_b, ln_a_g=ln_a_g, ln_a_b=ln_a_b,
                  ln_c_g=ln_c_g, ln_c_b=ln_c_b, conv_b_w=conv_b_w, w_a_out=w_a_out,
                  w_b_out=w_b_out, w_c_out=w_c_out, w_o=w_o, w_up=w_up, w_down=w_down,
                  norm1_g=norm1_g, norm2_g=norm2_g, q_norm_g=q_norm_g, k_norm_g=k_norm_g,
                  attn_sinks=attn_sinks)
    depth = w_in.shape[0]
    bp, seq, _ = x_prompt.shape
    bs, tq, _ = x_sample.shape

    c_all = jnp.concatenate([c_prompt, c_sample,
                             jnp.zeros((MOD_ROWS - bp - bs, D_MODEL), F32)], axis=0)
    mod_all = _modulation(c_all, w_ada, b_ada)
    rope_p = _rope_tables(jnp.arange(seq))
    rope_s = jnp.tile(_rope_tables(PAST_LEN + jnp.arange(tq)), (1, SAMPLE_ATTN_SEQS, 1))
    cache_k = cache_k_win.reshape(depth, bs, WINDOW, KV_DIM)
    cache_v = cache_v_win.reshape(depth, bs, WINDOW, KV_DIM)

    xp, xs = x_prompt, x_sample
    outs = [[] for _ in range(9)]
    for l in range(depth):
        lw = _layer_weights(l, params)
        mod_p = mod_all[l, :bp].reshape(bp, 6, D_MODEL)
        mod_s = mod_all[l, bp:bp + bs].reshape(bs, 6, D_MODEL)
        xp, pa, pb, pk, pv = _mixer_prompt(xp, mod_p, rope_p, lw)
        xp = _mlp(xp, mod_p, lw, per_sequence=False)
        s_attn, sk, sv = _sample_attention(xs, mod_s, rope_s, cache_k[l], cache_v[l], lw)
        xs, sa, sb, sg = _sample_rest(xs, mod_s, s_attn, state_conv_a[l], state_conv_b[l], lw)
        xs = _mlp(xs, mod_s, lw, per_sequence=True)
        kv_shape_p = (bp, WINDOW, N_KV_HEADS, HEAD_DIM)
        kv_shape_s = (bs, WINDOW, N_KV_HEADS, HEAD_DIM)
        for lst, val in zip(outs, (pa, sa, pb, sb, pk.reshape(kv_shape_p), sk.reshape(kv_shape_s),
                                   pv.reshape(kv_shape_p), sv.reshape(kv_shape_s), sg)):
            lst.append(val)
    return (xp, xs) + tuple(jnp.stack(o) for o in outs)
```

```python
import functools

import jax
import jax.numpy as jnp
from jax import lax
from jax.experimental import pallas as pl
from jax.experimental.pallas import tpu as pltpu

F32 = jnp.float32
BF16 = jnp.bfloat16

D_MODEL = 1024
HEAD_DIM = 64
N_HEADS = 16
N_KV_HEADS = 4
Q_REP = 4
KV_DIM = N_KV_HEADS * HEAD_DIM
ROT_DIM = 16
ROPE_THETA = 500000.0
WINDOW = 128
ATTN_BLOCK = 128
KEY_SPAN = 2 * ATTN_BLOCK
CONV_W = 512
CONF_CONV_W = 31
SC_CONV_W = 3
GMLP_CHUNK = 128
GMLP_GROUPS = 8
GMLP_GROUP_DIM = 64
D_FF = 4 * D_MODEL
EPS = 1e-6
PAST_LEN = 16384
DEC_SEQ = 8
NEG = -1e30

K0 = 0
V0 = K0 + KV_DIM
A0 = V0 + KV_DIM
B0 = A0 + 2 * CONV_W
C0 = B0 + 3 * CONV_W
G0 = C0 + 2 * CONV_W
KV_COLS = A0

LANES = 128
SUBLANES = 8
VMEM_BYTES = 64 * 1024 * 1024

PROMPT_TILE = 256
MLP_TILE = 512
SAMPLE_SEQS = 32
SAMPLE_ATTN_SEQS = 16
CONV_ROWS = 64
A_PAD = 32
B_PAD = 8
MOD_ROWS = 136


def _dot(a, b):
    return jnp.dot(a, b, preferred_element_type=F32)


def _rms_mod(x, g, scale, shift):
    ms = jnp.mean(x * x, axis=-1, keepdims=True)
    y = x * lax.rsqrt(ms + EPS)
    return (y * g) * (1.0 + scale) + shift


def _layer_norm(x, g, b):
    mu = jnp.mean(x, axis=-1, keepdims=True)
    xc = x - mu
    var = jnp.mean(xc * xc, axis=-1, keepdims=True)
    return xc * lax.rsqrt(var + EPS) * g + b


def _head_norm_rope(xc, gain, seg_mean, cos_t, sin_a, sin_b):
    ms = _dot((xc * xc).astype(BF16), seg_mean)
    y = xc * lax.rsqrt(ms + EPS) * gain
    return (y * cos_t + pltpu.roll(y, LANES - ROT_DIM // 2, 1) * sin_a
            + pltpu.roll(y, ROT_DIM // 2, 1) * sin_b)


def _qkv_heads(q_raw, kv_raw, qkg_ref, seg_mean, cos_t, sin_a, sin_b):
    def chunk(src, c, gain):
        return _head_norm_rope(src[:, c * LANES:(c + 1) * LANES], gain, seg_mean,
                               cos_t, sin_a, sin_b)
    q = jnp.concatenate([chunk(q_raw, c, qkg_ref[0:1, :]) for c in range(D_MODEL // LANES)],
                        axis=1)
    k = jnp.concatenate([chunk(kv_raw, K0 // LANES + c, qkg_ref[1:2, :])
                         for c in range(KV_DIM // LANES)], axis=1)
    return q, k, kv_raw[:, V0:V0 + KV_DIM]


def _kv_head_masks(scale):
    lane = lax.broadcasted_iota(jnp.int32, (1, KV_DIM), 1)
    return [jnp.where((lane >= g * HEAD_DIM) & (lane < (g + 1) * HEAD_DIM), scale, 0.0).astype(F32)
            for g in range(N_KV_HEADS)]


def _softmax_with_sink(s, sink):
    m = jnp.maximum(jnp.max(s, axis=-1, keepdims=True), sink)
    e = jnp.exp(s - m)
    den = jnp.sum(e, axis=-1, keepdims=True) + jnp.exp(sink - m)
    return e * (1.0 / den)


def _causal_conv_rows(hist_ref, row0, rows, w_ref, n_taps, base, bias):
    out = jnp.broadcast_to(bias, (rows, hist_ref.shape[1]))
    for phase in range(SUBLANES):
        span = rows if phase == 0 else rows + SUBLANES
        part = None
        for tap in range(n_taps):
            offset = base + tap
            if offset % SUBLANES != phase:
                continue
            lo = row0 + offset - phase
            term = w_ref[tap:tap + 1, :] * hist_ref[lo:lo + span, :]
            part = term if part is None else part + term
        if part is not None:
            out = out + part[phase:phase + rows]
    return out


def _mod_kernel(c_ref, w_ref, b_ref, o_ref):
    c = c_ref[...]
    s = (c * jax.nn.sigmoid(c)).astype(BF16)
    o_ref[...] = _dot(s, w_ref[...].astype(BF16)) + b_ref[...]


def _modulation(c_all, w_ada, b_ada):
    depth = w_ada.shape[0]
    n_chunks = w_ada.shape[2] // D_MODEL
    return pl.pallas_call(
        _mod_kernel,
        out_shape=jax.ShapeDtypeStruct((depth, MOD_ROWS, n_chunks * D_MODEL), F32),
        grid=(depth, n_chunks),
        in_specs=[
            pl.BlockSpec((MOD_ROWS, D_MODEL), lambda l, j: (0, 0)),
            pl.BlockSpec((None, D_MODEL, D_MODEL), lambda l, j: (l, 0, j)),
            pl.BlockSpec((None, 1, D_MODEL), lambda l, j: (l, 0, j)),
        ],
        out_specs=pl.BlockSpec((None, MOD_ROWS, D_MODEL), lambda l, j: (l, 0, j)),
        compiler_params=pltpu.CompilerParams(
            dimension_semantics=("arbitrary", "arbitrary"),
            vmem_limit_bytes=32 * 1024 * 1024),
        name="adaln_modulation",
    )(c_all, w_ada, b_ada.reshape(depth, 1, -1))


def _block_scores(q_blk, k_win, cm_scaled):
    pieces = [q_blk[:, r * KV_DIM:(r + 1) * KV_DIM] * cm_scaled[g]
              for r in range(Q_REP) for g in range(N_KV_HEADS)]
    qbd = jnp.concatenate(pieces, axis=0).astype(BF16)
    return lax.dot_general(qbd, k_win, (((1,), (1,)), ((), ())), preferred_element_type=F32)


def _block_probs(s, bias, sink_ref):
    tq = bias.shape[0]
    probs = []
    for r in range(Q_REP):
        for g in range(N_KV_HEADS):
            i = r * N_KV_HEADS + g
            sp = s[i * tq:(i + 1) * tq] + bias
            probs.append(_softmax_with_sink(sp, sink_ref[g * Q_REP + r]).astype(BF16))
    return jnp.concatenate(probs, axis=0)


def _block_output(p, v_win, cm_plain):
    tq = p.shape[0] // N_HEADS
    o = _dot(p, v_win)
    slabs = []
    for r in range(Q_REP):
        acc = None
        for g in range(N_KV_HEADS):
            i = r * N_KV_HEADS + g
            part = o[i * tq:(i + 1) * tq] * cm_plain[g]
            acc = part if acc is None else acc + part
        slabs.append(acc)
    return jnp.concatenate(slabs, axis=1)


def _mixer_prompt_kernel(x_ref, *refs):
    n_shared = 16
    mod_ref, shared, per_seq = refs[0], refs[1:1 + n_shared], refs[1 + n_shared:]
    waiting = [_mixer_prompt_sequence(x_ref.at[b], mod_ref.at[b], *shared,
                                      *[r.at[b] for r in per_seq])
               for b in range(x_ref.shape[0])]
    running = []
    while waiting or running:
        if waiting:
            running.append(waiting.pop(0))
        for gen in list(running):
            if next(gen, "done") == "done":
                running.remove(gen)


def _mixer_prompt_sequence(x_ref, mod_ref, rope_ref, sink_ref, n1g_ref, qkg_ref, caw_ref, v512_ref,
                           wpair_ref, bsp_ref, segm_ref,
                           wq_ref, w_in_ref, w_ao_ref, w_aout_ref, w_bout_ref, w_cout_ref, w_o_ref,
                           x1_ref, ca_ref, cb_ref, ko_ref, vo_ref,
                           kprev_s, vprev_s, aprev_s, bprev_s, acat_s, bcat_s, bias_s):
    tile = x_ref.shape[0]
    j = pl.program_id(0)
    last = j == pl.num_programs(0) - 1
    rd = j & 1
    wr = 1 - rd

    @pl.when(j == 0)
    def _():
        kprev_s[0] = jnp.zeros((ATTN_BLOCK, KV_DIM), BF16)
        vprev_s[0] = jnp.zeros((ATTN_BLOCK, KV_DIM), BF16)
        aprev_s[0] = jnp.zeros((A_PAD, CONV_W), F32)
        bprev_s[0] = jnp.zeros((B_PAD, CONV_W), F32)

    x = x_ref[...]
    shift1, scale1, gate1 = mod_ref[0:1, :], mod_ref[1:2, :], mod_ref[2:3, :]
    h = _rms_mod(x, n1g_ref[...], scale1, shift1).astype(BF16)

    def gate_logits(branch):
        lo = G0 + branch * D_MODEL
        return _dot(h, w_in_ref[:, lo:lo + D_MODEL])

    yield
    q_raw = _dot(h, wq_ref[...])
    kv_raw = _dot(h, w_in_ref[:, K0:KV_COLS])
    a_in = _dot(h, w_in_ref[:, A0:A0 + 2 * CONV_W])
    yield
    q, k, v = _qkv_heads(q_raw, kv_raw, qkg_ref, segm_ref[...],
                         rope_ref[0], rope_ref[1], rope_ref[2])
    kcat = jnp.concatenate([kprev_s[rd], k.astype(BF16)], axis=0)
    vcat = jnp.concatenate([vprev_s[rd], v.astype(BF16)], axis=0)

    @pl.when(last)
    def _():
        ko_ref[...] = k[tile - WINDOW:tile]
        vo_ref[...] = v[tile - WINDOW:tile]

    tt = lax.broadcasted_iota(jnp.int32, (ATTN_BLOCK, KEY_SPAN), 0)
    jj = lax.broadcasted_iota(jnp.int32, (ATTN_BLOCK, KEY_SPAN), 1)
    dist = jj - tt
    bias_rest = jnp.where((dist >= 0) & (dist <= WINDOW), 0.0, NEG).astype(F32)

    @pl.when(j == 0)
    def _():
        bias_s[...] = jnp.where(jj < ATTN_BLOCK, NEG, bias_rest)

    @pl.when(j == 1)
    def _():
        bias_s[...] = bias_rest

    bias_first = bias_s[...]
    cm_scaled = _kv_head_masks(HEAD_DIM ** -0.5)
    cm_plain = _kv_head_masks(1.0)
    kprev_s[wr] = kcat[tile:tile + ATTN_BLOCK]
    vprev_s[wr] = vcat[tile:tile + ATTN_BLOCK]
    n_blocks = tile // ATTN_BLOCK

    def scores(i):
        lo = i * ATTN_BLOCK
        return _block_scores(q[lo:lo + ATTN_BLOCK], kcat[lo:lo + KEY_SPAN], cm_scaled)

    glu = a_in[:, :CONV_W] * jax.nn.sigmoid(a_in[:, CONV_W:])
    acat_s[0:A_PAD, :] = aprev_s[rd]
    acat_s[A_PAD:A_PAD + tile, :] = glu
    aprev_s[wr] = glu[tile - A_PAD:tile]
    base = A_PAD - (CONF_CONV_W - 1)

    @pl.when(last)
    def _():
        ca_ref[...] = acat_s[base + tile:A_PAD + tile, :]

    n_chunks = tile // CONV_ROWS
    conv_rows = []

    def conv_chunks(lo, hi):
        for c in range(lo, hi):
            conv_rows.append(_causal_conv_rows(acat_s, c * CONV_ROWS, CONV_ROWS, caw_ref,
                                               CONF_CONV_W, base, v512_ref[0:1, :]))

    fillers = [lambda: _dot(h, w_in_ref[:, B0:B0 + 3 * CONV_W]),
               lambda: _dot(h, w_in_ref[:, C0:C0 + 2 * CONV_W])]
    filled = []
    blocks = []
    p = None
    yield
    for i in range(n_blocks + 1):
        if i > 0:
            lo = (i - 1) * ATTN_BLOCK
            blocks.append(_block_output(p, vcat[lo:lo + KEY_SPAN], cm_plain))
        if i < n_blocks:
            s_blk = scores(i)
        if i < len(fillers):
            filled.append(fillers[i]())
        if i == n_blocks:
            for f in fillers[len(filled):]:
                filled.append(f())
            g0 = gate_logits(0)
            g1 = gate_logits(1)
        yield
        if i < n_blocks:
            p = _block_probs(s_blk, bias_first if i == 0 else bias_rest, sink_ref)
            conv_chunks(i, i + 1)
            yield
    b_in, c_in = filled
    conv_chunks(n_blocks, n_chunks)
    attn = jnp.concatenate(blocks, axis=0).astype(BF16)
    a_conv = jnp.concatenate(conv_rows, axis=0)
    a_act = jax.nn.silu(_layer_norm(a_conv, v512_ref[1:2, :], v512_ref[2:3, :])).astype(BF16)
    yield
    attn_out = _dot(attn, w_ao_ref[...])
    a_out = _dot(a_act, w_aout_ref[...])
    g2 = gate_logits(2)
    yield

    ch = b_in[:, CONV_W:2 * CONV_W] * b_in[:, 2 * CONV_W:]
    bcat_s[0:B_PAD, :] = bprev_s[rd]
    bcat_s[B_PAD:B_PAD + tile, :] = ch
    bprev_s[wr] = ch[tile - B_PAD:tile]
    base_b = B_PAD - (SC_CONV_W - 1)
    sc = None
    for t in range(SC_CONV_W):
        term = v512_ref[5 + t:6 + t, :] * bcat_s[base_b + t:base_b + t + tile, :]
        sc = term if sc is None else sc + term

    @pl.when(last)
    def _():
        cb_ref[...] = bcat_s[base_b + tile:B_PAD + tile, :]

    merged = jax.nn.sigmoid(g0) * attn_out
    merged = merged + jax.nn.sigmoid(g1) * a_out

    zc = jax.nn.gelu(c_in)
    u = zc[:, :CONV_W]
    vv = _layer_norm(zc[:, CONV_W:], v512_ref[3:4, :], v512_ref[4:5, :])
    row = lax.broadcasted_iota(jnp.int32, (GMLP_CHUNK, 2 * GMLP_CHUNK), 0)
    col = lax.broadcasted_iota(jnp.int32, (GMLP_CHUNK, 2 * GMLP_CHUNK), 1)
    causal = (col & (GMLP_CHUNK - 1)) <= row
    n_pairs = GMLP_GROUPS // 2
    wpair = [jnp.where(causal, wpair_ref[p], 0.0).astype(BF16) for p in range(n_pairs)]
    low = lax.broadcasted_iota(jnp.int32, (1, LANES), 1) < GMLP_GROUP_DIM
    rhs_blocks = []
    for c in range(tile // GMLP_CHUNK):
        vch = vv[c * GMLP_CHUNK:(c + 1) * GMLP_CHUNK]
        for pair in range(n_pairs):
            blk = vch[:, pair * LANES:(pair + 1) * LANES]
            rhs = jnp.concatenate([jnp.where(low, blk, 0.0), jnp.where(low, 0.0, blk)], axis=0)
            rhs_blocks.append(rhs.astype(BF16))
    b_act = (b_in[:, :CONV_W] * sc).astype(BF16)
    yield
    b_out = _dot(b_act, w_bout_ref[...])
    zs = [_dot(wpair[i % n_pairs], rhs) for i, rhs in enumerate(rhs_blocks)]
    g3 = gate_logits(3)
    yield
    gated = []
    for c in range(tile // GMLP_CHUNK):
        z = jnp.concatenate(zs[c * n_pairs:(c + 1) * n_pairs], axis=1) + bsp_ref[...]
        gated.append(u[c * GMLP_CHUNK:(c + 1) * GMLP_CHUNK] * z)
    gated = jnp.concatenate(gated, axis=0).astype(BF16)
    merged = merged + jax.nn.sigmoid(g2) * b_out
    yield
    c_out = _dot(gated, w_cout_ref[...])
    yield
    merged = (merged + jax.nn.sigmoid(g3) * c_out).astype(BF16)
    yield
    x1_ref[...] = x + gate1 * _dot(merged, w_o_ref[...])


def _const_spec(shape):
    zeros = (0,) * len(shape)
    return pl.BlockSpec(shape, lambda *_: zeros, pipeline_mode=pl.Buffered(1))


def _layer_spec(stacked, layer, cols=None):
    block = stacked.shape[1:] if cols is None else stacked.shape[1:-1] + (cols,)
    index = (layer,) + (0,) * len(block)
    return pl.BlockSpec((None,) + block, lambda *_: index, pipeline_mode=pl.Buffered(1))


def _mixer_prompt(x, mod, rope, lw, layer):
    bsz, seq, _ = x.shape
    tile = PROMPT_TILE
    small = [lw["n1g"], lw["qkg"], lw["caw"], lw["v512"], lw["wpair"], lw["bsp"], lw["segm"]]
    weights = [lw["wq"], lw["w_in"], lw["w_ao"], lw["w_aout"], lw["w_bout"], lw["w_cout"],
               lw["w_o"]]
    def whole(r, c):
        return pl.BlockSpec((bsz, r, c), lambda j: (0, 0, 0))

    in_specs = (
        [pl.BlockSpec((bsz, tile, D_MODEL), lambda j: (0, j, 0)),
         whole(6, D_MODEL),
         pl.BlockSpec((3, tile, LANES), lambda j: (0, j, 0)),
         pl.BlockSpec(memory_space=pltpu.SMEM)]
        + [_layer_spec(a, layer) for a in small]
        + [_layer_spec(a, layer) for a in weights])
    out_shape = (
        jax.ShapeDtypeStruct((bsz, seq, D_MODEL), F32),
        jax.ShapeDtypeStruct((bsz, CONF_CONV_W - 1, CONV_W), F32),
        jax.ShapeDtypeStruct((bsz, SC_CONV_W - 1, CONV_W), F32),
        jax.ShapeDtypeStruct((bsz, WINDOW, KV_DIM), F32),
        jax.ShapeDtypeStruct((bsz, WINDOW, KV_DIM), F32),
    )
    out_specs = (
        pl.BlockSpec((bsz, tile, D_MODEL), lambda j: (0, j, 0)),
        whole(CONF_CONV_W - 1, CONV_W), whole(SC_CONV_W - 1, CONV_W),
        whole(WINDOW, KV_DIM), whole(WINDOW, KV_DIM),
    )
    return pl.pallas_call(
        _mixer_prompt_kernel,
        out_shape=out_shape,
        grid=(seq // tile,),
        in_specs=in_specs,
        out_specs=out_specs,
        scratch_shapes=[
            pltpu.VMEM((bsz, 2, ATTN_BLOCK, KV_DIM), BF16),
            pltpu.VMEM((bsz, 2, ATTN_BLOCK, KV_DIM), BF16),
            pltpu.VMEM((bsz, 2, A_PAD, CONV_W), F32),
            pltpu.VMEM((bsz, 2, B_PAD, CONV_W), F32),
            pltpu.VMEM((bsz, A_PAD + tile, CONV_W), F32),
            pltpu.VMEM((bsz, B_PAD + tile, CONV_W), F32),
            pltpu.VMEM((bsz, ATTN_BLOCK, KEY_SPAN), F32),
        ],
        compiler_params=pltpu.CompilerParams(
            dimension_semantics=("arbitrary",),
            vmem_limit_bytes=VMEM_BYTES - 8 * 1024 * 1024),
        name="mixer_prompt",
    )(x, mod, rope, lw["sinks"][layer], *small, *weights)


SAMPLE_NEW_ROWS = 16


def _sample_attn_kernel(x_ref, mod_ref, rope_ref, sink_ref, n1g_ref, qkg_ref, segm_ref,
                        ck_ref, cv_ref, wq_ref, wkv_ref, w_ao_ref,
                        ao_ref, nk_ref, nv_ref, kcat_s, vcat_s):
    nseq, tq, _ = x_ref.shape
    rows = nseq * tq
    cache_lo = SAMPLE_NEW_ROWS
    cache_hi = cache_lo + WINDOW

    x = x_ref[...]
    shift1, scale1 = mod_ref[:, 0:1, :], mod_ref[:, 1:2, :]
    h = _rms_mod(x, n1g_ref[...], scale1, shift1).reshape(rows, D_MODEL).astype(BF16)

    q, k, v = _qkv_heads(_dot(h, wq_ref[...]), _dot(h, wkv_ref[...]), qkg_ref, segm_ref[...],
                         rope_ref[0], rope_ref[1], rope_ref[2])
    k3 = k.reshape(nseq, tq, KV_DIM)
    v3 = v.reshape(nseq, tq, KV_DIM)
    keep = WINDOW - tq
    nk_ref[:, 0:keep, :] = ck_ref[:, tq:WINDOW, :]
    nk_ref[:, keep:WINDOW, :] = k3
    nv_ref[:, 0:keep, :] = cv_ref[:, tq:WINDOW, :]
    nv_ref[:, keep:WINDOW, :] = v3
    pad = jnp.zeros((nseq, SAMPLE_NEW_ROWS - tq, KV_DIM), F32)
    kcat_s[:, 0:cache_lo, :] = jnp.concatenate([k3, pad], axis=1).astype(BF16)
    vcat_s[:, 0:cache_lo, :] = jnp.concatenate([v3, pad], axis=1).astype(BF16)
    kcat_s[:, cache_lo:cache_hi, :] = ck_ref[...].astype(BF16)
    vcat_s[:, cache_lo:cache_hi, :] = cv_ref[...].astype(BF16)
    tail = jnp.zeros((nseq, KEY_SPAN - cache_hi, KV_DIM), BF16)
    kcat_s[:, cache_hi:KEY_SPAN, :] = tail
    vcat_s[:, cache_hi:KEY_SPAN, :] = tail

    n_q = N_HEADS * tq
    tt = lax.broadcasted_iota(jnp.int32, (n_q, KEY_SPAN), 0) & (tq - 1)
    ii = lax.broadcasted_iota(jnp.int32, (n_q, KEY_SPAN), 1)
    visible = ((ii < tq) & (ii <= tt)) | ((ii >= cache_lo + tt) & (ii < cache_hi))
    bias = jnp.where(visible, 0.0, NEG).astype(F32)
    cm_scaled = _kv_head_masks(HEAD_DIM ** -0.5)
    cm_plain = _kv_head_masks(1.0)
    q3 = q.reshape(nseq, tq, D_MODEL)
    qbd = jnp.concatenate([q3[:, :, r * KV_DIM:(r + 1) * KV_DIM] * cm_scaled[g]
                           for r in range(Q_REP) for g in range(N_KV_HEADS)], axis=1).astype(BF16)
    s = jnp.einsum("bqc,bkc->bqk", qbd, kcat_s[...], preferred_element_type=F32) + bias
    sink = jnp.concatenate([jnp.full((tq, 1), sink_ref[g * Q_REP + r], F32)
                            for r in range(Q_REP) for g in range(N_KV_HEADS)], axis=0)
    p = _softmax_with_sink(s, sink).astype(BF16)
    o = jnp.einsum("bqk,bkc->bqc", p, vcat_s[...], preferred_element_type=F32)
    slabs = []
    for r in range(Q_REP):
        acc = None
        for g in range(N_KV_HEADS):
            i = r * N_KV_HEADS + g
            part = o[:, i * tq:(i + 1) * tq, :] * cm_plain[g]
            acc = part if acc is None else acc + part
        slabs.append(acc)
    attn = jnp.concatenate(slabs, axis=2).reshape(rows, D_MODEL)
    ao_ref[...] = _dot(attn.astype(BF16), w_ao_ref[...]).reshape(nseq, tq, D_MODEL)


def _sample_rest_kernel(x_ref, mod_ref, ao_ref, n1g_ref, caw_ref, v512_ref, wc_ref, bsp_ref,
                        sta_ref, stb_ref, w_in_ref, w_aout_ref, w_bout_ref, w_cout_ref, w_o_ref,
                        x1_ref, na_ref, nb_ref, vv_ref, acat_s, bcat_s):
    nseq, tq, _ = x_ref.shape
    rows = nseq * tq

    x = x_ref[...]
    shift1, scale1, gate1 = mod_ref[:, 0:1, :], mod_ref[:, 1:2, :], mod_ref[:, 2:3, :]
    h = _rms_mod(x, n1g_ref[...], scale1, shift1).reshape(rows, D_MODEL).astype(BF16)
    attn_out = ao_ref[...].reshape(rows, D_MODEL)
    merged = jax.nn.sigmoid(_dot(h, w_in_ref[:, G0:G0 + D_MODEL])) * attn_out

    a_in = _dot(h, w_in_ref[:, A0:A0 + 2 * CONV_W])
    glu = a_in[:, :CONV_W] * jax.nn.sigmoid(a_in[:, CONV_W:])
    hist_a = CONF_CONV_W - 1
    base = A_PAD - hist_a
    acat_s[:, base:A_PAD, :] = sta_ref[...]
    acat_s[:, A_PAD:A_PAD + tq, :] = glu.reshape(nseq, tq, CONV_W)
    a_conv = jnp.broadcast_to(v512_ref[0:1, :], (nseq, tq, CONV_W))
    for t in range(CONF_CONV_W):
        a_conv = a_conv + caw_ref[t:t + 1, :] * acat_s[:, base + t:base + t + tq, :]
    na_ref[...] = acat_s[:, base + tq:A_PAD + tq, :]
    a_act = jax.nn.silu(_layer_norm(a_conv, v512_ref[1:2, :], v512_ref[2:3, :]))
    a_out = _dot(a_act.reshape(rows, CONV_W).astype(BF16), w_aout_ref[...])
    merged = merged + jax.nn.sigmoid(_dot(h, w_in_ref[:, G0 + D_MODEL:G0 + 2 * D_MODEL])) * a_out

    b_in = _dot(h, w_in_ref[:, B0:B0 + 3 * CONV_W])
    hist_b = SC_CONV_W - 1
    base_b = B_PAD - hist_b
    bcat_s[:, base_b:B_PAD, :] = stb_ref[...]
    bcat_s[:, B_PAD:B_PAD + tq, :] = (b_in[:, CONV_W:2 * CONV_W]
                                      * b_in[:, 2 * CONV_W:]).reshape(nseq, tq, CONV_W)
    sc = None
    for t in range(SC_CONV_W):
        term = v512_ref[5 + t:6 + t, :] * bcat_s[:, base_b + t:base_b + t + tq, :]
        sc = term if sc is None else sc + term
    nb_ref[...] = bcat_s[:, base_b + tq:B_PAD + tq, :]
    b_out = _dot((b_in[:, :CONV_W] * sc.reshape(rows, CONV_W)).astype(BF16), w_bout_ref[...])
    merged = merged + jax.nn.sigmoid(_dot(h, w_in_ref[:, G0 + 2 * D_MODEL:G0 + 3 * D_MODEL])) * b_out

    zc = jax.nn.gelu(_dot(h, w_in_ref[:, C0:C0 + 2 * CONV_W]))
    u3 = zc[:, :CONV_W].reshape(nseq, tq, CONV_W)
    vv3 = _layer_norm(zc[:, CONV_W:], v512_ref[3:4, :], v512_ref[4:5, :]).reshape(nseq, tq, CONV_W)
    vv_ref[...] = vv3
    t_idx = lax.broadcasted_iota(jnp.int32, (tq, CONV_W), 0)
    z = jnp.broadcast_to(bsp_ref[...], (nseq, tq, CONV_W))
    for src in range(tq):
        coef = jnp.where(t_idx >= src, wc_ref[src], 0.0)
        z = z + coef * vv3[:, src:src + 1, :]
    c_out = _dot((u3 * z).reshape(rows, CONV_W).astype(BF16), w_cout_ref[...])
    merged = merged + jax.nn.sigmoid(_dot(h, w_in_ref[:, G0 + 3 * D_MODEL:G0 + 4 * D_MODEL])) * c_out

    y = _dot(merged.astype(BF16), w_o_ref[...]).reshape(nseq, tq, D_MODEL)
    x1_ref[...] = x + gate1 * y


def _seq_spec(nseq, r, c):
    return pl.BlockSpec((nseq, r, c), lambda i: (i, 0, 0))


def _sample_attention(x, mod, rope, cache_k, cache_v, lw, layer):
    nbatch, tq, _ = x.shape
    nseq = SAMPLE_ATTN_SEQS
    small = [lw["n1g"], lw["qkg"], lw["segm"]]

    def cache_spec():
        return pl.BlockSpec((None, nseq, WINDOW, KV_DIM), lambda i: (layer, i, 0, 0))

    in_specs = (
        [_seq_spec(nseq, tq, D_MODEL), _seq_spec(nseq, 6, D_MODEL),
         _const_spec(rope.shape), pl.BlockSpec(memory_space=pltpu.SMEM)]
        + [_layer_spec(a, layer) for a in small]
        + [cache_spec(), cache_spec(),
           _layer_spec(lw["wq"], layer), _layer_spec(lw["w_in"], layer, cols=KV_COLS),
           _layer_spec(lw["w_ao"], layer)])
    out_shape = (
        jax.ShapeDtypeStruct((nbatch, tq, D_MODEL), F32),
        jax.ShapeDtypeStruct((nbatch, WINDOW, KV_DIM), F32),
        jax.ShapeDtypeStruct((nbatch, WINDOW, KV_DIM), F32),
    )
    out_specs = (_seq_spec(nseq, tq, D_MODEL), _seq_spec(nseq, WINDOW, KV_DIM),
                 _seq_spec(nseq, WINDOW, KV_DIM))
    return pl.pallas_call(
        _sample_attn_kernel,
        out_shape=out_shape,
        grid=(nbatch // nseq,),
        in_specs=in_specs,
        out_specs=out_specs,
        scratch_shapes=[
            pltpu.VMEM((nseq, KEY_SPAN, KV_DIM), BF16),
            pltpu.VMEM((nseq, KEY_SPAN, KV_DIM), BF16),
        ],
        compiler_params=pltpu.CompilerParams(
            dimension_semantics=("arbitrary",),
            vmem_limit_bytes=VMEM_BYTES - 16 * 1024 * 1024),
        name="sample_attention",
    )(x, mod, rope, lw["sinks"][layer], *small, cache_k, cache_v, lw["wq"], lw["w_in"], lw["w_ao"])


def _sample_rest(x, mod, attn_out, state_a, state_b, lw, layer):
    nbatch, tq, _ = x.shape
    nseq = SAMPLE_SEQS
    small = [lw["n1g"], lw["caw"], lw["v512"], lw["wc8"], lw["bsp8"]]
    weights = [lw["w_in"], lw["w_aout"], lw["w_bout"], lw["w_cout"], lw["w_o"]]
    def state_spec(r):
        return pl.BlockSpec((None, nseq, r, CONV_W), lambda i: (layer, i, 0, 0))

    in_specs = (
        [_seq_spec(nseq, tq, D_MODEL), _seq_spec(nseq, 6, D_MODEL), _seq_spec(nseq, tq, D_MODEL)]
        + [_layer_spec(a, layer) for a in small]
        + [state_spec(CONF_CONV_W - 1), state_spec(SC_CONV_W - 1)]
        + [_layer_spec(a, layer) for a in weights])
    out_shape = (
        jax.ShapeDtypeStruct((nbatch, tq, D_MODEL), F32),
        jax.ShapeDtypeStruct((nbatch, CONF_CONV_W - 1, CONV_W), F32),
        jax.ShapeDtypeStruct((nbatch, SC_CONV_W - 1, CONV_W), F32),
        jax.ShapeDtypeStruct((nbatch, tq, CONV_W), F32),
    )
    out_specs = (_seq_spec(nseq, tq, D_MODEL), _seq_spec(nseq, CONF_CONV_W - 1, CONV_W),
                 _seq_spec(nseq, SC_CONV_W - 1, CONV_W), _seq_spec(nseq, tq, CONV_W))
    return pl.pallas_call(
        _sample_rest_kernel,
        out_shape=out_shape,
        grid=(nbatch // nseq,),
        in_specs=in_specs,
        out_specs=out_specs,
        scratch_shapes=[
            pltpu.VMEM((nseq, A_PAD + tq, CONV_W), F32),
            pltpu.VMEM((nseq, B_PAD + tq, CONV_W), F32),
        ],
        compiler_params=pltpu.CompilerParams(
            dimension_semantics=("arbitrary",),
            vmem_limit_bytes=VMEM_BYTES - 8 * 1024 * 1024),
        name="sample_rest",
    )(x, mod, attn_out, *small, state_a, state_b, *weights)


def _mlp_kernel(x_ref, mod_ref, g_ref, w_up_ref, w_down_ref, o_ref):
    x = x_ref[...]
    if x.ndim == 3:
        shift2, scale2, gate2 = mod_ref[:, 3:4, :], mod_ref[:, 4:5, :], mod_ref[:, 5:6, :]
    else:
        shift2, scale2, gate2 = mod_ref[3:4, :], mod_ref[4:5, :], mod_ref[5:6, :]
    h2 = _rms_mod(x, g_ref[...], scale2, shift2)
    up = _dot(h2.reshape(-1, D_MODEL).astype(BF16), w_up_ref[...])
    act = jnp.square(jnp.maximum(up, 0.0)).astype(BF16)
    y = _dot(act, w_down_ref[...]).reshape(x.shape)
    o_ref[...] = x + gate2 * y


def _mlp(x, mod, lw, layer, per_sequence):
    if per_sequence:
        nbatch, tq, _ = x.shape
        nseq = MLP_TILE // tq
        grid = (nbatch // nseq,)
        x_spec = pl.BlockSpec((nseq, tq, D_MODEL), lambda i: (i, 0, 0))
        mod_spec = pl.BlockSpec((nseq, 6, D_MODEL), lambda i: (i, 0, 0))
    else:
        bsz, seq, _ = x.shape
        grid = (bsz, seq // MLP_TILE)
        x_spec = pl.BlockSpec((None, MLP_TILE, D_MODEL), lambda b, j: (b, j, 0))
        mod_spec = pl.BlockSpec((None, 6, D_MODEL), lambda b, j: (b, 0, 0))
    return pl.pallas_call(
        _mlp_kernel,
        out_shape=jax.ShapeDtypeStruct(x.shape, F32),
        grid=grid,
        in_specs=[x_spec, mod_spec, _layer_spec(lw["n2g"], layer),
                  _layer_spec(lw["w_up"], layer), _layer_spec(lw["w_down"], layer)],
        out_specs=x_spec,
        compiler_params=pltpu.CompilerParams(
            dimension_semantics=("arbitrary",) * len(grid),
            vmem_limit_bytes=VMEM_BYTES - 16 * 1024 * 1024),
        name="mlp_sample" if per_sequence else "mlp_prompt",
    )(x, mod, lw["n2g"], lw["w_up"], lw["w_down"])


def _rope_tables(pos):
    half = ROT_DIM // 2
    inv_freq = jnp.power(ROPE_THETA, -jnp.arange(half, dtype=F32) * 2.0 / ROT_DIM)
    ang = pos.astype(F32)[:, None] * inv_freq[None, :]
    cos, sin = jnp.cos(ang), jnp.sin(ang)
    lane = jnp.arange(LANES) % HEAD_DIM
    f = lane % half
    cos_t = jnp.where(lane < ROT_DIM, cos[:, f], 1.0)
    sin_a = jnp.where(lane < half, -sin[:, f], 0.0)
    sin_b = jnp.where((lane >= half) & (lane < ROT_DIM), sin[:, f], 0.0)
    return jnp.stack([cos_t, sin_a, sin_b]).astype(F32)


def _prepare_weights(p):
    depth = p["w_in"].shape[0]
    w_in = p["w_in"]
    wq = w_in[:, :, :D_MODEL].reshape(depth, D_MODEL, N_KV_HEADS, Q_REP, HEAD_DIM)
    wq = wq.transpose(0, 1, 3, 2, 4).reshape(depth, D_MODEL, D_MODEL).astype(BF16)
    w_rest = w_in[:, :, D_MODEL:].astype(BF16)
    w_ao = p["w_attn_o"].reshape(depth, N_KV_HEADS, Q_REP, HEAD_DIM, D_MODEL)
    w_ao = w_ao.transpose(0, 2, 1, 3, 4).reshape(depth, D_MODEL, D_MODEL).astype(BF16)
    ws = p["w_spatial"]
    n_pairs = GMLP_GROUPS // 2
    wpair = ws.reshape(depth, n_pairs, 2, GMLP_CHUNK, GMLP_CHUNK).transpose(0, 1, 3, 2, 4)
    wpair = wpair.reshape(depth, n_pairs, GMLP_CHUNK, 2 * GMLP_CHUNK)
    bsp = jnp.repeat(p["b_spatial"].transpose(0, 2, 1), GMLP_GROUP_DIM, axis=2)
    wc8 = jnp.repeat(ws[:, :, :DEC_SEQ, :DEC_SEQ].transpose(0, 3, 2, 1),
                     GMLP_GROUP_DIM, axis=3)
    lane_head = jnp.arange(LANES) // HEAD_DIM
    segm = jnp.where(lane_head[:, None] == lane_head[None, :], 1.0 / HEAD_DIM, 0.0).astype(BF16)
    caw = jnp.concatenate([p["conv_a_w"], jnp.zeros((depth, 1, CONV_W), F32)], axis=1)
    v512 = jnp.concatenate([p["conv_a_b"][:, None], p["ln_a_g"][:, None], p["ln_a_b"][:, None],
                            p["ln_c_g"][:, None], p["ln_c_b"][:, None], p["conv_b_w"]], axis=1)
    reps = LANES // HEAD_DIM
    return {
        "wq": wq, "w_in": w_rest, "w_ao": w_ao,
        "w_aout": p["w_a_out"].astype(BF16), "w_bout": p["w_b_out"].astype(BF16),
        "w_cout": p["w_c_out"].astype(BF16), "w_o": p["w_o"].astype(BF16),
        "w_up": p["w_up"].astype(BF16), "w_down": p["w_down"].astype(BF16),
        "n1g": p["norm1_g"][:, None], "n2g": p["norm2_g"][:, None],
        "qkg": jnp.stack([jnp.tile(p["q_norm_g"], (1, reps)),
                          jnp.tile(p["k_norm_g"], (1, reps))], axis=1),
        "caw": caw, "v512": v512, "wpair": wpair, "bsp": bsp, "wc8": wc8,
        "bsp8": bsp[:, :DEC_SEQ],
        "segm": jnp.broadcast_to(segm, (depth, LANES, LANES)), "sinks": p["attn_sinks"],
    }


def kernel(x_prompt, x_sample, c_prompt, c_sample, state_conv_a, state_conv_b, cache_k_win, cache_v_win, w_ada, b_ada, norm1_g, norm2_g, w_in, q_norm_g, k_norm_g, attn_sinks, w_attn_o, conv_a_w, conv_a_b, ln_a_g, ln_a_b, w_a_out, conv_b_w, w_b_out, ln_c_g, ln_c_b, w_spatial, b_spatial, w_c_out, w_o, w_up, w_down):
    params = dict(w_in=w_in, w_attn_o=w_attn_o, w_spatial=w_spatial, b_spatial=b_spatial,
                  conv_a_w=conv_a_w, conv_a_b=conv_a_b, ln_a_g=ln_a_g, ln_a_b=ln_a_b,
                  ln_c_g=ln_c_g, ln_c_b=ln_c_b, conv_b_w=conv_b_w, w_a_out=w_a_out,
                  w_b_out=w_b_out, w_c_out=w_c_out, w_o=w_o, w_up=w_up, w_down=w_down,
                  norm1_g=norm1_g, norm2_g=norm2_g, q_norm_g=q_norm_g, k_norm_g=k_norm_g,
                  attn_sinks=attn_sinks)
    depth = w_in.shape[0]
    bp, seq, _ = x_prompt.shape
    bs, tq, _ = x_sample.shape

    c_all = jnp.concatenate([c_prompt, c_sample,
                             jnp.zeros((MOD_ROWS - bp - bs, D_MODEL), F32)], axis=0)
    mod_all = _modulation(c_all, w_ada, b_ada)
    rope_p = _rope_tables(jnp.arange(seq))
    rope_s = jnp.tile(_rope_tables(PAST_LEN + jnp.arange(tq)), (1, SAMPLE_ATTN_SEQS, 1))
    cache_k = cache_k_win.reshape(depth, bs, WINDOW, KV_DIM)
    cache_v = cache_v_win.reshape(depth, bs, WINDOW, KV_DIM)

    xp, xs = x_prompt, x_sample
    outs = [[] for _ in range(9)]
    lw = _prepare_weights(params)
    for l in range(depth):
        mod_p = mod_all[l, :bp].reshape(bp, 6, D_MODEL)
        mod_s = mod_all[l, bp:bp + bs].reshape(bs, 6, D_MODEL)
        xp, pa, pb, pk, pv = _mixer_prompt(xp, mod_p, rope_p, lw, l)
        xp = _mlp(xp, mod_p, lw, l, per_sequence=False)
        s_attn, sk, sv = _sample_attention(xs, mod_s, rope_s, cache_k, cache_v, lw, l)
        xs, sa, sb, sg = _sample_rest(xs, mod_s, s_attn, state_conv_a, state_conv_b, lw, l)
        xs = _mlp(xs, mod_s, lw, l, per_sequence=True)
        kv_shape_p = (bp, WINDOW, N_KV_HEADS, HEAD_DIM)
        kv_shape_s = (bs, WINDOW, N_KV_HEADS, HEAD_DIM)
        for lst, val in zip(outs, (pa, sa, pb, sb, pk.reshape(kv_shape_p), sk.reshape(kv_shape_s),
                                   pv.reshape(kv_shape_p), sv.reshape(kv_shape_s), sg)):
            lst.append(val)
    return (xp, xs) + tuple(jnp.stack(o) for o in outs)
```

```python
import functools

import jax
import jax.numpy as jnp
from jax import lax
from jax.experimental import pallas as pl
from jax.experimental.pallas import tpu as pltpu

F32 = jnp.float32
BF16 = jnp.bfloat16

D_MODEL = 1024
HEAD_DIM = 64
N_HEADS = 16
N_KV_HEADS = 4
Q_REP = 4
KV_DIM = N_KV_HEADS * HEAD_DIM
ROT_DIM = 16
ROPE_THETA = 500000.0
WINDOW = 128
ATTN_BLOCK = 128
KEY_SPAN = 2 * ATTN_BLOCK
CONV_W = 512
CONF_CONV_W = 31
SC_CONV_W = 3
GMLP_CHUNK = 128
GMLP_GROUPS = 8
GMLP_GROUP_DIM = 64
D_FF = 4 * D_MODEL
EPS = 1e-6
PAST_LEN = 16384
DEC_SEQ = 8
NEG = -1e30

K0 = D_MODEL
V0 = K0 + KV_DIM
A0 = V0 + KV_DIM
B0 = A0 + 2 * CONV_W
C0 = B0 + 3 * CONV_W
G0 = C0 + 2 * CONV_W

LANES = 128
SUBLANES = 8
VMEM_BYTES = 64 * 1024 * 1024

PROMPT_TILE = 256
MLP_TILE = 512
SAMPLE_SEQS = 32
SAMPLE_ATTN_SEQS = 16
CONV_ROWS = 64
A_PAD = 32
B_PAD = 8
MOD_ROWS = 136


def _dot(a, b):
    return jnp.dot(a, b, preferred_element_type=F32)


def _rms_mod(x, g, scale, shift):
    ms = jnp.mean(x * x, axis=-1, keepdims=True)
    y = x * lax.rsqrt(ms + EPS)
    return (y * g) * (1.0 + scale) + shift


def _layer_norm(x, g, b):
    mu = jnp.mean(x, axis=-1, keepdims=True)
    xc = x - mu
    var = jnp.mean(xc * xc, axis=-1, keepdims=True)
    return xc * lax.rsqrt(var + EPS) * g + b


def _head_norm_rope(xc, gain, seg_mean, cos_t, sin_a, sin_b):
    ms = _dot((xc * xc).astype(BF16), seg_mean)
    y = xc * lax.rsqrt(ms + EPS) * gain
    return (y * cos_t + pltpu.roll(y, LANES - ROT_DIM // 2, 1) * sin_a
            + pltpu.roll(y, ROT_DIM // 2, 1) * sin_b)


def _qkv_heads(q_raw, kv_raw, qkg_ref, seg_mean, cos_t, sin_a, sin_b):
    def chunk(src, c, gain):
        return _head_norm_rope(src[:, c * LANES:(c + 1) * LANES], gain, seg_mean,
                               cos_t, sin_a, sin_b)
    q = jnp.concatenate([chunk(q_raw, c, qkg_ref[0:1, :]) for c in range(D_MODEL // LANES)],
                        axis=1)
    k = jnp.concatenate([chunk(kv_raw, c, qkg_ref[1:2, :])
                         for c in range(KV_DIM // LANES)], axis=1)
    return q, k, kv_raw[:, KV_DIM:2 * KV_DIM]


def _kv_head_masks(scale):
    lane = lax.broadcasted_iota(jnp.int32, (1, KV_DIM), 1)
    return [jnp.where((lane >= g * HEAD_DIM) & (lane < (g + 1) * HEAD_DIM), scale, 0.0).astype(F32)
            for g in range(N_KV_HEADS)]


def _softmax_with_sink(s, sink):
    m = jnp.maximum(jnp.max(s, axis=-1, keepdims=True), sink)
    e = jnp.exp(s - m)
    den = jnp.sum(e, axis=-1, keepdims=True) + jnp.exp(sink - m)
    return e * (1.0 / den)


def _causal_conv_rows(hist_ref, row0, rows, w_ref, n_taps, base, bias):
    out = jnp.broadcast_to(bias, (rows, hist_ref.shape[1]))
    for phase in range(SUBLANES):
        span = rows if phase == 0 else rows + SUBLANES
        part = None
        for tap in range(n_taps):
            offset = base + tap
            if offset % SUBLANES != phase:
                continue
            lo = row0 + offset - phase
            term = w_ref[tap:tap + 1, :] * hist_ref[lo:lo + span, :]
            part = term if part is None else part + term
        if part is not None:
            out = out + part[phase:phase + rows]
    return out


def _mod_kernel(c_ref, w_ref, b_ref, o_ref):
    c = c_ref[...]
    s = (c * jax.nn.sigmoid(c)).astype(BF16)
    o_ref[...] = _dot(s, w_ref[...].astype(BF16)) + b_ref[...]


def _modulation(c_all, w_ada, b_ada):
    depth = w_ada.shape[0]
    n_chunks = w_ada.shape[2] // D_MODEL
    return pl.pallas_call(
        _mod_kernel,
        out_shape=jax.ShapeDtypeStruct((depth, MOD_ROWS, n_chunks * D_MODEL), F32),
        grid=(depth, n_chunks),
        in_specs=[
            pl.BlockSpec((MOD_ROWS, D_MODEL), lambda l, j: (0, 0)),
            pl.BlockSpec((None, D_MODEL, D_MODEL), lambda l, j: (l, 0, j)),
            pl.BlockSpec((None, 1, D_MODEL), lambda l, j: (l, 0, j)),
        ],
        out_specs=pl.BlockSpec((None, MOD_ROWS, D_MODEL), lambda l, j: (l, 0, j)),
        compiler_params=pltpu.CompilerParams(
            dimension_semantics=("arbitrary", "arbitrary"),
            vmem_limit_bytes=32 * 1024 * 1024),
        name="adaln_modulation",
    )(c_all, w_ada, b_ada.reshape(depth, 1, -1))


def _block_scores(q_blk, k_win, cm_scaled):
    pieces = [q_blk[:, r * KV_DIM:(r + 1) * KV_DIM] * cm_scaled[g]
              for r in range(Q_REP) for g in range(N_KV_HEADS)]
    qbd = jnp.concatenate(pieces, axis=0).astype(BF16)
    return lax.dot_general(qbd, k_win, (((1,), (1,)), ((), ())), preferred_element_type=F32)


def _block_probs(s, bias, sink_ref):
    tq = bias.shape[0]
    probs = []
    for r in range(Q_REP):
        for g in range(N_KV_HEADS):
            i = r * N_KV_HEADS + g
            sp = s[i * tq:(i + 1) * tq] + bias
            probs.append(_softmax_with_sink(sp, sink_ref[g * Q_REP + r]).astype(BF16))
    return jnp.concatenate(probs, axis=0)


def _block_output(p, v_win, cm_plain):
    tq = p.shape[0] // N_HEADS
    o = _dot(p, v_win)
    slabs = []
    for r in range(Q_REP):
        acc = None
        for g in range(N_KV_HEADS):
            i = r * N_KV_HEADS + g
            part = o[i * tq:(i + 1) * tq] * cm_plain[g]
            acc = part if acc is None else acc + part
        slabs.append(acc)
    return jnp.concatenate(slabs, axis=1)


def _mixer_prompt_kernel(x_ref, *refs):
    n_shared = 16
    mod_ref, shared, per_seq = refs[0], refs[1:1 + n_shared], refs[1 + n_shared:]
    waiting = [_mixer_prompt_sequence(x_ref.at[b], mod_ref.at[b], *shared,
                                      *[r.at[b] for r in per_seq])
               for b in range(x_ref.shape[0])]
    running = []
    while waiting or running:
        if waiting:
            running.append(waiting.pop(0))
        for gen in list(running):
            if next(gen, "done") == "done":
                running.remove(gen)


def _mixer_prompt_sequence(x_ref, mod_ref, rope_ref, sink_ref, n1g_ref, qkg_ref, caw_ref, v512_ref,
                           wpair_ref, bsp_ref, segm_ref,
                           wq_ref, w_in_ref, w_ao_ref, w_aout_ref, w_bout_ref, w_cout_ref, w_o_ref,
                           x1_ref, ca_ref, cb_ref, ko_ref, vo_ref,
                           kprev_s, vprev_s, aprev_s, bprev_s, acat_s, bcat_s, bias_s):
    tile = x_ref.shape[0]
    j = pl.program_id(0)
    last = j == pl.num_programs(0) - 1
    rd = j & 1
    wr = 1 - rd

    @pl.when(j == 0)
    def _():
        kprev_s[0] = jnp.zeros((ATTN_BLOCK, KV_DIM), BF16)
        vprev_s[0] = jnp.zeros((ATTN_BLOCK, KV_DIM), BF16)
        aprev_s[0] = jnp.zeros((A_PAD, CONV_W), F32)
        bprev_s[0] = jnp.zeros((B_PAD, CONV_W), F32)

    x = x_ref[...]
    shift1, scale1, gate1 = mod_ref[0:1, :], mod_ref[1:2, :], mod_ref[2:3, :]
    h = _rms_mod(x, n1g_ref[...], scale1, shift1).astype(BF16)

    def gate_logits(branch):
        lo = G0 + branch * D_MODEL
        return _dot(h, w_in_ref[:, lo:lo + D_MODEL])

    yield
    q_raw = _dot(h, wq_ref[...])
    kv_raw = _dot(h, w_in_ref[:, K0:A0])
    a_in = _dot(h, w_in_ref[:, A0:A0 + 2 * CONV_W])
    yield
    q, k, v = _qkv_heads(q_raw, kv_raw, qkg_ref, segm_ref[...],
                         rope_ref[0], rope_ref[1], rope_ref[2])
    kcat = jnp.concatenate([kprev_s[rd], k.astype(BF16)], axis=0)
    vcat = jnp.concatenate([vprev_s[rd], v.astype(BF16)], axis=0)

    @pl.when(last)
    def _():
        ko_ref[...] = k[tile - WINDOW:tile]
        vo_ref[...] = v[tile - WINDOW:tile]

    tt = lax.broadcasted_iota(jnp.int32, (ATTN_BLOCK, KEY_SPAN), 0)
    jj = lax.broadcasted_iota(jnp.int32, (ATTN_BLOCK, KEY_SPAN), 1)
    dist = jj - tt
    bias_rest = jnp.where((dist >= 0) & (dist <= WINDOW), 0.0, NEG).astype(F32)

    @pl.when(j == 0)
    def _():
        bias_s[...] = jnp.where(jj < ATTN_BLOCK, NEG, bias_rest)

    @pl.when(j == 1)
    def _():
        bias_s[...] = bias_rest

    bias_first = bias_s[...]
    cm_scaled = _kv_head_masks(HEAD_DIM ** -0.5)
    cm_plain = _kv_head_masks(1.0)
    kprev_s[wr] = kcat[tile:tile + ATTN_BLOCK]
    vprev_s[wr] = vcat[tile:tile + ATTN_BLOCK]
    n_blocks = tile // ATTN_BLOCK

    def scores(i):
        lo = i * ATTN_BLOCK
        return _block_scores(q[lo:lo + ATTN_BLOCK], kcat[lo:lo + KEY_SPAN], cm_scaled)

    glu = a_in[:, :CONV_W] * jax.nn.sigmoid(a_in[:, CONV_W:])
    acat_s[0:A_PAD, :] = aprev_s[rd]
    acat_s[A_PAD:A_PAD + tile, :] = glu
    aprev_s[wr] = glu[tile - A_PAD:tile]
    base = A_PAD - (CONF_CONV_W - 1)

    @pl.when(last)
    def _():
        ca_ref[...] = acat_s[base + tile:A_PAD + tile, :]

    n_chunks = tile // CONV_ROWS
    conv_rows = []

    def conv_chunks(lo, hi):
        for c in range(lo, hi):
            conv_rows.append(_causal_conv_rows(acat_s, c * CONV_ROWS, CONV_ROWS, caw_ref,
                                               CONF_CONV_W, base, v512_ref[0:1, :]))

    fillers = [lambda: _dot(h, w_in_ref[:, B0:B0 + 3 * CONV_W]),
               lambda: _dot(h, w_in_ref[:, C0:C0 + 2 * CONV_W])]
    filled = []
    blocks = []
    p = None
    yield
    for i in range(n_blocks + 1):
        if i > 0:
            lo = (i - 1) * ATTN_BLOCK
            blocks.append(_block_output(p, vcat[lo:lo + KEY_SPAN], cm_plain))
        if i < n_blocks:
            s_blk = scores(i)
        if i < len(fillers):
            filled.append(fillers[i]())
        if i == n_blocks:
            for f in fillers[len(filled):]:
                filled.append(f())
            g0 = gate_logits(0)
            g1 = gate_logits(1)
        yield
        if i < n_blocks:
            p = _block_probs(s_blk, bias_first if i == 0 else bias_rest, sink_ref)
            conv_chunks(i, i + 1)
            yield
    b_in, c_in = filled
    conv_chunks(n_blocks, n_chunks)
    attn = jnp.concatenate(blocks, axis=0).astype(BF16)
    a_conv = jnp.concatenate(conv_rows, axis=0)
    a_act = jax.nn.silu(_layer_norm(a_conv, v512_ref[1:2, :], v512_ref[2:3, :])).astype(BF16)
    yield
    attn_out = _dot(attn, w_ao_ref[...])
    a_out = _dot(a_act, w_aout_ref[...])
    g2 = gate_logits(2)
    yield

    ch = b_in[:, CONV_W:2 * CONV_W] * b_in[:, 2 * CONV_W:]
    bcat_s[0:B_PAD, :] = bprev_s[rd]
    bcat_s[B_PAD:B_PAD + tile, :] = ch
    bprev_s[wr] = ch[tile - B_PAD:tile]
    base_b = B_PAD - (SC_CONV_W - 1)
    sc = None
    for t in range(SC_CONV_W):
        term = v512_ref[5 + t:6 + t, :] * bcat_s[base_b + t:base_b + t + tile, :]
        sc = term if sc is None else sc + term

    @pl.when(last)
    def _():
        cb_ref[...] = bcat_s[base_b + tile:B_PAD + tile, :]

    merged = jax.nn.sigmoid(g0) * attn_out
    merged = merged + jax.nn.sigmoid(g1) * a_out

    zc = jax.nn.gelu(c_in)
    u = zc[:, :CONV_W]
    vv = _layer_norm(zc[:, CONV_W:], v512_ref[3:4, :], v512_ref[4:5, :])
    row = lax.broadcasted_iota(jnp.int32, (GMLP_CHUNK, 2 * GMLP_CHUNK), 0)
    col = lax.broadcasted_iota(jnp.int32, (GMLP_CHUNK, 2 * GMLP_CHUNK), 1)
    causal = (col & (GMLP_CHUNK - 1)) <= row
    n_pairs = GMLP_GROUPS // 2
    wpair = [jnp.where(causal, wpair_ref[p], 0.0).astype(BF16) for p in range(n_pairs)]
    low = lax.broadcasted_iota(jnp.int32, (1, LANES), 1) < GMLP_GROUP_DIM
    rhs_blocks = []
    for c in range(tile // GMLP_CHUNK):
        vch = vv[c * GMLP_CHUNK:(c + 1) * GMLP_CHUNK]
        for pair in range(n_pairs):
            blk = vch[:, pair * LANES:(pair + 1) * LANES]
            rhs = jnp.concatenate([jnp.where(low, blk, 0.0), jnp.where(low, 0.0, blk)], axis=0)
            rhs_blocks.append(rhs.astype(BF16))
    b_act = (b_in[:, :CONV_W] * sc).astype(BF16)
    yield
    b_out = _dot(b_act, w_bout_ref[...])
    zs = [_dot(wpair[i % n_pairs], rhs) for i, rhs in enumerate(rhs_blocks)]
    g3 = gate_logits(3)
    yield
    gated = []
    for c in range(tile // GMLP_CHUNK):
        z = jnp.concatenate(zs[c * n_pairs:(c + 1) * n_pairs], axis=1) + bsp_ref[...]
        gated.append(u[c * GMLP_CHUNK:(c + 1) * GMLP_CHUNK] * z)
    gated = jnp.concatenate(gated, axis=0).astype(BF16)
    merged = merged + jax.nn.sigmoid(g2) * b_out
    yield
    c_out = _dot(gated, w_cout_ref[...])
    yield
    merged = (merged + jax.nn.sigmoid(g3) * c_out).astype(BF16)
    yield
    x1_ref[...] = x + gate1 * _dot(merged, w_o_ref[...])


def _const_spec(shape):
    zeros = (0,) * len(shape)
    return pl.BlockSpec(shape, lambda *_: zeros, pipeline_mode=pl.Buffered(1))


def _layer_spec(stacked, layer, cols=None):
    if cols is None:
        block, col_block = stacked.shape[1:], 0
    else:
        start, width = cols
        assert start % width == 0
        block, col_block = stacked.shape[1:-1] + (width,), start // width
    index = (layer,) + (0,) * (len(block) - 1) + (col_block,)
    return pl.BlockSpec((None,) + block, lambda *_: index, pipeline_mode=pl.Buffered(1))


def _mixer_prompt(x, mod, rope, lw, layer):
    bsz, seq, _ = x.shape
    tile = PROMPT_TILE
    small = [lw["n1g"], lw["qkg"], lw["caw"], lw["v512"], lw["wpair"], lw["bsp"], lw["segm"]]
    weights = [lw["wq"], lw["w_in"], lw["w_ao"], lw["w_aout"], lw["w_bout"], lw["w_cout"],
               lw["w_o"]]
    def whole(r, c):
        return pl.BlockSpec((bsz, r, c), lambda j: (0, 0, 0))

    in_specs = (
        [pl.BlockSpec((bsz, tile, D_MODEL), lambda j: (0, j, 0)),
         whole(6, D_MODEL),
         pl.BlockSpec((3, tile, LANES), lambda j: (0, j, 0)),
         pl.BlockSpec(memory_space=pltpu.SMEM)]
        + [_layer_spec(a, layer) for a in small]
        + [_layer_spec(a, layer) for a in weights])
    out_shape = (
        jax.ShapeDtypeStruct((bsz, seq, D_MODEL), F32),
        jax.ShapeDtypeStruct((bsz, CONF_CONV_W - 1, CONV_W), F32),
        jax.ShapeDtypeStruct((bsz, SC_CONV_W - 1, CONV_W), F32),
        jax.ShapeDtypeStruct((bsz, WINDOW, KV_DIM), F32),
        jax.ShapeDtypeStruct((bsz, WINDOW, KV_DIM), F32),
    )
    out_specs = (
        pl.BlockSpec((bsz, tile, D_MODEL), lambda j: (0, j, 0)),
        whole(CONF_CONV_W - 1, CONV_W), whole(SC_CONV_W - 1, CONV_W),
        whole(WINDOW, KV_DIM), whole(WINDOW, KV_DIM),
    )
    return pl.pallas_call(
        _mixer_prompt_kernel,
        out_shape=out_shape,
        grid=(seq // tile,),
        in_specs=in_specs,
        out_specs=out_specs,
        scratch_shapes=[
            pltpu.VMEM((bsz, 2, ATTN_BLOCK, KV_DIM), BF16),
            pltpu.VMEM((bsz, 2, ATTN_BLOCK, KV_DIM), BF16),
            pltpu.VMEM((bsz, 2, A_PAD, CONV_W), F32),
            pltpu.VMEM((bsz, 2, B_PAD, CONV_W), F32),
            pltpu.VMEM((bsz, A_PAD + tile, CONV_W), F32),
            pltpu.VMEM((bsz, B_PAD + tile, CONV_W), F32),
            pltpu.VMEM((bsz, ATTN_BLOCK, KEY_SPAN), F32),
        ],
        compiler_params=pltpu.CompilerParams(
            dimension_semantics=("arbitrary",),
            vmem_limit_bytes=VMEM_BYTES - 5 * 1024 * 1024),
        name="mixer_prompt",
    )(x, mod, rope, lw["sinks"][layer], *small, *weights)


SAMPLE_NEW_ROWS = 16


def _sample_attn_kernel(x_ref, mod_ref, rope_ref, sink_ref, n1g_ref, qkg_ref, segm_ref,
                        ck_ref, cv_ref, wq_ref, wkv_ref, w_ao_ref,
                        ao_ref, nk_ref, nv_ref, kcat_s, vcat_s):
    nseq, tq, _ = x_ref.shape
    rows = nseq * tq
    cache_lo = SAMPLE_NEW_ROWS
    cache_hi = cache_lo + WINDOW

    x = x_ref[...]
    shift1, scale1 = mod_ref[:, 0:1, :], mod_ref[:, 1:2, :]
    h = _rms_mod(x, n1g_ref[...], scale1, shift1).reshape(rows, D_MODEL).astype(BF16)

    q, k, v = _qkv_heads(_dot(h, wq_ref[...]), _dot(h, wkv_ref[...]), qkg_ref, segm_ref[...],
                         rope_ref[0], rope_ref[1], rope_ref[2])
    k3 = k.reshape(nseq, tq, KV_DIM)
    v3 = v.reshape(nseq, tq, KV_DIM)
    keep = WINDOW - tq
    nk_ref[:, 0:keep, :] = ck_ref[:, tq:WINDOW, :]
    nk_ref[:, keep:WINDOW, :] = k3
    nv_ref[:, 0:keep, :] = cv_ref[:, tq:WINDOW, :]
    nv_ref[:, keep:WINDOW, :] = v3
    pad = jnp.zeros((nseq, SAMPLE_NEW_ROWS - tq, KV_DIM), F32)
    kcat_s[:, 0:cache_lo, :] = jnp.concatenate([k3, pad], axis=1).astype(BF16)
    vcat_s[:, 0:cache_lo, :] = jnp.concatenate([v3, pad], axis=1).astype(BF16)
    kcat_s[:, cache_lo:cache_hi, :] = ck_ref[...].astype(BF16)
    vcat_s[:, cache_lo:cache_hi, :] = cv_ref[...].astype(BF16)
    tail = jnp.zeros((nseq, KEY_SPAN - cache_hi, KV_DIM), BF16)
    kcat_s[:, cache_hi:KEY_SPAN, :] = tail
    vcat_s[:, cache_hi:KEY_SPAN, :] = tail

    n_q = N_HEADS * tq
    tt = lax.broadcasted_iota(jnp.int32, (n_q, KEY_SPAN), 0) & (tq - 1)
    ii = lax.broadcasted_iota(jnp.int32, (n_q, KEY_SPAN), 1)
    visible = ((ii < tq) & (ii <= tt)) | ((ii >= cache_lo + tt) & (ii < cache_hi))
    bias = jnp.where(visible, 0.0, NEG).astype(F32)
    cm_scaled = _kv_head_masks(HEAD_DIM ** -0.5)
    cm_plain = _kv_head_masks(1.0)
    q3 = q.reshape(nseq, tq, D_MODEL)
    qbd = jnp.concatenate([q3[:, :, r * KV_DIM:(r + 1) * KV_DIM] * cm_scaled[g]
                           for r in range(Q_REP) for g in range(N_KV_HEADS)], axis=1).astype(BF16)
    s = jnp.einsum("bqc,bkc->bqk", qbd, kcat_s[...], preferred_element_type=F32) + bias
    sink = jnp.concatenate([jnp.full((tq, 1), sink_ref[g * Q_REP + r], F32)
                            for r in range(Q_REP) for g in range(N_KV_HEADS)], axis=0)
    p = _softmax_with_sink(s, sink).astype(BF16)
    o = jnp.einsum("bqk,bkc->bqc", p, vcat_s[...], preferred_element_type=F32)
    slabs = []
    for r in range(Q_REP):
        acc = None
        for g in range(N_KV_HEADS):
            i = r * N_KV_HEADS + g
            part = o[:, i * tq:(i + 1) * tq, :] * cm_plain[g]
            acc = part if acc is None else acc + part
        slabs.append(acc)
    attn = jnp.concatenate(slabs, axis=2).reshape(rows, D_MODEL)
    ao_ref[...] = _dot(attn.astype(BF16), w_ao_ref[...]).reshape(nseq, tq, D_MODEL)


def _sample_rest_kernel(x_ref, mod_ref, ao_ref, n1g_ref, caw_ref, v512_ref, wc_ref, bsp_ref,
                        sta_ref, stb_ref, w_in_ref, w_aout_ref, w_bout_ref, w_cout_ref, w_o_ref,
                        x1_ref, na_ref, nb_ref, vv_ref, glu_s, aact_s, bcat_s):
    nseq, tq, _ = x_ref.shape
    rows = nseq * tq

    x = x_ref[...]
    shift1, scale1, gate1 = mod_ref[:, 0:1, :], mod_ref[:, 1:2, :], mod_ref[:, 2:3, :]
    h = _rms_mod(x, n1g_ref[...], scale1, shift1).reshape(rows, D_MODEL).astype(BF16)
    attn_out = ao_ref[...].reshape(rows, D_MODEL)
    merged = jax.nn.sigmoid(_dot(h, w_in_ref[:, G0:G0 + D_MODEL])) * attn_out

    a_in = _dot(h, w_in_ref[:, A0:A0 + 2 * CONV_W])
    glu = a_in[:, :CONV_W] * jax.nn.sigmoid(a_in[:, CONV_W:])
    hist_a = CONF_CONV_W - 1
    glu_s[...] = glu.reshape(nseq, tq, CONV_W)

    def conv_in(t):
        return sta_ref[t] if t < hist_a else glu_s[:, t - hist_a, :]

    for t in range(tq):
        acc = jnp.broadcast_to(v512_ref[0:1, :], (nseq, CONV_W))
        for tap in range(CONF_CONV_W):
            acc = acc + caw_ref[tap:tap + 1, :] * conv_in(t + tap)
        aact_s[:, t, :] = jax.nn.silu(_layer_norm(acc, v512_ref[1:2, :], v512_ref[2:3, :]))
    for t in range(hist_a):
        na_ref[t] = conv_in(t + tq)
    a_out = _dot(aact_s[...].reshape(rows, CONV_W).astype(BF16), w_aout_ref[...])
    merged = merged + jax.nn.sigmoid(_dot(h, w_in_ref[:, G0 + D_MODEL:G0 + 2 * D_MODEL])) * a_out

    b_in = _dot(h, w_in_ref[:, B0:B0 + 3 * CONV_W])
    hist_b = SC_CONV_W - 1
    base_b = B_PAD - hist_b
    bcat_s[:, base_b:B_PAD, :] = stb_ref[...]
    bcat_s[:, B_PAD:B_PAD + tq, :] = (b_in[:, CONV_W:2 * CONV_W]
                                      * b_in[:, 2 * CONV_W:]).reshape(nseq, tq, CONV_W)
    sc = None
    for t in range(SC_CONV_W):
        term = v512_ref[5 + t:6 + t, :] * bcat_s[:, base_b + t:base_b + t + tq, :]
        sc = term if sc is None else sc + term
    nb_ref[...] = bcat_s[:, base_b + tq:B_PAD + tq, :]
    b_out = _dot((b_in[:, :CONV_W] * sc.reshape(rows, CONV_W)).astype(BF16), w_bout_ref[...])
    merged = merged + jax.nn.sigmoid(_dot(h, w_in_ref[:, G0 + 2 * D_MODEL:G0 + 3 * D_MODEL])) * b_out

    zc = jax.nn.gelu(_dot(h, w_in_ref[:, C0:C0 + 2 * CONV_W]))
    u3 = zc[:, :CONV_W].reshape(nseq, tq, CONV_W)
    vv3 = _layer_norm(zc[:, CONV_W:], v512_ref[3:4, :], v512_ref[4:5, :]).reshape(nseq, tq, CONV_W)
    vv_ref[...] = vv3
    t_idx = lax.broadcasted_iota(jnp.int32, (tq, CONV_W), 0)
    z = jnp.broadcast_to(bsp_ref[...], (nseq, tq, CONV_W))
    for src in range(tq):
        coef = jnp.where(t_idx >= src, wc_ref[src], 0.0)
        z = z + coef * vv3[:, src:src + 1, :]
    c_out = _dot((u3 * z).reshape(rows, CONV_W).astype(BF16), w_cout_ref[...])
    merged = merged + jax.nn.sigmoid(_dot(h, w_in_ref[:, G0 + 3 * D_MODEL:G0 + 4 * D_MODEL])) * c_out

    y = _dot(merged.astype(BF16), w_o_ref[...]).reshape(nseq, tq, D_MODEL)
    x1_ref[...] = x + gate1 * y


def _seq_spec(nseq, r, c):
    return pl.BlockSpec((nseq, r, c), lambda i: (i, 0, 0))


def _sample_attention(x, mod, rope, cache_k, cache_v, lw, layer):
    nbatch, tq, _ = x.shape
    nseq = SAMPLE_ATTN_SEQS
    small = [lw["n1g"], lw["qkg"], lw["segm"]]

    def cache_spec():
        return pl.BlockSpec((None, nseq, WINDOW, KV_DIM), lambda i: (layer, i, 0, 0))

    in_specs = (
        [_seq_spec(nseq, tq, D_MODEL), _seq_spec(nseq, 6, D_MODEL),
         _const_spec(rope.shape), pl.BlockSpec(memory_space=pltpu.SMEM)]
        + [_layer_spec(a, layer) for a in small]
        + [cache_spec(), cache_spec(),
           _layer_spec(lw["wq"], layer), _layer_spec(lw["w_in"], layer, cols=(K0, 2 * KV_DIM)),
           _layer_spec(lw["w_ao"], layer)])
    out_shape = (
        jax.ShapeDtypeStruct((nbatch, tq, D_MODEL), F32),
        jax.ShapeDtypeStruct((nbatch, WINDOW, KV_DIM), F32),
        jax.ShapeDtypeStruct((nbatch, WINDOW, KV_DIM), F32),
    )
    out_specs = (_seq_spec(nseq, tq, D_MODEL), _seq_spec(nseq, WINDOW, KV_DIM),
                 _seq_spec(nseq, WINDOW, KV_DIM))
    return pl.pallas_call(
        _sample_attn_kernel,
        out_shape=out_shape,
        grid=(nbatch // nseq,),
        in_specs=in_specs,
        out_specs=out_specs,
        scratch_shapes=[
            pltpu.VMEM((nseq, KEY_SPAN, KV_DIM), BF16),
            pltpu.VMEM((nseq, KEY_SPAN, KV_DIM), BF16),
        ],
        compiler_params=pltpu.CompilerParams(
            dimension_semantics=("arbitrary",),
            vmem_limit_bytes=VMEM_BYTES - 16 * 1024 * 1024),
        name="sample_attention",
    )(x, mod, rope, lw["sinks"][layer], *small, cache_k, cache_v, lw["wq"], lw["w_in"], lw["w_ao"])


def _sample_rest(x, mod, attn_out, state_a, state_b, lw, layer):
    nbatch, tq, _ = x.shape
    nseq = SAMPLE_SEQS
    small = [lw["n1g"], lw["caw"], lw["v512"], lw["wc8"], lw["bsp8"]]
    weights = [lw["w_in"], lw["w_aout"], lw["w_bout"], lw["w_cout"], lw["w_o"]]
    hist_a = CONF_CONV_W - 1
    in_specs = (
        [_seq_spec(nseq, tq, D_MODEL), _seq_spec(nseq, 6, D_MODEL), _seq_spec(nseq, tq, D_MODEL)]
        + [_layer_spec(a, layer) for a in small]
        + [pl.BlockSpec((None, hist_a, nseq, CONV_W), lambda i: (layer, 0, i, 0)),
           pl.BlockSpec((None, nseq, SC_CONV_W - 1, CONV_W), lambda i: (layer, i, 0, 0))]
        + [_layer_spec(a, layer) for a in weights])
    out_shape = (
        jax.ShapeDtypeStruct((nbatch, tq, D_MODEL), F32),
        jax.ShapeDtypeStruct((hist_a, nbatch, CONV_W), F32),
        jax.ShapeDtypeStruct((nbatch, SC_CONV_W - 1, CONV_W), F32),
        jax.ShapeDtypeStruct((nbatch, tq, CONV_W), F32),
    )
    out_specs = (_seq_spec(nseq, tq, D_MODEL),
                 pl.BlockSpec((hist_a, nseq, CONV_W), lambda i: (0, i, 0)),
                 _seq_spec(nseq, SC_CONV_W - 1, CONV_W), _seq_spec(nseq, tq, CONV_W))
    return pl.pallas_call(
        _sample_rest_kernel,
        out_shape=out_shape,
        grid=(nbatch // nseq,),
        in_specs=in_specs,
        out_specs=out_specs,
        scratch_shapes=[
            pltpu.VMEM((nseq, tq, CONV_W), F32),
            pltpu.VMEM((nseq, tq, CONV_W), F32),
            pltpu.VMEM((nseq, B_PAD + tq, CONV_W), F32),
        ],
        compiler_params=pltpu.CompilerParams(
            dimension_semantics=("arbitrary",),
            vmem_limit_bytes=VMEM_BYTES - 8 * 1024 * 1024),
        name="sample_rest",
    )(x, mod, attn_out, *small, state_a, state_b, *weights)


def _mlp_kernel(x_ref, mod_ref, g_ref, w_up_ref, w_down_ref, o_ref):
    x = x_ref[...]
    if x.ndim == 3:
        shift2, scale2, gate2 = mod_ref[:, 3:4, :], mod_ref[:, 4:5, :], mod_ref[:, 5:6, :]
    else:
        shift2, scale2, gate2 = mod_ref[3:4, :], mod_ref[4:5, :], mod_ref[5:6, :]
    h2 = _rms_mod(x, g_ref[...], scale2, shift2)
    up = _dot(h2.reshape(-1, D_MODEL).astype(BF16), w_up_ref[...])
    act = jnp.square(jnp.maximum(up, 0.0)).astype(BF16)
    y = _dot(act, w_down_ref[...]).reshape(x.shape)
    o_ref[...] = x + gate2 * y


def _mlp(x, mod, lw, layer, per_sequence):
    if per_sequence:
        nbatch, tq, _ = x.shape
        nseq = MLP_TILE // tq
        grid = (nbatch // nseq,)
        x_spec = pl.BlockSpec((nseq, tq, D_MODEL), lambda i: (i, 0, 0))
        mod_spec = pl.BlockSpec((nseq, 6, D_MODEL), lambda i: (i, 0, 0))
    else:
        bsz, seq, _ = x.shape
        grid = (bsz, seq // MLP_TILE)
        x_spec = pl.BlockSpec((None, MLP_TILE, D_MODEL), lambda b, j: (b, j, 0))
        mod_spec = pl.BlockSpec((None, 6, D_MODEL), lambda b, j: (b, 0, 0))
    return pl.pallas_call(
        _mlp_kernel,
        out_shape=jax.ShapeDtypeStruct(x.shape, F32),
        grid=grid,
        in_specs=[x_spec, mod_spec, _layer_spec(lw["n2g"], layer),
                  _layer_spec(lw["w_up"], layer), _layer_spec(lw["w_down"], layer)],
        out_specs=x_spec,
        compiler_params=pltpu.CompilerParams(
            dimension_semantics=("arbitrary",) * len(grid),
            vmem_limit_bytes=VMEM_BYTES - 16 * 1024 * 1024),
        name="mlp_sample" if per_sequence else "mlp_prompt",
    )(x, mod, lw["n2g"], lw["w_up"], lw["w_down"])


def _rope_tables(pos):
    half = ROT_DIM // 2
    inv_freq = jnp.power(ROPE_THETA, -jnp.arange(half, dtype=F32) * 2.0 / ROT_DIM)
    ang = pos.astype(F32)[:, None] * inv_freq[None, :]
    cos, sin = jnp.cos(ang), jnp.sin(ang)
    lane = jnp.arange(LANES) % HEAD_DIM
    f = lane % half
    cos_t = jnp.where(lane < ROT_DIM, cos[:, f], 1.0)
    sin_a = jnp.where(lane < half, -sin[:, f], 0.0)
    sin_b = jnp.where((lane >= half) & (lane < ROT_DIM), sin[:, f], 0.0)
    return jnp.stack([cos_t, sin_a, sin_b]).astype(F32)


def _prepare_weights(p):
    depth = p["w_in"].shape[0]
    w_in = p["w_in"]
    wq = w_in[:, :, :D_MODEL].reshape(depth, D_MODEL, N_KV_HEADS, Q_REP, HEAD_DIM)
    wq = wq.transpose(0, 1, 3, 2, 4).reshape(depth, D_MODEL, D_MODEL).astype(BF16)
    w_ao = p["w_attn_o"].reshape(depth, N_KV_HEADS, Q_REP, HEAD_DIM, D_MODEL)
    w_ao = w_ao.transpose(0, 2, 1, 3, 4).reshape(depth, D_MODEL, D_MODEL).astype(BF16)
    ws = p["w_spatial"]
    n_pairs = GMLP_GROUPS // 2
    wpair = ws.reshape(depth, n_pairs, 2, GMLP_CHUNK, GMLP_CHUNK).transpose(0, 1, 3, 2, 4)
    wpair = wpair.reshape(depth, n_pairs, GMLP_CHUNK, 2 * GMLP_CHUNK)
    bsp = jnp.repeat(p["b_spatial"].transpose(0, 2, 1), GMLP_GROUP_DIM, axis=2)
    wc8 = jnp.repeat(ws[:, :, :DEC_SEQ, :DEC_SEQ].transpose(0, 3, 2, 1),
                     GMLP_GROUP_DIM, axis=3)
    lane_head = jnp.arange(LANES) // HEAD_DIM
    segm = jnp.where(lane_head[:, None] == lane_head[None, :], 1.0 / HEAD_DIM, 0.0).astype(BF16)
    caw = jnp.concatenate([p["conv_a_w"], jnp.zeros((depth, 1, CONV_W), F32)], axis=1)
    v512 = jnp.concatenate([p["conv_a_b"][:, None], p["ln_a_g"][:, None], p["ln_a_b"][:, None],
                            p["ln_c_g"][:, None], p["ln_c_b"][:, None], p["conv_b_w"]], axis=1)
    reps = LANES // HEAD_DIM
    return {
        "wq": wq, "w_in": w_in.astype(BF16), "w_ao": w_ao,
        "w_aout": p["w_a_out"].astype(BF16), "w_bout": p["w_b_out"].astype(BF16),
        "w_cout": p["w_c_out"].astype(BF16), "w_o": p["w_o"].astype(BF16),
        "w_up": p["w_up"].astype(BF16), "w_down": p["w_down"].astype(BF16),
        "n1g": p["norm1_g"][:, None], "n2g": p["norm2_g"][:, None],
        "qkg": jnp.stack([jnp.tile(p["q_norm_g"], (1, reps)),
                          jnp.tile(p["k_norm_g"], (1, reps))], axis=1),
        "caw": caw, "v512": v512, "wpair": wpair, "bsp": bsp, "wc8": wc8,
        "bsp8": bsp[:, :DEC_SEQ],
        "segm": jnp.broadcast_to(segm, (depth, LANES, LANES)), "sinks": p["attn_sinks"],
    }


def kernel(x_prompt, x_sample, c_prompt, c_sample, state_conv_a, state_conv_b, cache_k_win, cache_v_win, w_ada, b_ada, norm1_g, norm2_g, w_in, q_norm_g, k_norm_g, attn_sinks, w_attn_o, conv_a_w, conv_a_b, ln_a_g, ln_a_b, w_a_out, conv_b_w, w_b_out, ln_c_g, ln_c_b, w_spatial, b_spatial, w_c_out, w_o, w_up, w_down):
    params = dict(w_in=w_in, w_attn_o=w_attn_o, w_spatial=w_spatial, b_spatial=b_spatial,
                  conv_a_w=conv_a_w, conv_a_b=conv_a_b, ln_a_g=ln_a_g, ln_a_b=ln_a_b,
                  ln_c_g=ln_c_g, ln_c_b=ln_c_b, conv_b_w=conv_b_w, w_a_out=w_a_out,
                  w_b_out=w_b_out, w_c_out=w_c_out, w_o=w_o, w_up=w_up, w_down=w_down,
                  norm1_g=norm1_g, norm2_g=norm2_g, q_norm_g=q_norm_g, k_norm_g=k_norm_g,
                  attn_sinks=attn_sinks)
    depth = w_in.shape[0]
    bp, seq, _ = x_prompt.shape
    bs, tq, _ = x_sample.shape

    c_all = jnp.concatenate([c_prompt, c_sample,
                             jnp.zeros((MOD_ROWS - bp - bs, D_MODEL), F32)], axis=0)
    mod_all = _modulation(c_all, w_ada, b_ada)
    rope_p = _rope_tables(jnp.arange(seq))
    rope_s = jnp.tile(_rope_tables(PAST_LEN + jnp.arange(tq)), (1, SAMPLE_ATTN_SEQS, 1))
    state_a_tm = state_conv_a.transpose(0, 2, 1, 3)
    cache_k = cache_k_win.reshape(depth, bs, WINDOW, KV_DIM)
    cache_v = cache_v_win.reshape(depth, bs, WINDOW, KV_DIM)

    xp, xs = x_prompt, x_sample
    outs = [[] for _ in range(9)]
    lw = _prepare_weights(params)
    for l in range(depth):
        mod_p = mod_all[l, :bp].reshape(bp, 6, D_MODEL)
        mod_s = mod_all[l, bp:bp + bs].reshape(bs, 6, D_MODEL)
        xp, pa, pb, pk, pv = _mixer_prompt(xp, mod_p, rope_p, lw, l)
        xp = _mlp(xp, mod_p, lw, l, per_sequence=False)
        s_attn, sk, sv = _sample_attention(xs, mod_s, rope_s, cache_k, cache_v, lw, l)
        xs, sa, sb, sg = _sample_rest(xs, mod_s, s_attn, state_a_tm, state_conv_b, lw, l)
        sa = sa.transpose(1, 0, 2)
        xs = _mlp(xs, mod_s, lw, l, per_sequence=True)
        kv_shape_p = (bp, WINDOW, N_KV_HEADS, HEAD_DIM)
        kv_shape_s = (bs, WINDOW, N_KV_HEADS, HEAD_DIM)
        for lst, val in zip(outs, (pa, sa, pb, sb, pk.reshape(kv_shape_p), sk.reshape(kv_shape_s),
                                   pv.reshape(kv_shape_p), sv.reshape(kv_shape_s), sg)):
            lst.append(val)
    return (xp, xs) + tuple(jnp.stack(o) for o in outs)
```

```python
import functools

import jax
import jax.numpy as jnp
from jax import lax
from jax.experimental import pallas as pl
from jax.experimental.pallas import tpu as pltpu

F32 = jnp.float32
BF16 = jnp.bfloat16

D_MODEL = 1024
HEAD_DIM = 64
N_HEADS = 16
N_KV_HEADS = 4
Q_REP = 4
KV_DIM = N_KV_HEADS * HEAD_DIM
ROT_DIM = 16
ROPE_THETA = 500000.0
WINDOW = 128
ATTN_BLOCK = 128
KEY_SPAN = 2 * ATTN_BLOCK
CONV_W = 512
CONF_CONV_W = 31
SC_CONV_W = 3
GMLP_CHUNK = 128
GMLP_GROUPS = 8
GMLP_GROUP_DIM = 64
D_FF = 4 * D_MODEL
EPS = 1e-6
PAST_LEN = 16384
DEC_SEQ = 8
NEG = -1e30

K0 = D_MODEL
V0 = K0 + KV_DIM
A0 = V0 + KV_DIM
B0 = A0 + 2 * CONV_W
C0 = B0 + 3 * CONV_W
G0 = C0 + 2 * CONV_W

LANES = 128
SUBLANES = 8
VMEM_BYTES = 64 * 1024 * 1024

PROMPT_TILE = 256
MLP_TILE = 512
SAMPLE_SEQS = 32
SAMPLE_ATTN_SEQS = 16
CONV_ROWS = 64
A_PAD = 32
B_PAD = 8
MOD_ROWS = 136


def _dot(a, b):
    return jnp.dot(a, b, preferred_element_type=F32)


def _rms_mod(x, g, scale, shift):
    ms = jnp.mean(x * x, axis=-1, keepdims=True)
    y = x * lax.rsqrt(ms + EPS)
    return (y * g) * (1.0 + scale) + shift


def _layer_norm(x, g, b):
    mu = jnp.mean(x, axis=-1, keepdims=True)
    xc = x - mu
    var = jnp.mean(xc * xc, axis=-1, keepdims=True)
    return xc * lax.rsqrt(var + EPS) * g + b


def _head_norm_rope(xc, gain, seg_mean, cos_t, sin_a, sin_b):
    ms = _dot((xc * xc).astype(BF16), seg_mean)
    y = xc * lax.rsqrt(ms + EPS) * gain
    return (y * cos_t + pltpu.roll(y, LANES - ROT_DIM // 2, 1) * sin_a
            + pltpu.roll(y, ROT_DIM // 2, 1) * sin_b)


def _qkv_heads(q_raw, kv_raw, qkg_ref, seg_mean, cos_t, sin_a, sin_b):
    def chunk(src, c, gain):
        return _head_norm_rope(src[:, c * LANES:(c + 1) * LANES], gain, seg_mean,
                               cos_t, sin_a, sin_b)
    q = jnp.concatenate([chunk(q_raw, c, qkg_ref[0:1, :]) for c in range(D_MODEL // LANES)],
                        axis=1)
    k = jnp.concatenate([chunk(kv_raw, c, qkg_ref[1:2, :])
                         for c in range(KV_DIM // LANES)], axis=1)
    return q, k, kv_raw[:, KV_DIM:2 * KV_DIM]


def _kv_head_masks(scale):
    lane = lax.broadcasted_iota(jnp.int32, (1, KV_DIM), 1)
    return [jnp.where((lane >= g * HEAD_DIM) & (lane < (g + 1) * HEAD_DIM), scale, 0.0).astype(F32)
            for g in range(N_KV_HEADS)]


def _softmax_with_sink(s, sink):
    m = jnp.maximum(jnp.max(s, axis=-1, keepdims=True), sink)
    e = jnp.exp(s - m)
    den = jnp.sum(e, axis=-1, keepdims=True) + jnp.exp(sink - m)
    return e * (1.0 / den)


def _causal_conv_rows(hist_ref, row0, rows, w_ref, n_taps, base, bias):
    out = jnp.broadcast_to(bias, (rows, hist_ref.shape[1]))
    for phase in range(SUBLANES):
        span = rows if phase == 0 else rows + SUBLANES
        part = None
        for tap in range(n_taps):
            offset = base + tap
            if offset % SUBLANES != phase:
                continue
            lo = row0 + offset - phase
            term = w_ref[tap:tap + 1, :] * hist_ref[lo:lo + span, :]
            part = term if part is None else part + term
        if part is not None:
            out = out + part[phase:phase + rows]
    return out


def _mod_kernel(c_ref, w_ref, b_ref, o_ref):
    c = c_ref[...]
    s = (c * jax.nn.sigmoid(c)).astype(BF16)
    o_ref[...] = _dot(s, w_ref[...].astype(BF16)) + b_ref[...]


def _modulation(c_all, w_ada, b_ada):
    depth = w_ada.shape[0]
    n_chunks = w_ada.shape[2] // D_MODEL
    return pl.pallas_call(
        _mod_kernel,
        out_shape=jax.ShapeDtypeStruct((depth, MOD_ROWS, n_chunks * D_MODEL), F32),
        grid=(depth, n_chunks),
        in_specs=[
            pl.BlockSpec((MOD_ROWS, D_MODEL), lambda l, j: (0, 0)),
            pl.BlockSpec((None, D_MODEL, D_MODEL), lambda l, j: (l, 0, j)),
            pl.BlockSpec((None, 1, D_MODEL), lambda l, j: (l, 0, j)),
        ],
        out_specs=pl.BlockSpec((None, MOD_ROWS, D_MODEL), lambda l, j: (l, 0, j)),
        compiler_params=pltpu.CompilerParams(
            dimension_semantics=("arbitrary", "arbitrary"),
            vmem_limit_bytes=32 * 1024 * 1024),
        name="adaln_modulation",
    )(c_all, w_ada, b_ada.reshape(depth, 1, -1))


def _per_head_copies(win, head_masks):
    return jnp.concatenate([jnp.where(m, win, jnp.zeros_like(win)) for m in head_masks], axis=0)


def _block_scores(q_blk, k_heads):
    qs = (q_blk * HEAD_DIM ** -0.5).astype(BF16)
    q_rows = jnp.concatenate([qs[:, r * KV_DIM:(r + 1) * KV_DIM] for r in range(Q_REP)], axis=0)
    return lax.dot_general(q_rows, k_heads, (((1,), (1,)), ((), ())), preferred_element_type=F32)


def _block_probs(s, bias, sink_ref):
    tq, keys = bias.shape
    rows = []
    for r in range(Q_REP):
        cols = []
        for g in range(N_KV_HEADS):
            sp = s[r * tq:(r + 1) * tq, g * keys:(g + 1) * keys] + bias
            cols.append(_softmax_with_sink(sp, sink_ref[g * Q_REP + r]).astype(BF16))
        rows.append(jnp.concatenate(cols, axis=1))
    return jnp.concatenate(rows, axis=0)


def _block_output(p, v_heads):
    tq = p.shape[0] // Q_REP
    o = _dot(p, v_heads)
    return jnp.concatenate([o[r * tq:(r + 1) * tq] for r in range(Q_REP)], axis=1)


def _mixer_prompt_kernel(x_ref, *refs):
    n_shared = 16
    mod_ref, shared, per_seq = refs[0], refs[1:1 + n_shared], refs[1 + n_shared:]
    waiting = [_mixer_prompt_sequence(x_ref.at[b], mod_ref.at[b], *shared,
                                      *[r.at[b] for r in per_seq])
               for b in range(x_ref.shape[0])]
    running = []
    while waiting or running:
        if waiting:
            running.append(waiting.pop(0))
        for gen in list(running):
            if next(gen, "done") == "done":
                running.remove(gen)


def _mixer_prompt_sequence(x_ref, mod_ref, rope_ref, sink_ref, n1g_ref, qkg_ref, caw_ref, v512_ref,
                           wpair_ref, bsp_ref, segm_ref,
                           wq_ref, w_in_ref, w_ao_ref, w_aout_ref, w_bout_ref, w_cout_ref, w_o_ref,
                           x1_ref, ca_ref, cb_ref, ko_ref, vo_ref,
                           kprev_s, vprev_s, aprev_s, bprev_s, acat_s, bcat_s, bias_s):
    tile = x_ref.shape[0]
    j = pl.program_id(0)
    last = j == pl.num_programs(0) - 1
    rd = j & 1
    wr = 1 - rd

    @pl.when(j == 0)
    def _():
        kprev_s[0] = jnp.zeros((ATTN_BLOCK, KV_DIM), BF16)
        vprev_s[0] = jnp.zeros((ATTN_BLOCK, KV_DIM), BF16)
        aprev_s[0] = jnp.zeros((A_PAD, CONV_W), F32)
        bprev_s[0] = jnp.zeros((B_PAD, CONV_W), F32)

    x = x_ref[...]
    shift1, scale1, gate1 = mod_ref[0:1, :], mod_ref[1:2, :], mod_ref[2:3, :]
    h = _rms_mod(x, n1g_ref[...], scale1, shift1).astype(BF16)

    def gate_logits(branch):
        lo = G0 + branch * D_MODEL
        return _dot(h, w_in_ref[:, lo:lo + D_MODEL])

    yield
    q_raw = _dot(h, wq_ref[...])
    kv_raw = _dot(h, w_in_ref[:, K0:A0])
    a_in = _dot(h, w_in_ref[:, A0:A0 + 2 * CONV_W])
    yield
    q, k, v = _qkv_heads(q_raw, kv_raw, qkg_ref, segm_ref[...],
                         rope_ref[0], rope_ref[1], rope_ref[2])
    kcat = jnp.concatenate([kprev_s[rd], k.astype(BF16)], axis=0)
    vcat = jnp.concatenate([vprev_s[rd], v.astype(BF16)], axis=0)

    @pl.when(last)
    def _():
        ko_ref[...] = k[tile - WINDOW:tile]
        vo_ref[...] = v[tile - WINDOW:tile]

    tt = lax.broadcasted_iota(jnp.int32, (ATTN_BLOCK, KEY_SPAN), 0)
    jj = lax.broadcasted_iota(jnp.int32, (ATTN_BLOCK, KEY_SPAN), 1)
    dist = jj - tt
    bias_rest = jnp.where((dist >= 0) & (dist <= WINDOW), 0.0, NEG).astype(F32)

    @pl.when(j == 0)
    def _():
        bias_s[...] = jnp.where(jj < ATTN_BLOCK, NEG, bias_rest)

    @pl.when(j == 1)
    def _():
        bias_s[...] = bias_rest

    bias_first = bias_s[...]
    lane = lax.broadcasted_iota(jnp.int32, (1, KV_DIM), 1)
    head_masks = [(lane >= g * HEAD_DIM) & (lane < (g + 1) * HEAD_DIM) for g in range(N_KV_HEADS)]
    kprev_s[wr] = kcat[tile:tile + ATTN_BLOCK]
    vprev_s[wr] = vcat[tile:tile + ATTN_BLOCK]
    n_blocks = tile // ATTN_BLOCK

    def scores(i):
        lo = i * ATTN_BLOCK
        return _block_scores(q[lo:lo + ATTN_BLOCK],
                             _per_head_copies(kcat[lo:lo + KEY_SPAN], head_masks))

    glu = a_in[:, :CONV_W] * jax.nn.sigmoid(a_in[:, CONV_W:])
    acat_s[0:A_PAD, :] = aprev_s[rd]
    acat_s[A_PAD:A_PAD + tile, :] = glu
    aprev_s[wr] = glu[tile - A_PAD:tile]
    base = A_PAD - (CONF_CONV_W - 1)

    @pl.when(last)
    def _():
        ca_ref[...] = acat_s[base + tile:A_PAD + tile, :]

    n_chunks = tile // CONV_ROWS
    conv_rows = []

    def conv_chunks(lo, hi):
        for c in range(lo, hi):
            conv_rows.append(_causal_conv_rows(acat_s, c * CONV_ROWS, CONV_ROWS, caw_ref,
                                               CONF_CONV_W, base, v512_ref[0:1, :]))

    fillers = [lambda: _dot(h, w_in_ref[:, B0:B0 + 3 * CONV_W]),
               lambda: _dot(h, w_in_ref[:, C0:C0 + 2 * CONV_W])]
    filled = []
    blocks = []
    p = None
    yield
    for i in range(n_blocks + 1):
        if i > 0:
            lo = (i - 1) * ATTN_BLOCK
            blocks.append(_block_output(p, _per_head_copies(vcat[lo:lo + KEY_SPAN], head_masks)))
        if i < n_blocks:
            s_blk = scores(i)
        if i < len(fillers):
            filled.append(fillers[i]())
        if i == n_blocks:
            for f in fillers[len(filled):]:
                filled.append(f())
            g0 = gate_logits(0)
            g1 = gate_logits(1)
        yield
        if i < n_blocks:
            p = _block_probs(s_blk, bias_first if i == 0 else bias_rest, sink_ref)
            conv_chunks(i, i + 1)
            yield
    b_in, c_in = filled
    conv_chunks(n_blocks, n_chunks)
    attn = jnp.concatenate(blocks, axis=0).astype(BF16)
    a_conv = jnp.concatenate(conv_rows, axis=0)
    a_act = jax.nn.silu(_layer_norm(a_conv, v512_ref[1:2, :], v512_ref[2:3, :])).astype(BF16)
    yield
    attn_out = _dot(attn, w_ao_ref[...])
    a_out = _dot(a_act, w_aout_ref[...])
    g2 = gate_logits(2)
    yield

    ch = b_in[:, CONV_W:2 * CONV_W] * b_in[:, 2 * CONV_W:]
    bcat_s[0:B_PAD, :] = bprev_s[rd]
    bcat_s[B_PAD:B_PAD + tile, :] = ch
    bprev_s[wr] = ch[tile - B_PAD:tile]
    base_b = B_PAD - (SC_CONV_W - 1)
    sc = None
    for t in range(SC_CONV_W):
        term = v512_ref[5 + t:6 + t, :] * bcat_s[base_b + t:base_b + t + tile, :]
        sc = term if sc is None else sc + term

    @pl.when(last)
    def _():
        cb_ref[...] = bcat_s[base_b + tile:B_PAD + tile, :]

    merged = jax.nn.sigmoid(g0) * attn_out
    merged = merged + jax.nn.sigmoid(g1) * a_out

    zc = jax.nn.gelu(c_in)
    u = zc[:, :CONV_W]
    vv = _layer_norm(zc[:, CONV_W:], v512_ref[3:4, :], v512_ref[4:5, :])
    row = lax.broadcasted_iota(jnp.int32, (GMLP_CHUNK, 2 * GMLP_CHUNK), 0)
    col = lax.broadcasted_iota(jnp.int32, (GMLP_CHUNK, 2 * GMLP_CHUNK), 1)
    causal = (col & (GMLP_CHUNK - 1)) <= row
    n_pairs = GMLP_GROUPS // 2
    wpair = [jnp.where(causal, wpair_ref[p], 0.0).astype(BF16) for p in range(n_pairs)]
    low = lax.broadcasted_iota(jnp.int32, (1, LANES), 1) < GMLP_GROUP_DIM
    rhs_blocks = []
    for c in range(tile // GMLP_CHUNK):
        vch = vv[c * GMLP_CHUNK:(c + 1) * GMLP_CHUNK]
        for pair in range(n_pairs):
            blk = vch[:, pair * LANES:(pair + 1) * LANES]
            rhs = jnp.concatenate([jnp.where(low, blk, 0.0), jnp.where(low, 0.0, blk)], axis=0)
            rhs_blocks.append(rhs.astype(BF16))
    b_act = (b_in[:, :CONV_W] * sc).astype(BF16)
    yield
    b_out = _dot(b_act, w_bout_ref[...])
    zs = [_dot(wpair[i % n_pairs], rhs) for i, rhs in enumerate(rhs_blocks)]
    g3 = gate_logits(3)
    yield
    gated = []
    for c in range(tile // GMLP_CHUNK):
        z = jnp.concatenate(zs[c * n_pairs:(c + 1) * n_pairs], axis=1) + bsp_ref[...]
        gated.append(u[c * GMLP_CHUNK:(c + 1) * GMLP_CHUNK] * z)
    gated = jnp.concatenate(gated, axis=0).astype(BF16)
    merged = merged + jax.nn.sigmoid(g2) * b_out
    yield
    c_out = _dot(gated, w_cout_ref[...])
    yield
    merged = (merged + jax.nn.sigmoid(g3) * c_out).astype(BF16)
    yield
    x1_ref[...] = x + gate1 * _dot(merged, w_o_ref[...])


def _const_spec(shape):
    zeros = (0,) * len(shape)
    return pl.BlockSpec(shape, lambda *_: zeros, pipeline_mode=pl.Buffered(1))


def _layer_spec(stacked, layer, cols=None):
    if cols is None:
        block, col_block = stacked.shape[1:], 0
    else:
        start, width = cols
        assert start % width == 0
        block, col_block = stacked.shape[1:-1] + (width,), start // width
    index = (layer,) + (0,) * (len(block) - 1) + (col_block,)
    return pl.BlockSpec((None,) + block, lambda *_: index, pipeline_mode=pl.Buffered(1))


def _mixer_prompt(x, mod, rope, lw, layer):
    bsz, seq, _ = x.shape
    tile = PROMPT_TILE
    small = [lw["n1g"], lw["qkg"], lw["caw"], lw["v512"], lw["wpair"], lw["bsp"], lw["segm"]]
    weights = [lw["wq"], lw["w_in"], lw["w_ao"], lw["w_aout"], lw["w_bout"], lw["w_cout"],
               lw["w_o"]]
    def whole(r, c):
        return pl.BlockSpec((bsz, r, c), lambda j: (0, 0, 0))

    in_specs = (
        [pl.BlockSpec((bsz, tile, D_MODEL), lambda j: (0, j, 0)),
         whole(6, D_MODEL),
         pl.BlockSpec((3, tile, LANES), lambda j: (0, j, 0)),
         pl.BlockSpec(memory_space=pltpu.SMEM)]
        + [_layer_spec(a, layer) for a in small]
        + [_layer_spec(a, layer) for a in weights])
    out_shape = (
        jax.ShapeDtypeStruct((bsz, seq, D_MODEL), F32),
        jax.ShapeDtypeStruct((bsz, CONF_CONV_W - 1, CONV_W), F32),
        jax.ShapeDtypeStruct((bsz, SC_CONV_W - 1, CONV_W), F32),
        jax.ShapeDtypeStruct((bsz, WINDOW, KV_DIM), F32),
        jax.ShapeDtypeStruct((bsz, WINDOW, KV_DIM), F32),
    )
    out_specs = (
        pl.BlockSpec((bsz, tile, D_MODEL), lambda j: (0, j, 0)),
        whole(CONF_CONV_W - 1, CONV_W), whole(SC_CONV_W - 1, CONV_W),
        whole(WINDOW, KV_DIM), whole(WINDOW, KV_DIM),
    )
    return pl.pallas_call(
        _mixer_prompt_kernel,
        out_shape=out_shape,
        grid=(seq // tile,),
        in_specs=in_specs,
        out_specs=out_specs,
        scratch_shapes=[
            pltpu.VMEM((bsz, 2, ATTN_BLOCK, KV_DIM), BF16),
            pltpu.VMEM((bsz, 2, ATTN_BLOCK, KV_DIM), BF16),
            pltpu.VMEM((bsz, 2, A_PAD, CONV_W), F32),
            pltpu.VMEM((bsz, 2, B_PAD, CONV_W), F32),
            pltpu.VMEM((bsz, A_PAD + tile, CONV_W), F32),
            pltpu.VMEM((bsz, B_PAD + tile, CONV_W), F32),
            pltpu.VMEM((bsz, ATTN_BLOCK, KEY_SPAN), F32),
        ],
        compiler_params=pltpu.CompilerParams(
            dimension_semantics=("arbitrary",),
            vmem_limit_bytes=VMEM_BYTES - 5 * 1024 * 1024),
        name="mixer_prompt",
    )(x, mod, rope, lw["sinks"][layer], *small, *weights)


SAMPLE_NEW_ROWS = 16


def _sample_attn_kernel(x_ref, mod_ref, rope_ref, sink_ref, n1g_ref, qkg_ref, segm_ref,
                        ck_ref, cv_ref, wq_ref, wkv_ref, w_ao_ref,
                        ao_ref, nk_ref, nv_ref, kcat_s, vcat_s):
    nseq, tq, _ = x_ref.shape
    rows = nseq * tq
    cache_lo = SAMPLE_NEW_ROWS
    cache_hi = cache_lo + WINDOW

    x = x_ref[...]
    shift1, scale1 = mod_ref[:, 0:1, :], mod_ref[:, 1:2, :]
    h = _rms_mod(x, n1g_ref[...], scale1, shift1).reshape(rows, D_MODEL).astype(BF16)

    q, k, v = _qkv_heads(_dot(h, wq_ref[...]), _dot(h, wkv_ref[...]), qkg_ref, segm_ref[...],
                         rope_ref[0], rope_ref[1], rope_ref[2])
    k3 = k.reshape(nseq, tq, KV_DIM)
    v3 = v.reshape(nseq, tq, KV_DIM)
    keep = WINDOW - tq
    nk_ref[:, 0:keep, :] = ck_ref[:, tq:WINDOW, :]
    nk_ref[:, keep:WINDOW, :] = k3
    nv_ref[:, 0:keep, :] = cv_ref[:, tq:WINDOW, :]
    nv_ref[:, keep:WINDOW, :] = v3
    pad = jnp.zeros((nseq, SAMPLE_NEW_ROWS - tq, KV_DIM), F32)
    kcat_s[:, 0:cache_lo, :] = jnp.concatenate([k3, pad], axis=1).astype(BF16)
    vcat_s[:, 0:cache_lo, :] = jnp.concatenate([v3, pad], axis=1).astype(BF16)
    kcat_s[:, cache_lo:cache_hi, :] = ck_ref[...].astype(BF16)
    vcat_s[:, cache_lo:cache_hi, :] = cv_ref[...].astype(BF16)
    tail = jnp.zeros((nseq, KEY_SPAN - cache_hi, KV_DIM), BF16)
    kcat_s[:, cache_hi:KEY_SPAN, :] = tail
    vcat_s[:, cache_hi:KEY_SPAN, :] = tail

    n_q = N_HEADS * tq
    tt = lax.broadcasted_iota(jnp.int32, (n_q, KEY_SPAN), 0) & (tq - 1)
    ii = lax.broadcasted_iota(jnp.int32, (n_q, KEY_SPAN), 1)
    visible = ((ii < tq) & (ii <= tt)) | ((ii >= cache_lo + tt) & (ii < cache_hi))
    bias = jnp.where(visible, 0.0, NEG).astype(F32)
    cm_scaled = _kv_head_masks(HEAD_DIM ** -0.5)
    cm_plain = _kv_head_masks(1.0)
    q3 = q.reshape(nseq, tq, D_MODEL)
    qbd = jnp.concatenate([q3[:, :, r * KV_DIM:(r + 1) * KV_DIM] * cm_scaled[g]
                           for r in range(Q_REP) for g in range(N_KV_HEADS)], axis=1).astype(BF16)
    s = jnp.einsum("bqc,bkc->bqk", qbd, kcat_s[...], preferred_element_type=F32) + bias
    sink = jnp.concatenate([jnp.full((tq, 1), sink_ref[g * Q_REP + r], F32)
                            for r in range(Q_REP) for g in range(N_KV_HEADS)], axis=0)
    p = _softmax_with_sink(s, sink).astype(BF16)
    o = jnp.einsum("bqk,bkc->bqc", p, vcat_s[...], preferred_element_type=F32)
    slabs = []
    for r in range(Q_REP):
        acc = None
        for g in range(N_KV_HEADS):
            i = r * N_KV_HEADS + g
            part = o[:, i * tq:(i + 1) * tq, :] * cm_plain[g]
            acc = part if acc is None else acc + part
        slabs.append(acc)
    attn = jnp.concatenate(slabs, axis=2).reshape(rows, D_MODEL)
    ao_ref[...] = _dot(attn.astype(BF16), w_ao_ref[...]).reshape(nseq, tq, D_MODEL)


def _sample_rest_kernel(x_ref, mod_ref, ao_ref, n1g_ref, caw_ref, v512_ref, wc_ref, bsp_ref,
                        sta_ref, stb_ref, w_in_ref, w_aout_ref, w_bout_ref, w_cout_ref, w_o_ref,
                        x1_ref, na_ref, nb_ref, vv_ref, glu_s, aact_s, bcat_s):
    nseq, tq, _ = x_ref.shape
    rows = nseq * tq

    x = x_ref[...]
    shift1, scale1, gate1 = mod_ref[:, 0:1, :], mod_ref[:, 1:2, :], mod_ref[:, 2:3, :]
    h = _rms_mod(x, n1g_ref[...], scale1, shift1).reshape(rows, D_MODEL).astype(BF16)
    attn_out = ao_ref[...].reshape(rows, D_MODEL)
    merged = jax.nn.sigmoid(_dot(h, w_in_ref[:, G0:G0 + D_MODEL])) * attn_out

    a_in = _dot(h, w_in_ref[:, A0:A0 + 2 * CONV_W])
    glu = a_in[:, :CONV_W] * jax.nn.sigmoid(a_in[:, CONV_W:])
    hist_a = CONF_CONV_W - 1
    glu_s[...] = glu.reshape(nseq, tq, CONV_W)

    def conv_in(t):
        return sta_ref[t] if t < hist_a else glu_s[:, t - hist_a, :]

    for t in range(tq):
        acc = jnp.broadcast_to(v512_ref[0:1, :], (nseq, CONV_W))
        for tap in range(CONF_CONV_W):
            acc = acc + caw_ref[tap:tap + 1, :] * conv_in(t + tap)
        aact_s[:, t, :] = jax.nn.silu(_layer_norm(acc, v512_ref[1:2, :], v512_ref[2:3, :]))
    for t in range(hist_a):
        na_ref[t] = conv_in(t + tq)
    a_out = _dot(aact_s[...].reshape(rows, CONV_W).astype(BF16), w_aout_ref[...])
    merged = merged + jax.nn.sigmoid(_dot(h, w_in_ref[:, G0 + D_MODEL:G0 + 2 * D_MODEL])) * a_out

    b_in = _dot(h, w_in_ref[:, B0:B0 + 3 * CONV_W])
    hist_b = SC_CONV_W - 1
    base_b = B_PAD - hist_b
    bcat_s[:, base_b:B_PAD, :] = stb_ref[...]
    bcat_s[:, B_PAD:B_PAD + tq, :] = (b_in[:, CONV_W:2 * CONV_W]
                                      * b_in[:, 2 * CONV_W:]).reshape(nseq, tq, CONV_W)
    sc = None
    for t in range(SC_CONV_W):
        term = v512_ref[5 + t:6 + t, :] * bcat_s[:, base_b + t:base_b + t + tq, :]
        sc = term if sc is None else sc + term
    nb_ref[...] = bcat_s[:, base_b + tq:B_PAD + tq, :]
    b_out = _dot((b_in[:, :CONV_W] * sc.reshape(rows, CONV_W)).astype(BF16), w_bout_ref[...])
    merged = merged + jax.nn.sigmoid(_dot(h, w_in_ref[:, G0 + 2 * D_MODEL:G0 + 3 * D_MODEL])) * b_out

    zc = jax.nn.gelu(_dot(h, w_in_ref[:, C0:C0 + 2 * CONV_W]))
    u3 = zc[:, :CONV_W].reshape(nseq, tq, CONV_W)
    vv3 = _layer_norm(zc[:, CONV_W:], v512_ref[3:4, :], v512_ref[4:5, :]).reshape(nseq, tq, CONV_W)
    vv_ref[...] = vv3
    t_idx = lax.broadcasted_iota(jnp.int32, (tq, CONV_W), 0)
    z = jnp.broadcast_to(bsp_ref[...], (nseq, tq, CONV_W))
    for src in range(tq):
        coef = jnp.where(t_idx >= src, wc_ref[src], 0.0)
        z = z + coef * vv3[:, src:src + 1, :]
    c_out = _dot((u3 * z).reshape(rows, CONV_W).astype(BF16), w_cout_ref[...])
    merged = merged + jax.nn.sigmoid(_dot(h, w_in_ref[:, G0 + 3 * D_MODEL:G0 + 4 * D_MODEL])) * c_out

    y = _dot(merged.astype(BF16), w_o_ref[...]).reshape(nseq, tq, D_MODEL)
    x1_ref[...] = x + gate1 * y


def _seq_spec(nseq, r, c):
    return pl.BlockSpec((nseq, r, c), lambda i: (i, 0, 0))


def _sample_attention(x, mod, rope, cache_k, cache_v, lw, layer):
    nbatch, tq, _ = x.shape
    nseq = SAMPLE_ATTN_SEQS
    small = [lw["n1g"], lw["qkg"], lw["segm"]]

    def cache_spec():
        return pl.BlockSpec((None, nseq, WINDOW, KV_DIM), lambda i: (layer, i, 0, 0))

    in_specs = (
        [_seq_spec(nseq, tq, D_MODEL), _seq_spec(nseq, 6, D_MODEL),
         _const_spec(rope.shape), pl.BlockSpec(memory_space=pltpu.SMEM)]
        + [_layer_spec(a, layer) for a in small]
        + [cache_spec(), cache_spec(),
           _layer_spec(lw["wq"], layer), _layer_spec(lw["w_in"], layer, cols=(K0, 2 * KV_DIM)),
           _layer_spec(lw["w_ao"], layer)])
    out_shape = (
        jax.ShapeDtypeStruct((nbatch, tq, D_MODEL), F32),
        jax.ShapeDtypeStruct((nbatch, WINDOW, KV_DIM), F32),
        jax.ShapeDtypeStruct((nbatch, WINDOW, KV_DIM), F32),
    )
    out_specs = (_seq_spec(nseq, tq, D_MODEL), _seq_spec(nseq, WINDOW, KV_DIM),
                 _seq_spec(nseq, WINDOW, KV_DIM))
    return pl.pallas_call(
        _sample_attn_kernel,
        out_shape=out_shape,
        grid=(nbatch // nseq,),
        in_specs=in_specs,
        out_specs=out_specs,
        scratch_shapes=[
            pltpu.VMEM((nseq, KEY_SPAN, KV_DIM), BF16),
            pltpu.VMEM((nseq, KEY_SPAN, KV_DIM), BF16),
        ],
        compiler_params=pltpu.CompilerParams(
            dimension_semantics=("arbitrary",),
            vmem_limit_bytes=VMEM_BYTES - 16 * 1024 * 1024),
        name="sample_attention",
    )(x, mod, rope, lw["sinks"][layer], *small, cache_k, cache_v, lw["wq"], lw["w_in"], lw["w_ao"])


def _sample_rest(x, mod, attn_out, state_a, state_b, lw, layer):
    nbatch, tq, _ = x.shape
    nseq = SAMPLE_SEQS
    small = [lw["n1g"], lw["caw"], lw["v512"], lw["wc8"], lw["bsp8"]]
    weights = [lw["w_in"], lw["w_aout"], lw["w_bout"], lw["w_cout"], lw["w_o"]]
    hist_a = CONF_CONV_W - 1
    in_specs = (
        [_seq_spec(nseq, tq, D_MODEL), _seq_spec(nseq, 6, D_MODEL), _seq_spec(nseq, tq, D_MODEL)]
        + [_layer_spec(a, layer) for a in small]
        + [pl.BlockSpec((None, hist_a, nseq, CONV_W), lambda i: (layer, 0, i, 0)),
           pl.BlockSpec((None, nseq, SC_CONV_W - 1, CONV_W), lambda i: (layer, i, 0, 0))]
        + [_layer_spec(a, layer) for a in weights])
    out_shape = (
        jax.ShapeDtypeStruct((nbatch, tq, D_MODEL), F32),
        jax.ShapeDtypeStruct((hist_a, nbatch, CONV_W), F32),
        jax.ShapeDtypeStruct((nbatch, SC_CONV_W - 1, CONV_W), F32),
        jax.ShapeDtypeStruct((nbatch, tq, CONV_W), F32),
    )
    out_specs = (_seq_spec(nseq, tq, D_MODEL),
                 pl.BlockSpec((hist_a, nseq, CONV_W), lambda i: (0, i, 0)),
                 _seq_spec(nseq, SC_CONV_W - 1, CONV_W), _seq_spec(nseq, tq, CONV_W))
    return pl.pallas_call(
        _sample_rest_kernel,
        out_shape=out_shape,
        grid=(nbatch // nseq,),
        in_specs=in_specs,
        out_specs=out_specs,
        scratch_shapes=[
            pltpu.VMEM((nseq, tq, CONV_W), F32),
            pltpu.VMEM((nseq, tq, CONV_W), F32),
            pltpu.VMEM((nseq, B_PAD + tq, CONV_W), F32),
        ],
        compiler_params=pltpu.CompilerParams(
            dimension_semantics=("arbitrary",),
            vmem_limit_bytes=VMEM_BYTES - 8 * 1024 * 1024),
        name="sample_rest",
    )(x, mod, attn_out, *small, state_a, state_b, *weights)


def _mlp_kernel(x_ref, mod_ref, g_ref, w_up_ref, w_down_ref, o_ref):
    x = x_ref[...]
    if x.ndim == 3:
        shift2, scale2, gate2 = mod_ref[:, 3:4, :], mod_ref[:, 4:5, :], mod_ref[:, 5:6, :]
    else:
        shift2, scale2, gate2 = mod_ref[3:4, :], mod_ref[4:5, :], mod_ref[5:6, :]
    h2 = _rms_mod(x, g_ref[...], scale2, shift2)
    up = _dot(h2.reshape(-1, D_MODEL).astype(BF16), w_up_ref[...])
    act = jnp.square(jnp.maximum(up, 0.0)).astype(BF16)
    y = _dot(act, w_down_ref[...]).reshape(x.shape)
    o_ref[...] = x + gate2 * y


def _mlp(x, mod, lw, layer, per_sequence):
    if per_sequence:
        nbatch, tq, _ = x.shape
        nseq = MLP_TILE // tq
        grid = (nbatch // nseq,)
        x_spec = pl.BlockSpec((nseq, tq, D_MODEL), lambda i: (i, 0, 0))
        mod_spec = pl.BlockSpec((nseq, 6, D_MODEL), lambda i: (i, 0, 0))
    else:
        bsz, seq, _ = x.shape
        grid = (bsz, seq // MLP_TILE)
        x_spec = pl.BlockSpec((None, MLP_TILE, D_MODEL), lambda b, j: (b, j, 0))
        mod_spec = pl.BlockSpec((None, 6, D_MODEL), lambda b, j: (b, 0, 0))
    return pl.pallas_call(
        _mlp_kernel,
        out_shape=jax.ShapeDtypeStruct(x.shape, F32),
        grid=grid,
        in_specs=[x_spec, mod_spec, _layer_spec(lw["n2g"], layer),
                  _layer_spec(lw["w_up"], layer), _layer_spec(lw["w_down"], layer)],
        out_specs=x_spec,
        compiler_params=pltpu.CompilerParams(
            dimension_semantics=("arbitrary",) * len(grid),
            vmem_limit_bytes=VMEM_BYTES - 16 * 1024 * 1024),
        name="mlp_sample" if per_sequence else "mlp_prompt",
    )(x, mod, lw["n2g"], lw["w_up"], lw["w_down"])


def _rope_tables(pos):
    half = ROT_DIM // 2
    inv_freq = jnp.power(ROPE_THETA, -jnp.arange(half, dtype=F32) * 2.0 / ROT_DIM)
    ang = pos.astype(F32)[:, None] * inv_freq[None, :]
    cos, sin = jnp.cos(ang), jnp.sin(ang)
    lane = jnp.arange(LANES) % HEAD_DIM
    f = lane % half
    cos_t = jnp.where(lane < ROT_DIM, cos[:, f], 1.0)
    sin_a = jnp.where(lane < half, -sin[:, f], 0.0)
    sin_b = jnp.where((lane >= half) & (lane < ROT_DIM), sin[:, f], 0.0)
    return jnp.stack([cos_t, sin_a, sin_b]).astype(F32)


def _prepare_weights(p):
    depth = p["w_in"].shape[0]
    w_in = p["w_in"]
    wq = w_in[:, :, :D_MODEL].reshape(depth, D_MODEL, N_KV_HEADS, Q_REP, HEAD_DIM)
    wq = wq.transpose(0, 1, 3, 2, 4).reshape(depth, D_MODEL, D_MODEL).astype(BF16)
    w_ao = p["w_attn_o"].reshape(depth, N_KV_HEADS, Q_REP, HEAD_DIM, D_MODEL)
    w_ao = w_ao.transpose(0, 2, 1, 3, 4).reshape(depth, D_MODEL, D_MODEL).astype(BF16)
    ws = p["w_spatial"]
    n_pairs = GMLP_GROUPS // 2
    wpair = ws.reshape(depth, n_pairs, 2, GMLP_CHUNK, GMLP_CHUNK).transpose(0, 1, 3, 2, 4)
    wpair = wpair.reshape(depth, n_pairs, GMLP_CHUNK, 2 * GMLP_CHUNK)
    bsp = jnp.repeat(p["b_spatial"].transpose(0, 2, 1), GMLP_GROUP_DIM, axis=2)
    wc8 = jnp.repeat(ws[:, :, :DEC_SEQ, :DEC_SEQ].transpose(0, 3, 2, 1),
                     GMLP_GROUP_DIM, axis=3)
    lane_head = jnp.arange(LANES) // HEAD_DIM
    segm = jnp.where(lane_head[:, None] == lane_head[None, :], 1.0 / HEAD_DIM, 0.0).astype(BF16)
    caw = jnp.concatenate([p["conv_a_w"], jnp.zeros((depth, 1, CONV_W), F32)], axis=1)
    v512 = jnp.concatenate([p["conv_a_b"][:, None], p["ln_a_g"][:, None], p["ln_a_b"][:, None],
                            p["ln_c_g"][:, None], p["ln_c_b"][:, None], p["conv_b_w"]], axis=1)
    reps = LANES // HEAD_DIM
    return {
        "wq": wq, "w_in": w_in.astype(BF16), "w_ao": w_ao,
        "w_aout": p["w_a_out"].astype(BF16), "w_bout": p["w_b_out"].astype(BF16),
        "w_cout": p["w_c_out"].astype(BF16), "w_o": p["w_o"].astype(BF16),
        "w_up": p["w_up"].astype(BF16), "w_down": p["w_down"].astype(BF16),
        "n1g": p["norm1_g"][:, None], "n2g": p["norm2_g"][:, None],
        "qkg": jnp.stack([jnp.tile(p["q_norm_g"], (1, reps)),
                          jnp.tile(p["k_norm_g"], (1, reps))], axis=1),
        "caw": caw, "v512": v512, "wpair": wpair, "bsp": bsp, "wc8": wc8,
        "bsp8": bsp[:, :DEC_SEQ],
        "segm": jnp.broadcast_to(segm, (depth, LANES, LANES)), "sinks": p["attn_sinks"],
    }


def kernel(x_prompt, x_sample, c_prompt, c_sample, state_conv_a, state_conv_b, cache_k_win, cache_v_win, w_ada, b_ada, norm1_g, norm2_g, w_in, q_norm_g, k_norm_g, attn_sinks, w_attn_o, conv_a_w, conv_a_b, ln_a_g, ln_a_b, w_a_out, conv_b_w, w_b_out, ln_c_g, ln_c_b, w_spatial, b_spatial, w_c_out, w_o, w_up, w_down):
    params = dict(w_in=w_in, w_attn_o=w_attn_o, w_spatial=w_spatial, b_spatial=b_spatial,
                  conv_a_w=conv_a_w, conv_a_b=conv_a_b, ln_a_g=ln_a_g, ln_a_b=ln_a_b,
                  ln_c_g=ln_c_g, ln_c_b=ln_c_b, conv_b_w=conv_b_w, w_a_out=w_a_out,
                  w_b_out=w_b_out, w_c_out=w_c_out, w_o=w_o, w_up=w_up, w_down=w_down,
                  norm1_g=norm1_g, norm2_g=norm2_g, q_norm_g=q_norm_g, k_norm_g=k_norm_g,
                  attn_sinks=attn_sinks)
    depth = w_in.shape[0]
    bp, seq, _ = x_prompt.shape
    bs, tq, _ = x_sample.shape

    c_all = jnp.concatenate([c_prompt, c_sample,
                             jnp.zeros((MOD_ROWS - bp - bs, D_MODEL), F32)], axis=0)
    mod_all = _modulation(c_all, w_ada, b_ada)
    rope_p = _rope_tables(jnp.arange(seq))
    rope_s = jnp.tile(_rope_tables(PAST_LEN + jnp.arange(tq)), (1, SAMPLE_ATTN_SEQS, 1))
    state_a_tm = state_conv_a.transpose(0, 2, 1, 3)
    cache_k = cache_k_win.reshape(depth, bs, WINDOW, KV_DIM)
    cache_v = cache_v_win.reshape(depth, bs, WINDOW, KV_DIM)

    xp, xs = x_prompt, x_sample
    outs = [[] for _ in range(9)]
    lw = _prepare_weights(params)
    for l in range(depth):
        mod_p = mod_all[l, :bp].reshape(bp, 6, D_MODEL)
        mod_s = mod_all[l, bp:bp + bs].reshape(bs, 6, D_MODEL)
        xp, pa, pb, pk, pv = _mixer_prompt(xp, mod_p, rope_p, lw, l)
        xp = _mlp(xp, mod_p, lw, l, per_sequence=False)
        s_attn, sk, sv = _sample_attention(xs, mod_s, rope_s, cache_k, cache_v, lw, l)
        xs, sa, sb, sg = _sample_rest(xs, mod_s, s_attn, state_a_tm, state_conv_b, lw, l)
        sa = sa.transpose(1, 0, 2)
        xs = _mlp(xs, mod_s, lw, l, per_sequence=True)
        kv_shape_p = (bp, WINDOW, N_KV_HEADS, HEAD_DIM)
        kv_shape_s = (bs, WINDOW, N_KV_HEADS, HEAD_DIM)
        for lst, val in zip(outs, (pa, sa, pb, sb, pk.reshape(kv_shape_p), sk.reshape(kv_shape_s),
                                   pv.reshape(kv_shape_p), sv.reshape(kv_shape_s), sg)):
            lst.append(val)
    return (xp, xs) + tuple(jnp.stack(o) for o in outs)
```

```python
import functools

import jax
import jax.numpy as jnp
from jax import lax
from jax.experimental import pallas as pl
from jax.experimental.pallas import tpu as pltpu

F32 = jnp.float32
BF16 = jnp.bfloat16

D_MODEL = 1024
HEAD_DIM = 64
N_HEADS = 16
N_KV_HEADS = 4
Q_REP = 4
KV_DIM = N_KV_HEADS * HEAD_DIM
ROT_DIM = 16
ROPE_THETA = 500000.0
WINDOW = 128
ATTN_BLOCK = 128
KEY_SPAN = 2 * ATTN_BLOCK
CONV_W = 512
CONF_CONV_W = 31
SC_CONV_W = 3
GMLP_CHUNK = 128
GMLP_GROUPS = 8
GMLP_GROUP_DIM = 64
D_FF = 4 * D_MODEL
EPS = 1e-6
PAST_LEN = 16384
DEC_SEQ = 8
NEG = -1e30
LOG2_E = 1.4426950408889634

K0 = D_MODEL
V0 = K0 + KV_DIM
A0 = V0 + KV_DIM
B0 = A0 + 2 * CONV_W
C0 = B0 + 3 * CONV_W
G0 = C0 + 2 * CONV_W

LANES = 128
SUBLANES = 8
VMEM_BYTES = 64 * 1024 * 1024

PROMPT_TILE = 256
MLP_TILE = 512
SAMPLE_SEQS = 32
SAMPLE_ATTN_SEQS = 16
CONV_ROWS = 64
A_PAD = 32
B_PAD = 8
MOD_ROWS = 136


def _dot(a, b):
    return jnp.dot(a, b, preferred_element_type=F32)


def _rms_mod(x, g, scale, shift):
    ms = jnp.mean(x * x, axis=-1, keepdims=True)
    y = x * lax.rsqrt(ms + EPS)
    return (y * g) * (1.0 + scale) + shift


def _layer_norm(x, g, b):
    mu = jnp.mean(x, axis=-1, keepdims=True)
    xc = x - mu
    var = jnp.mean(xc * xc, axis=-1, keepdims=True)
    return xc * lax.rsqrt(var + EPS) * g + b


def _head_norm_rope(xc, gain, seg_mean, cos_t, sin_a, sin_b):
    ms = _dot((xc * xc).astype(BF16), seg_mean)
    y = xc * lax.rsqrt(ms + EPS) * gain
    return (y * cos_t + pltpu.roll(y, LANES - ROT_DIM // 2, 1) * sin_a
            + pltpu.roll(y, ROT_DIM // 2, 1) * sin_b)


def _qkv_heads(q_raw, kv_raw, qkg_ref, seg_mean, cos_t, sin_a, sin_b):
    def chunk(src, c, gain):
        return _head_norm_rope(src[:, c * LANES:(c + 1) * LANES], gain, seg_mean,
                               cos_t, sin_a, sin_b)
    q = jnp.concatenate([chunk(q_raw, c, qkg_ref[0:1, :]) for c in range(D_MODEL // LANES)],
                        axis=1)
    k = jnp.concatenate([chunk(kv_raw, c, qkg_ref[1:2, :])
                         for c in range(KV_DIM // LANES)], axis=1)
    return q, k, kv_raw[:, KV_DIM:2 * KV_DIM]


def _kv_head_masks(scale):
    lane = lax.broadcasted_iota(jnp.int32, (1, KV_DIM), 1)
    return [jnp.where((lane >= g * HEAD_DIM) & (lane < (g + 1) * HEAD_DIM), scale, 0.0).astype(F32)
            for g in range(N_KV_HEADS)]


def _softmax_with_sink(s, sink):
    m = jnp.maximum(jnp.max(s, axis=-1, keepdims=True), sink)
    e = jnp.exp(s - m)
    den = jnp.sum(e, axis=-1, keepdims=True) + jnp.exp(sink - m)
    return e * (1.0 / den)


def _softmax2_with_sink(s2, sink2):
    m = jnp.maximum(jnp.max(s2, axis=-1, keepdims=True), sink2)
    e = jnp.exp2(s2 - m)
    den = jnp.sum(e, axis=-1, keepdims=True) + jnp.exp2(sink2 - m)
    return e * (1.0 / den)


def _causal_conv_rows(hist_ref, row0, rows, w_ref, n_taps, base, bias):
    out = jnp.broadcast_to(bias, (rows, hist_ref.shape[1]))
    for phase in range(SUBLANES):
        span = rows if phase == 0 else rows + SUBLANES
        part = None
        for tap in range(n_taps):
            offset = base + tap
            if offset % SUBLANES != phase:
                continue
            lo = row0 + offset - phase
            term = w_ref[tap:tap + 1, :] * hist_ref[lo:lo + span, :]
            part = term if part is None else part + term
        if part is not None:
            out = out + part[phase:phase + rows]
    return out


def _mod_kernel(c_ref, w_ref, b_ref, o_ref):
    c = c_ref[...]
    s = (c * jax.nn.sigmoid(c)).astype(BF16)
    o_ref[...] = _dot(s, w_ref[...].astype(BF16)) + b_ref[...]


def _modulation(c_all, w_ada, b_ada):
    depth = w_ada.shape[0]
    n_chunks = w_ada.shape[2] // D_MODEL
    return pl.pallas_call(
        _mod_kernel,
        out_shape=jax.ShapeDtypeStruct((depth, MOD_ROWS, n_chunks * D_MODEL), F32),
        grid=(depth, n_chunks),
        in_specs=[
            pl.BlockSpec((MOD_ROWS, D_MODEL), lambda l, j: (0, 0)),
            pl.BlockSpec((None, D_MODEL, D_MODEL), lambda l, j: (l, 0, j)),
            pl.BlockSpec((None, 1, D_MODEL), lambda l, j: (l, 0, j)),
        ],
        out_specs=pl.BlockSpec((None, MOD_ROWS, D_MODEL), lambda l, j: (l, 0, j)),
        compiler_params=pltpu.CompilerParams(
            dimension_semantics=("arbitrary", "arbitrary"),
            vmem_limit_bytes=32 * 1024 * 1024),
        name="adaln_modulation",
    )(c_all, w_ada, b_ada.reshape(depth, 1, -1))


def _per_head_copies(win, head_masks):
    return jnp.concatenate([jnp.where(m, win, jnp.zeros_like(win)) for m in head_masks], axis=0)


def _block_scores(q_blk, k_heads):
    qs = (q_blk * (HEAD_DIM ** -0.5 * LOG2_E)).astype(BF16)
    q_rows = jnp.concatenate([qs[:, r * KV_DIM:(r + 1) * KV_DIM] for r in range(Q_REP)], axis=0)
    return lax.dot_general(q_rows, k_heads, (((1,), (1,)), ((), ())), preferred_element_type=F32)


def _block_probs(s, bias, sink_ref):
    tq, keys = bias.shape
    rows = []
    for r in range(Q_REP):
        cols = []
        for g in range(N_KV_HEADS):
            sp = s[r * tq:(r + 1) * tq, g * keys:(g + 1) * keys] + bias
            sink2 = sink_ref[g * Q_REP + r] * LOG2_E
            cols.append(_softmax2_with_sink(sp, sink2).astype(BF16))
        rows.append(jnp.concatenate(cols, axis=1))
    return jnp.concatenate(rows, axis=0)


def _block_output(p, v_heads):
    tq = p.shape[0] // Q_REP
    o = _dot(p, v_heads)
    return jnp.concatenate([o[r * tq:(r + 1) * tq] for r in range(Q_REP)], axis=1)


def _mixer_prompt_kernel(x_ref, *refs):
    n_shared = 16
    mod_ref, shared, per_seq = refs[0], refs[1:1 + n_shared], refs[1 + n_shared:]
    waiting = [_mixer_prompt_sequence(x_ref.at[b], mod_ref.at[b], *shared,
                                      *[r.at[b] for r in per_seq])
               for b in range(x_ref.shape[0])]
    running = []
    while waiting or running:
        if waiting:
            running.append(waiting.pop(0))
        for gen in list(running):
            if next(gen, "done") == "done":
                running.remove(gen)


def _mixer_prompt_sequence(x_ref, mod_ref, rope_ref, sink_ref, n1g_ref, qkg_ref, caw_ref, v512_ref,
                           wpair_ref, bsp_ref, segm_ref,
                           wq_ref, w_in_ref, w_ao_ref, w_aout_ref, w_bout_ref, w_cout_ref, w_o_ref,
                           x1_ref, ca_ref, cb_ref, ko_ref, vo_ref,
                           kprev_s, vprev_s, aprev_s, bprev_s, acat_s, bcat_s, bias_s):
    tile = x_ref.shape[0]
    j = pl.program_id(0)
    last = j == pl.num_programs(0) - 1
    rd = j & 1
    wr = 1 - rd

    @pl.when(j == 0)
    def _():
        kprev_s[0] = jnp.zeros((ATTN_BLOCK, KV_DIM), BF16)
        vprev_s[0] = jnp.zeros((ATTN_BLOCK, KV_DIM), BF16)
        aprev_s[0] = jnp.zeros((A_PAD, CONV_W), F32)
        bprev_s[0] = jnp.zeros((B_PAD, CONV_W), F32)

    x = x_ref[...]
    shift1, scale1, gate1 = mod_ref[0:1, :], mod_ref[1:2, :], mod_ref[2:3, :]
    h = _rms_mod(x, n1g_ref[...], scale1, shift1).astype(BF16)

    def gate_logits(branch):
        lo = G0 + branch * D_MODEL
        return _dot(h, w_in_ref[:, lo:lo + D_MODEL])

    yield
    q_raw = _dot(h, wq_ref[...])
    kv_raw = _dot(h, w_in_ref[:, K0:A0])
    a_in = _dot(h, w_in_ref[:, A0:A0 + 2 * CONV_W])
    yield
    q, k, v = _qkv_heads(q_raw, kv_raw, qkg_ref, segm_ref[...],
                         rope_ref[0], rope_ref[1], rope_ref[2])
    kcat = jnp.concatenate([kprev_s[rd], k.astype(BF16)], axis=0)
    vcat = jnp.concatenate([vprev_s[rd], v.astype(BF16)], axis=0)

    @pl.when(last)
    def _():
        ko_ref[...] = k[tile - WINDOW:tile]
        vo_ref[...] = v[tile - WINDOW:tile]

    tt = lax.broadcasted_iota(jnp.int32, (ATTN_BLOCK, KEY_SPAN), 0)
    jj = lax.broadcasted_iota(jnp.int32, (ATTN_BLOCK, KEY_SPAN), 1)
    dist = jj - tt
    bias_rest = jnp.where((dist >= 0) & (dist <= WINDOW), 0.0, NEG).astype(F32)

    @pl.when(j == 0)
    def _():
        bias_s[...] = jnp.where(jj < ATTN_BLOCK, NEG, bias_rest)

    @pl.when(j == 1)
    def _():
        bias_s[...] = bias_rest

    bias_first = bias_s[...]
    lane = lax.broadcasted_iota(jnp.int32, (1, KV_DIM), 1)
    head_masks = [(lane >= g * HEAD_DIM) & (lane < (g + 1) * HEAD_DIM) for g in range(N_KV_HEADS)]
    kprev_s[wr] = kcat[tile:tile + ATTN_BLOCK]
    vprev_s[wr] = vcat[tile:tile + ATTN_BLOCK]
    n_blocks = tile // ATTN_BLOCK

    def scores(i):
        lo = i * ATTN_BLOCK
        return _block_scores(q[lo:lo + ATTN_BLOCK],
                             _per_head_copies(kcat[lo:lo + KEY_SPAN], head_masks))

    glu = a_in[:, :CONV_W] * jax.nn.sigmoid(a_in[:, CONV_W:])
    acat_s[0:A_PAD, :] = aprev_s[rd]
    acat_s[A_PAD:A_PAD + tile, :] = glu
    aprev_s[wr] = glu[tile - A_PAD:tile]
    base = A_PAD - (CONF_CONV_W - 1)

    @pl.when(last)
    def _():
        ca_ref[...] = acat_s[base + tile:A_PAD + tile, :]

    n_chunks = tile // CONV_ROWS
    conv_rows = []

    def conv_chunks(lo, hi):
        for c in range(lo, hi):
            conv_rows.append(_causal_conv_rows(acat_s, c * CONV_ROWS, CONV_ROWS, caw_ref,
                                               CONF_CONV_W, base, v512_ref[0:1, :]))

    fillers = [lambda: _dot(h, w_in_ref[:, B0:B0 + 3 * CONV_W]),
               lambda: _dot(h, w_in_ref[:, C0:C0 + 2 * CONV_W])]
    filled = []
    blocks = []
    p = None
    yield
    for i in range(n_blocks + 1):
        if i > 0:
            lo = (i - 1) * ATTN_BLOCK
            blocks.append(_block_output(p, _per_head_copies(vcat[lo:lo + KEY_SPAN], head_masks)))
        if i < n_blocks:
            s_blk = scores(i)
        if i < len(fillers):
            filled.append(fillers[i]())
        if i == n_blocks:
            for f in fillers[len(filled):]:
                filled.append(f())
            g0 = gate_logits(0)
            g1 = gate_logits(1)
        yield
        if i < n_blocks:
            p = _block_probs(s_blk, bias_first if i == 0 else bias_rest, sink_ref)
            conv_chunks(i, i + 1)
            yield
    b_in, c_in = filled
    conv_chunks(n_blocks, n_chunks)
    attn = jnp.concatenate(blocks, axis=0).astype(BF16)
    a_conv = jnp.concatenate(conv_rows, axis=0)
    a_act = jax.nn.silu(_layer_norm(a_conv, v512_ref[1:2, :], v512_ref[2:3, :])).astype(BF16)
    yield
    attn_out = _dot(attn, w_ao_ref[...])
    a_out = _dot(a_act, w_aout_ref[...])
    g2 = gate_logits(2)
    yield

    ch = b_in[:, CONV_W:2 * CONV_W] * b_in[:, 2 * CONV_W:]
    bcat_s[0:B_PAD, :] = bprev_s[rd]
    bcat_s[B_PAD:B_PAD + tile, :] = ch
    bprev_s[wr] = ch[tile - B_PAD:tile]
    base_b = B_PAD - (SC_CONV_W - 1)
    sc = None
    for t in range(SC_CONV_W):
        term = v512_ref[5 + t:6 + t, :] * bcat_s[base_b + t:base_b + t + tile, :]
        sc = term if sc is None else sc + term

    @pl.when(last)
    def _():
        cb_ref[...] = bcat_s[base_b + tile:B_PAD + tile, :]

    merged = jax.nn.sigmoid(g0) * attn_out
    merged = merged + jax.nn.sigmoid(g1) * a_out

    zc = jax.nn.gelu(c_in)
    u = zc[:, :CONV_W]
    vv = _layer_norm(zc[:, CONV_W:], v512_ref[3:4, :], v512_ref[4:5, :])
    row = lax.broadcasted_iota(jnp.int32, (GMLP_CHUNK, 2 * GMLP_CHUNK), 0)
    col = lax.broadcasted_iota(jnp.int32, (GMLP_CHUNK, 2 * GMLP_CHUNK), 1)
    causal = (col & (GMLP_CHUNK - 1)) <= row
    n_pairs = GMLP_GROUPS // 2
    wpair = [jnp.where(causal, wpair_ref[p], 0.0).astype(BF16) for p in range(n_pairs)]
    low = lax.broadcasted_iota(jnp.int32, (1, LANES), 1) < GMLP_GROUP_DIM
    rhs_blocks = []
    for c in range(tile // GMLP_CHUNK):
        vch = vv[c * GMLP_CHUNK:(c + 1) * GMLP_CHUNK]
        for pair in range(n_pairs):
            blk = vch[:, pair * LANES:(pair + 1) * LANES]
            rhs = jnp.concatenate([jnp.where(low, blk, 0.0), jnp.where(low, 0.0, blk)], axis=0)
            rhs_blocks.append(rhs.astype(BF16))
    b_act = (b_in[:, :CONV_W] * sc).astype(BF16)
    yield
    b_out = _dot(b_act, w_bout_ref[...])
    zs = [_dot(wpair[i % n_pairs], rhs) for i, rhs in enumerate(rhs_blocks)]
    g3 = gate_logits(3)
    yield
    gated = []
    for c in range(tile // GMLP_CHUNK):
        z = jnp.concatenate(zs[c * n_pairs:(c + 1) * n_pairs], axis=1) + bsp_ref[...]
        gated.append(u[c * GMLP_CHUNK:(c + 1) * GMLP_CHUNK] * z)
    gated = jnp.concatenate(gated, axis=0).astype(BF16)
    merged = merged + jax.nn.sigmoid(g2) * b_out
    yield
    c_out = _dot(gated, w_cout_ref[...])
    yield
    merged = (merged + jax.nn.sigmoid(g3) * c_out).astype(BF16)
    yield
    x1_ref[...] = x + gate1 * _dot(merged, w_o_ref[...])


def _const_spec(shape):
    zeros = (0,) * len(shape)
    return pl.BlockSpec(shape, lambda *_: zeros, pipeline_mode=pl.Buffered(1))


def _layer_spec(stacked, layer, cols=None):
    if cols is None:
        block, col_block = stacked.shape[1:], 0
    else:
        start, width = cols
        assert start % width == 0
        block, col_block = stacked.shape[1:-1] + (width,), start // width
    index = (layer,) + (0,) * (len(block) - 1) + (col_block,)
    return pl.BlockSpec((None,) + block, lambda *_: index, pipeline_mode=pl.Buffered(1))


def _mixer_prompt(x, mod, rope, lw, layer):
    bsz, seq, _ = x.shape
    tile = PROMPT_TILE
    small = [lw["n1g"], lw["qkg"], lw["caw"], lw["v512"], lw["wpair"], lw["bsp"], lw["segm"]]
    weights = [lw["wq"], lw["w_in"], lw["w_ao"], lw["w_aout"], lw["w_bout"], lw["w_cout"],
               lw["w_o"]]
    def whole(r, c):
        return pl.BlockSpec((bsz, r, c), lambda j: (0, 0, 0))

    in_specs = (
        [pl.BlockSpec((bsz, tile, D_MODEL), lambda j: (0, j, 0)),
         whole(6, D_MODEL),
         pl.BlockSpec((3, tile, LANES), lambda j: (0, j, 0)),
         pl.BlockSpec(memory_space=pltpu.SMEM)]
        + [_layer_spec(a, layer) for a in small]
        + [_layer_spec(a, layer) for a in weights])
    out_shape = (
        jax.ShapeDtypeStruct((bsz, seq, D_MODEL), F32),
        jax.ShapeDtypeStruct((bsz, CONF_CONV_W - 1, CONV_W), F32),
        jax.ShapeDtypeStruct((bsz, SC_CONV_W - 1, CONV_W), F32),
        jax.ShapeDtypeStruct((bsz, WINDOW, KV_DIM), F32),
        jax.ShapeDtypeStruct((bsz, WINDOW, KV_DIM), F32),
    )
    out_specs = (
        pl.BlockSpec((bsz, tile, D_MODEL), lambda j: (0, j, 0)),
        whole(CONF_CONV_W - 1, CONV_W), whole(SC_CONV_W - 1, CONV_W),
        whole(WINDOW, KV_DIM), whole(WINDOW, KV_DIM),
    )
    return pl.pallas_call(
        _mixer_prompt_kernel,
        out_shape=out_shape,
        grid=(seq // tile,),
        in_specs=in_specs,
        out_specs=out_specs,
        scratch_shapes=[
            pltpu.VMEM((bsz, 2, ATTN_BLOCK, KV_DIM), BF16),
            pltpu.VMEM((bsz, 2, ATTN_BLOCK, KV_DIM), BF16),
            pltpu.VMEM((bsz, 2, A_PAD, CONV_W), F32),
            pltpu.VMEM((bsz, 2, B_PAD, CONV_W), F32),
            pltpu.VMEM((bsz, A_PAD + tile, CONV_W), F32),
            pltpu.VMEM((bsz, B_PAD + tile, CONV_W), F32),
            pltpu.VMEM((bsz, ATTN_BLOCK, KEY_SPAN), F32),
        ],
        compiler_params=pltpu.CompilerParams(
            dimension_semantics=("arbitrary",),
            vmem_limit_bytes=VMEM_BYTES - 5 * 1024 * 1024),
        name="mixer_prompt",
    )(x, mod, rope, lw["sinks"][layer], *small, *weights)


SAMPLE_NEW_ROWS = 16


def _sample_attn_kernel(x_ref, mod_ref, rope_ref, sink_ref, n1g_ref, qkg_ref, segm_ref,
                        ck_ref, cv_ref, wq_ref, wkv_ref, w_ao_ref,
                        ao_ref, nk_ref, nv_ref, kcat_s, vcat_s):
    nseq, tq, _ = x_ref.shape
    rows = nseq * tq
    cache_lo = SAMPLE_NEW_ROWS
    cache_hi = cache_lo + WINDOW

    x = x_ref[...]
    shift1, scale1 = mod_ref[:, 0:1, :], mod_ref[:, 1:2, :]
    h = _rms_mod(x, n1g_ref[...], scale1, shift1).reshape(rows, D_MODEL).astype(BF16)

    q, k, v = _qkv_heads(_dot(h, wq_ref[...]), _dot(h, wkv_ref[...]), qkg_ref, segm_ref[...],
                         rope_ref[0], rope_ref[1], rope_ref[2])
    k3 = k.reshape(nseq, tq, KV_DIM)
    v3 = v.reshape(nseq, tq, KV_DIM)
    keep = WINDOW - tq
    nk_ref[:, 0:keep, :] = ck_ref[:, tq:WINDOW, :]
    nk_ref[:, keep:WINDOW, :] = k3
    nv_ref[:, 0:keep, :] = cv_ref[:, tq:WINDOW, :]
    nv_ref[:, keep:WINDOW, :] = v3
    pad = jnp.zeros((nseq, SAMPLE_NEW_ROWS - tq, KV_DIM), F32)
    kcat_s[:, 0:cache_lo, :] = jnp.concatenate([k3, pad], axis=1).astype(BF16)
    vcat_s[:, 0:cache_lo, :] = jnp.concatenate([v3, pad], axis=1).astype(BF16)
    kcat_s[:, cache_lo:cache_hi, :] = ck_ref[...].astype(BF16)
    vcat_s[:, cache_lo:cache_hi, :] = cv_ref[...].astype(BF16)
    tail = jnp.zeros((nseq, KEY_SPAN - cache_hi, KV_DIM), BF16)
    kcat_s[:, cache_hi:KEY_SPAN, :] = tail
    vcat_s[:, cache_hi:KEY_SPAN, :] = tail

    n_q = N_HEADS * tq
    tt = lax.broadcasted_iota(jnp.int32, (n_q, KEY_SPAN), 0) & (tq - 1)
    ii = lax.broadcasted_iota(jnp.int32, (n_q, KEY_SPAN), 1)
    visible = ((ii < tq) & (ii <= tt)) | ((ii >= cache_lo + tt) & (ii < cache_hi))
    bias = jnp.where(visible, 0.0, NEG).astype(F32)
    cm_scaled = _kv_head_masks(HEAD_DIM ** -0.5)
    cm_plain = _kv_head_masks(1.0)
    q3 = q.reshape(nseq, tq, D_MODEL)
    qbd = jnp.concatenate([q3[:, :, r * KV_DIM:(r + 1) * KV_DIM] * cm_scaled[g]
                           for r in range(Q_REP) for g in range(N_KV_HEADS)], axis=1).astype(BF16)
    s = jnp.einsum("bqc,bkc->bqk", qbd, kcat_s[...], preferred_element_type=F32) + bias
    sink = jnp.concatenate([jnp.full((tq, 1), sink_ref[g * Q_REP + r], F32)
                            for r in range(Q_REP) for g in range(N_KV_HEADS)], axis=0)
    p = _softmax_with_sink(s, sink).astype(BF16)
    o = jnp.einsum("bqk,bkc->bqc", p, vcat_s[...], preferred_element_type=F32)
    slabs = []
    for r in range(Q_REP):
        acc = None
        for g in range(N_KV_HEADS):
            i = r * N_KV_HEADS + g
            part = o[:, i * tq:(i + 1) * tq, :] * cm_plain[g]
            acc = part if acc is None else acc + part
        slabs.append(acc)
    attn = jnp.concatenate(slabs, axis=2).reshape(rows, D_MODEL)
    ao_ref[...] = _dot(attn.astype(BF16), w_ao_ref[...]).reshape(nseq, tq, D_MODEL)


def _sample_rest_kernel(x_ref, mod_ref, ao_ref, n1g_ref, caw_ref, v512_ref, wc_ref, bsp_ref,
                        sta_ref, stb_ref, w_in_ref, w_aout_ref, w_bout_ref, w_cout_ref, w_o_ref,
                        x1_ref, na_ref, nb_ref, vv_ref, glu_s, aact_s, bcat_s):
    nseq, tq, _ = x_ref.shape
    rows = nseq * tq

    x = x_ref[...]
    shift1, scale1, gate1 = mod_ref[:, 0:1, :], mod_ref[:, 1:2, :], mod_ref[:, 2:3, :]
    h = _rms_mod(x, n1g_ref[...], scale1, shift1).reshape(rows, D_MODEL).astype(BF16)
    attn_out = ao_ref[...].reshape(rows, D_MODEL)
    merged = jax.nn.sigmoid(_dot(h, w_in_ref[:, G0:G0 + D_MODEL])) * attn_out

    a_in = _dot(h, w_in_ref[:, A0:A0 + 2 * CONV_W])
    glu = a_in[:, :CONV_W] * jax.nn.sigmoid(a_in[:, CONV_W:])
    hist_a = CONF_CONV_W - 1
    glu_s[...] = glu.reshape(nseq, tq, CONV_W)

    def conv_in(t):
        return sta_ref[t] if t < hist_a else glu_s[:, t - hist_a, :]

    for t in range(tq):
        acc = jnp.broadcast_to(v512_ref[0:1, :], (nseq, CONV_W))
        for tap in range(CONF_CONV_W):
            acc = acc + caw_ref[tap:tap + 1, :] * conv_in(t + tap)
        aact_s[:, t, :] = jax.nn.silu(_layer_norm(acc, v512_ref[1:2, :], v512_ref[2:3, :]))
    for t in range(hist_a):
        na_ref[t] = conv_in(t + tq)
    a_out = _dot(aact_s[...].reshape(rows, CONV_W).astype(BF16), w_aout_ref[...])
    merged = merged + jax.nn.sigmoid(_dot(h, w_in_ref[:, G0 + D_MODEL:G0 + 2 * D_MODEL])) * a_out

    b_in = _dot(h, w_in_ref[:, B0:B0 + 3 * CONV_W])
    hist_b = SC_CONV_W - 1
    base_b = B_PAD - hist_b
    bcat_s[:, base_b:B_PAD, :] = stb_ref[...]
    bcat_s[:, B_PAD:B_PAD + tq, :] = (b_in[:, CONV_W:2 * CONV_W]
                                      * b_in[:, 2 * CONV_W:]).reshape(nseq, tq, CONV_W)
    sc = None
    for t in range(SC_CONV_W):
        term = v512_ref[5 + t:6 + t, :] * bcat_s[:, base_b + t:base_b + t + tq, :]
        sc = term if sc is None else sc + term
    nb_ref[...] = bcat_s[:, base_b + tq:B_PAD + tq, :]
    b_out = _dot((b_in[:, :CONV_W] * sc.reshape(rows, CONV_W)).astype(BF16), w_bout_ref[...])
    merged = merged + jax.nn.sigmoid(_dot(h, w_in_ref[:, G0 + 2 * D_MODEL:G0 + 3 * D_MODEL])) * b_out

    zc = jax.nn.gelu(_dot(h, w_in_ref[:, C0:C0 + 2 * CONV_W]))
    u3 = zc[:, :CONV_W].reshape(nseq, tq, CONV_W)
    vv3 = _layer_norm(zc[:, CONV_W:], v512_ref[3:4, :], v512_ref[4:5, :]).reshape(nseq, tq, CONV_W)
    vv_ref[...] = vv3
    t_idx = lax.broadcasted_iota(jnp.int32, (tq, CONV_W), 0)
    z = jnp.broadcast_to(bsp_ref[...], (nseq, tq, CONV_W))
    for src in range(tq):
        coef = jnp.where(t_idx >= src, wc_ref[src], 0.0)
        z = z + coef * vv3[:, src:src + 1, :]
    c_out = _dot((u3 * z).reshape(rows, CONV_W).astype(BF16), w_cout_ref[...])
    merged = merged + jax.nn.sigmoid(_dot(h, w_in_ref[:, G0 + 3 * D_MODEL:G0 + 4 * D_MODEL])) * c_out

    y = _dot(merged.astype(BF16), w_o_ref[...]).reshape(nseq, tq, D_MODEL)
    x1_ref[...] = x + gate1 * y


def _seq_spec(nseq, r, c):
    return pl.BlockSpec((nseq, r, c), lambda i: (i, 0, 0))


def _sample_attention(x, mod, rope, cache_k, cache_v, lw, layer):
    nbatch, tq, _ = x.shape
    nseq = SAMPLE_ATTN_SEQS
    small = [lw["n1g"], lw["qkg"], lw["segm"]]

    def cache_spec():
        return pl.BlockSpec((None, nseq, WINDOW, KV_DIM), lambda i: (layer, i, 0, 0))

    in_specs = (
        [_seq_spec(nseq, tq, D_MODEL), _seq_spec(nseq, 6, D_MODEL),
         _const_spec(rope.shape), pl.BlockSpec(memory_space=pltpu.SMEM)]
        + [_layer_spec(a, layer) for a in small]
        + [cache_spec(), cache_spec(),
           _layer_spec(lw["wq"], layer), _layer_spec(lw["w_in"], layer, cols=(K0, 2 * KV_DIM)),
           _layer_spec(lw["w_ao"], layer)])
    out_shape = (
        jax.ShapeDtypeStruct((nbatch, tq, D_MODEL), F32),
        jax.ShapeDtypeStruct((nbatch, WINDOW, KV_DIM), F32),
        jax.ShapeDtypeStruct((nbatch, WINDOW, KV_DIM), F32),
    )
    out_specs = (_seq_spec(nseq, tq, D_MODEL), _seq_spec(nseq, WINDOW, KV_DIM),
                 _seq_spec(nseq, WINDOW, KV_DIM))
    return pl.pallas_call(
        _sample_attn_kernel,
        out_shape=out_shape,
        grid=(nbatch // nseq,),
        in_specs=in_specs,
        out_specs=out_specs,
        scratch_shapes=[
            pltpu.VMEM((nseq, KEY_SPAN, KV_DIM), BF16),
            pltpu.VMEM((nseq, KEY_SPAN, KV_DIM), BF16),
        ],
        compiler_params=pltpu.CompilerParams(
            dimension_semantics=("arbitrary",),
            vmem_limit_bytes=VMEM_BYTES - 16 * 1024 * 1024),
        name="sample_attention",
    )(x, mod, rope, lw["sinks"][layer], *small, cache_k, cache_v, lw["wq"], lw["w_in"], lw["w_ao"])


def _sample_rest(x, mod, attn_out, state_a, state_b, lw, layer):
    nbatch, tq, _ = x.shape
    nseq = SAMPLE_SEQS
    small = [lw["n1g"], lw["caw"], lw["v512"], lw["wc8"], lw["bsp8"]]
    weights = [lw["w_in"], lw["w_aout"], lw["w_bout"], lw["w_cout"], lw["w_o"]]
    hist_a = CONF_CONV_W - 1
    in_specs = (
        [_seq_spec(nseq, tq, D_MODEL), _seq_spec(nseq, 6, D_MODEL), _seq_spec(nseq, tq, D_MODEL)]
        + [_layer_spec(a, layer) for a in small]
        + [pl.BlockSpec((None, hist_a, nseq, CONV_W), lambda i: (layer, 0, i, 0)),
           pl.BlockSpec((None, nseq, SC_CONV_W - 1, CONV_W), lambda i: (layer, i, 0, 0))]
        + [_layer_spec(a, layer) for a in weights])
    out_shape = (
        jax.ShapeDtypeStruct((nbatch, tq, D_MODEL), F32),
        jax.ShapeDtypeStruct((hist_a, nbatch, CONV_W), F32),
        jax.ShapeDtypeStruct((nbatch, SC_CONV_W - 1, CONV_W), F32),
        jax.ShapeDtypeStruct((nbatch, tq, CONV_W), F32),
    )
    out_specs = (_seq_spec(nseq, tq, D_MODEL),
                 pl.BlockSpec((hist_a, nseq, CONV_W), lambda i: (0, i, 0)),
                 _seq_spec(nseq, SC_CONV_W - 1, CONV_W), _seq_spec(nseq, tq, CONV_W))
    return pl.pallas_call(
        _sample_rest_kernel,
        out_shape=out_shape,
        grid=(nbatch // nseq,),
        in_specs=in_specs,
        out_specs=out_specs,
        scratch_shapes=[
            pltpu.VMEM((nseq, tq, CONV_W), F32),
            pltpu.VMEM((nseq, tq, CONV_W), F32),
            pltpu.VMEM((nseq, B_PAD + tq, CONV_W), F32),
        ],
        compiler_params=pltpu.CompilerParams(
            dimension_semantics=("arbitrary",),
            vmem_limit_bytes=VMEM_BYTES - 8 * 1024 * 1024),
        name="sample_rest",
    )(x, mod, attn_out, *small, state_a, state_b, *weights)


def _mlp_kernel(x_ref, mod_ref, g_ref, w_up_ref, w_down_ref, o_ref):
    x = x_ref[...]
    if x.ndim == 3:
        shift2, scale2, gate2 = mod_ref[:, 3:4, :], mod_ref[:, 4:5, :], mod_ref[:, 5:6, :]
    else:
        shift2, scale2, gate2 = mod_ref[3:4, :], mod_ref[4:5, :], mod_ref[5:6, :]
    h2 = _rms_mod(x, g_ref[...], scale2, shift2)
    up = _dot(h2.reshape(-1, D_MODEL).astype(BF16), w_up_ref[...])
    act = jnp.square(jnp.maximum(up, 0.0)).astype(BF16)
    y = _dot(act, w_down_ref[...]).reshape(x.shape)
    o_ref[...] = x + gate2 * y


def _mlp(x, mod, lw, layer, per_sequence):
    if per_sequence:
        nbatch, tq, _ = x.shape
        nseq = MLP_TILE // tq
        grid = (nbatch // nseq,)
        x_spec = pl.BlockSpec((nseq, tq, D_MODEL), lambda i: (i, 0, 0))
        mod_spec = pl.BlockSpec((nseq, 6, D_MODEL), lambda i: (i, 0, 0))
    else:
        bsz, seq, _ = x.shape
        grid = (bsz, seq // MLP_TILE)
        x_spec = pl.BlockSpec((None, MLP_TILE, D_MODEL), lambda b, j: (b, j, 0))
        mod_spec = pl.BlockSpec((None, 6, D_MODEL), lambda b, j: (b, 0, 0))
    return pl.pallas_call(
        _mlp_kernel,
        out_shape=jax.ShapeDtypeStruct(x.shape, F32),
        grid=grid,
        in_specs=[x_spec, mod_spec, _layer_spec(lw["n2g"], layer),
                  _layer_spec(lw["w_up"], layer), _layer_spec(lw["w_down"], layer)],
        out_specs=x_spec,
        compiler_params=pltpu.CompilerParams(
            dimension_semantics=("arbitrary",) * len(grid),
            vmem_limit_bytes=VMEM_BYTES - 16 * 1024 * 1024),
        name="mlp_sample" if per_sequence else "mlp_prompt",
    )(x, mod, lw["n2g"], lw["w_up"], lw["w_down"])


def _rope_tables(pos):
    half = ROT_DIM // 2
    inv_freq = jnp.power(ROPE_THETA, -jnp.arange(half, dtype=F32) * 2.0 / ROT_DIM)
    ang = pos.astype(F32)[:, None] * inv_freq[None, :]
    cos, sin = jnp.cos(ang), jnp.sin(ang)
    lane = jnp.arange(LANES) % HEAD_DIM
    f = lane % half
    cos_t = jnp.where(lane < ROT_DIM, cos[:, f], 1.0)
    sin_a = jnp.where(lane < half, -sin[:, f], 0.0)
    sin_b = jnp.where((lane >= half) & (lane < ROT_DIM), sin[:, f], 0.0)
    return jnp.stack([cos_t, sin_a, sin_b]).astype(F32)


def _prepare_weights(p):
    depth = p["w_in"].shape[0]
    w_in = p["w_in"]
    wq = w_in[:, :, :D_MODEL].reshape(depth, D_MODEL, N_KV_HEADS, Q_REP, HEAD_DIM)
    wq = wq.transpose(0, 1, 3, 2, 4).reshape(depth, D_MODEL, D_MODEL).astype(BF16)
    w_ao = p["w_attn_o"].reshape(depth, N_KV_HEADS, Q_REP, HEAD_DIM, D_MODEL)
    w_ao = w_ao.transpose(0, 2, 1, 3, 4).reshape(depth, D_MODEL, D_MODEL).astype(BF16)
    ws = p["w_spatial"]
    n_pairs = GMLP_GROUPS // 2
    wpair = ws.reshape(depth, n_pairs, 2, GMLP_CHUNK, GMLP_CHUNK).transpose(0, 1, 3, 2, 4)
    wpair = wpair.reshape(depth, n_pairs, GMLP_CHUNK, 2 * GMLP_CHUNK)
    bsp = jnp.repeat(p["b_spatial"].transpose(0, 2, 1), GMLP_GROUP_DIM, axis=2)
    wc8 = jnp.repeat(ws[:, :, :DEC_SEQ, :DEC_SEQ].transpose(0, 3, 2, 1),
                     GMLP_GROUP_DIM, axis=3)
    lane_head = jnp.arange(LANES) // HEAD_DIM
    segm = jnp.where(lane_head[:, None] == lane_head[None, :], 1.0 / HEAD_DIM, 0.0).astype(BF16)
    caw = jnp.concatenate([p["conv_a_w"], jnp.zeros((depth, 1, CONV_W), F32)], axis=1)
    v512 = jnp.concatenate([p["conv_a_b"][:, None], p["ln_a_g"][:, None], p["ln_a_b"][:, None],
                            p["ln_c_g"][:, None], p["ln_c_b"][:, None], p["conv_b_w"]], axis=1)
    reps = LANES // HEAD_DIM
    return {
        "wq": wq, "w_in": w_in.astype(BF16), "w_ao": w_ao,
        "w_aout": p["w_a_out"].astype(BF16), "w_bout": p["w_b_out"].astype(BF16),
        "w_cout": p["w_c_out"].astype(BF16), "w_o": p["w_o"].astype(BF16),
        "w_up": p["w_up"].astype(BF16), "w_down": p["w_down"].astype(BF16),
        "n1g": p["norm1_g"][:, None], "n2g": p["norm2_g"][:, None],
        "qkg": jnp.stack([jnp.tile(p["q_norm_g"], (1, reps)),
                          jnp.tile(p["k_norm_g"], (1, reps))], axis=1),
        "caw": caw, "v512": v512, "wpair": wpair, "bsp": bsp, "wc8": wc8,
        "bsp8": bsp[:, :DEC_SEQ],
        "segm": jnp.broadcast_to(segm, (depth, LANES, LANES)), "sinks": p["attn_sinks"],
    }


def kernel(x_prompt, x_sample, c_prompt, c_sample, state_conv_a, state_conv_b, cache_k_win, cache_v_win, w_ada, b_ada, norm1_g, norm2_g, w_in, q_norm_g, k_norm_g, attn_sinks, w_attn_o, conv_a_w, conv_a_b, ln_a_g, ln_a_b, w_a_out, conv_b_w, w_b_out, ln_c_g, ln_c_b, w_spatial, b_spatial, w_c_out, w_o, w_up, w_down):
    params = dict(w_in=w_in, w_attn_o=w_attn_o, w_spatial=w_spatial, b_spatial=b_spatial,
                  conv_a_w=conv_a_w, conv_a_b=conv_a_b, ln_a_g=ln_a_g, ln_a_b=ln_a_b,
                  ln_c_g=ln_c_g, ln_c_b=ln_c_b, conv_b_w=conv_b_w, w_a_out=w_a_out,
                  w_b_out=w_b_out, w_c_out=w_c_out, w_o=w_o, w_up=w_up, w_down=w_down,
                  norm1_g=norm1_g, norm2_g=norm2_g, q_norm_g=q_norm_g, k_norm_g=k_norm_g,
                  attn_sinks=attn_sinks)
    depth = w_in.shape[0]
    bp, seq, _ = x_prompt.shape
    bs, tq, _ = x_sample.shape

    c_all = jnp.concatenate([c_prompt, c_sample,
                             jnp.zeros((MOD_ROWS - bp - bs, D_MODEL), F32)], axis=0)
    mod_all = _modulation(c_all, w_ada, b_ada)
    rope_p = _rope_tables(jnp.arange(seq))
    rope_s = jnp.tile(_rope_tables(PAST_LEN + jnp.arange(tq)), (1, SAMPLE_ATTN_SEQS, 1))
    state_a_tm = state_conv_a.transpose(0, 2, 1, 3)
    cache_k = cache_k_win.reshape(depth, bs, WINDOW, KV_DIM)
    cache_v = cache_v_win.reshape(depth, bs, WINDOW, KV_DIM)

    xp, xs = x_prompt, x_sample
    outs = [[] for _ in range(9)]
    lw = _prepare_weights(params)
    for l in range(depth):
        mod_p = mod_all[l, :bp].reshape(bp, 6, D_MODEL)
        mod_s = mod_all[l, bp:bp + bs].reshape(bs, 6, D_MODEL)
        xp, pa, pb, pk, pv = _mixer_prompt(xp, mod_p, rope_p, lw, l)
        xp = _mlp(xp, mod_p, lw, l, per_sequence=False)
        s_attn, sk, sv = _sample_attention(xs, mod_s, rope_s, cache_k, cache_v, lw, l)
        xs, sa, sb, sg = _sample_rest(xs, mod_s, s_attn, state_a_tm, state_conv_b, lw, l)
        sa = sa.transpose(1, 0, 2)
        xs = _mlp(xs, mod_s, lw, l, per_sequence=True)
        kv_shape_p = (bp, WINDOW, N_KV_HEADS, HEAD_DIM)
        kv_shape_s = (bs, WINDOW, N_KV_HEADS, HEAD_DIM)
        for lst, val in zip(outs, (pa, sa, pb, sb, pk.reshape(kv_shape_p), sk.reshape(kv_shape_s),
                                   pv.reshape(kv_shape_p), sv.reshape(kv_shape_s), sg)):
            lst.append(val)
    return (xp, xs) + tuple(jnp.stack(o) for o in outs)
```

```python
import jax
import jax.numpy as jnp
from jax import lax
from jax.experimental import pallas as pl
from jax.experimental.pallas import tpu as pltpu

F32 = jnp.float32
BF16 = jnp.bfloat16

D_MODEL = 1024
HEAD_DIM = 64
N_HEADS = 16
N_KV_HEADS = 4
Q_REP = 4
KV_DIM = N_KV_HEADS * HEAD_DIM
ROT_DIM = 16
ROPE_THETA = 500000.0
WINDOW = 128
ATTN_BLOCK = 128
KEY_SPAN = 2 * ATTN_BLOCK
CONV_W = 512
CONF_CONV_W = 31
SC_CONV_W = 3
GMLP_CHUNK = 128
GMLP_GROUPS = 8
GMLP_GROUP_DIM = 64
D_FF = 4 * D_MODEL
EPS = 1e-6
PAST_LEN = 16384
DEC_SEQ = 8
NEG = -1e30

K0 = D_MODEL
V0 = K0 + KV_DIM
A0 = V0 + KV_DIM
B0 = A0 + 2 * CONV_W
C0 = B0 + 3 * CONV_W
G0 = C0 + 2 * CONV_W

LANES = 128
SUBLANES = 8
VMEM_BYTES = 64 * 1024 * 1024

PROMPT_TILE = 512
MLP_TILE = 512
SAMPLE_SEQS = 32
SAMPLE_ATTN_SEQS = 16
CONV_ROWS = 64
A_PAD = 32
B_PAD = 8
MOD_ROWS = 136


def _dot(a, b):
    return jnp.dot(a, b, preferred_element_type=F32)


def _rms_mod(x, g, scale, shift):
    ms = jnp.mean(x * x, axis=-1, keepdims=True)
    y = x * lax.rsqrt(ms + EPS)
    return (y * g) * (1.0 + scale) + shift


def _layer_norm(x, g, b):
    mu = jnp.mean(x, axis=-1, keepdims=True)
    xc = x - mu
    var = jnp.mean(xc * xc, axis=-1, keepdims=True)
    return xc * lax.rsqrt(var + EPS) * g + b


def _head_norm_rope(xc, gain, seg_mean, cos_t, sin_a, sin_b):
    ms = _dot((xc * xc).astype(BF16), seg_mean)
    y = xc * lax.rsqrt(ms + EPS) * gain
    return (y * cos_t + pltpu.roll(y, LANES - ROT_DIM // 2, 1) * sin_a
            + pltpu.roll(y, ROT_DIM // 2, 1) * sin_b)


def _qkv_heads(q_raw, kv_raw, qkg_ref, seg_mean, cos_t, sin_a, sin_b):
    def chunk(src, c, gain):
        return _head_norm_rope(src[:, c * LANES:(c + 1) * LANES], gain, seg_mean,
                               cos_t, sin_a, sin_b)
    q = jnp.concatenate([chunk(q_raw, c, qkg_ref[0:1, :]) for c in range(D_MODEL // LANES)],
                        axis=1)
    k = jnp.concatenate([chunk(kv_raw, c, qkg_ref[1:2, :])
                         for c in range(KV_DIM // LANES)], axis=1)
    return q, k, kv_raw[:, KV_DIM:2 * KV_DIM]


def _kv_head_masks(scale):
    lane = lax.broadcasted_iota(jnp.int32, (1, KV_DIM), 1)
    return [jnp.where((lane >= g * HEAD_DIM) & (lane < (g + 1) * HEAD_DIM), scale, 0.0).astype(F32)
            for g in range(N_KV_HEADS)]


def _softmax_with_sink(s, sink):
    m = jnp.maximum(jnp.max(s, axis=-1, keepdims=True), sink)
    e = jnp.exp(s - m)
    den = jnp.sum(e, axis=-1, keepdims=True) + jnp.exp(sink - m)
    return e * (1.0 / den)


def _causal_conv_rows(hist_ref, row0, rows, w_ref, n_taps, base, bias):
    out = jnp.broadcast_to(bias, (rows, hist_ref.shape[1]))
    for phase in range(SUBLANES):
        span = rows if phase == 0 else rows + SUBLANES
        part = None
        for tap in range(n_taps):
            offset = base + tap
            if offset % SUBLANES != phase:
                continue
            lo = row0 + offset - phase
            term = w_ref[tap:tap + 1, :] * hist_ref[lo:lo + span, :]
            part = term if part is None else part + term
        if part is not None:
            out = out + part[phase:phase + rows]
    return out


def _mod_kernel(c_ref, w_ref, b_ref, o_ref):
    c = c_ref[...]
    s = (c * jax.nn.sigmoid(c)).astype(BF16)
    o_ref[...] = _dot(s, w_ref[...].astype(BF16)) + b_ref[...]


def _modulation(c_all, w_ada, b_ada):
    depth = w_ada.shape[0]
    n_chunks = w_ada.shape[2] // D_MODEL
    return pl.pallas_call(
        _mod_kernel,
        out_shape=jax.ShapeDtypeStruct((depth, MOD_ROWS, n_chunks * D_MODEL), F32),
        grid=(depth, n_chunks),
        in_specs=[
            pl.BlockSpec((MOD_ROWS, D_MODEL), lambda l, j: (0, 0)),
            pl.BlockSpec((None, D_MODEL, D_MODEL), lambda l, j: (l, 0, j)),
            pl.BlockSpec((None, 1, D_MODEL), lambda l, j: (l, 0, j)),
        ],
        out_specs=pl.BlockSpec((None, MOD_ROWS, D_MODEL), lambda l, j: (l, 0, j)),
        compiler_params=pltpu.CompilerParams(
            dimension_semantics=("arbitrary", "arbitrary"),
            vmem_limit_bytes=32 * 1024 * 1024),
        name="adaln_modulation",
    )(c_all, w_ada, b_ada.reshape(depth, 1, -1))


def _per_head_copies(win, head_masks):
    return jnp.concatenate([jnp.where(m, win, jnp.zeros_like(win)) for m in head_masks], axis=0)


def _block_scores(q_blk, k_heads):
    qs = (q_blk * HEAD_DIM ** -0.5).astype(BF16)
    q_rows = jnp.concatenate([qs[:, r * KV_DIM:(r + 1) * KV_DIM] for r in range(Q_REP)], axis=0)
    return lax.dot_general(q_rows, k_heads, (((1,), (1,)), ((), ())), preferred_element_type=F32)


def _block_probs(s, bias, sink_ref):
    tq, keys = bias.shape
    rows = []
    for r in range(Q_REP):
        cols = []
        for g in range(N_KV_HEADS):
            sp = s[r * tq:(r + 1) * tq, g * keys:(g + 1) * keys] + bias
            cols.append(_softmax_with_sink(sp, sink_ref[g * Q_REP + r]).astype(BF16))
        rows.append(jnp.concatenate(cols, axis=1))
    return jnp.concatenate(rows, axis=0)


def _block_output(p, v_heads):
    tq = p.shape[0] // Q_REP
    o = _dot(p, v_heads)
    return jnp.concatenate([o[r * tq:(r + 1) * tq] for r in range(Q_REP)], axis=1)


def _mixer_prompt_kernel(x_ref, mod_ref, rope_ref, sink_ref, n1g_ref, qkg_ref, caw_ref, v512_ref,
                         wpair_ref, bsp_ref, segm_ref,
                         wq_ref, w_in_ref, w_ao_ref, w_aout_ref, w_bout_ref, w_cout_ref, w_o_ref,
                         x1_ref, ca_ref, cb_ref, ko_ref, vo_ref,
                         kprev_s, vprev_s, aprev_s, bprev_s, acat_s, bcat_s, bias_s):
    tile = x_ref.shape[0]
    j = pl.program_id(1)
    last = j == pl.num_programs(1) - 1
    rd = j & 1
    wr = 1 - rd

    @pl.when(j == 0)
    def _():
        kprev_s[0] = jnp.zeros((ATTN_BLOCK, KV_DIM), BF16)
        vprev_s[0] = jnp.zeros((ATTN_BLOCK, KV_DIM), BF16)
        aprev_s[0] = jnp.zeros((A_PAD, CONV_W), F32)
        bprev_s[0] = jnp.zeros((B_PAD, CONV_W), F32)

    x = x_ref[...]
    shift1, scale1, gate1 = mod_ref[0:1, :], mod_ref[1:2, :], mod_ref[2:3, :]
    h = _rms_mod(x, n1g_ref[...], scale1, shift1).astype(BF16)

    def gate_logits(branch):
        lo = G0 + branch * D_MODEL
        return _dot(h, w_in_ref[:, lo:lo + D_MODEL])

    q_raw = _dot(h, wq_ref[...])
    kv_raw = _dot(h, w_in_ref[:, K0:A0])
    a_in = _dot(h, w_in_ref[:, A0:A0 + 2 * CONV_W])
    q, k, v = _qkv_heads(q_raw, kv_raw, qkg_ref, segm_ref[...],
                         rope_ref[0], rope_ref[1], rope_ref[2])
    kcat = jnp.concatenate([kprev_s[rd], k.astype(BF16)], axis=0)
    vcat = jnp.concatenate([vprev_s[rd], v.astype(BF16)], axis=0)

    @pl.when(last)
    def _():
        ko_ref[...] = k[tile - WINDOW:tile]
        vo_ref[...] = v[tile - WINDOW:tile]

    tt = lax.broadcasted_iota(jnp.int32, (ATTN_BLOCK, KEY_SPAN), 0)
    jj = lax.broadcasted_iota(jnp.int32, (ATTN_BLOCK, KEY_SPAN), 1)
    dist = jj - tt
    bias_rest = jnp.where((dist >= 0) & (dist <= WINDOW), 0.0, NEG).astype(F32)

    @pl.when(j == 0)
    def _():
        bias_s[...] = jnp.where(jj < ATTN_BLOCK, NEG, bias_rest)

    @pl.when(j == 1)
    def _():
        bias_s[...] = bias_rest

    bias_first = bias_s[...]
    lane = lax.broadcasted_iota(jnp.int32, (1, KV_DIM), 1)
    head_masks = [(lane >= g * HEAD_DIM) & (lane < (g + 1) * HEAD_DIM) for g in range(N_KV_HEADS)]
    kprev_s[wr] = kcat[tile:tile + ATTN_BLOCK]
    vprev_s[wr] = vcat[tile:tile + ATTN_BLOCK]
    n_blocks = tile // ATTN_BLOCK

    def scores(i):
        lo = i * ATTN_BLOCK
        return _block_scores(q[lo:lo + ATTN_BLOCK],
                             _per_head_copies(kcat[lo:lo + KEY_SPAN], head_masks))

    glu = a_in[:, :CONV_W] * jax.nn.sigmoid(a_in[:, CONV_W:])
    acat_s[0:A_PAD, :] = aprev_s[rd]
    acat_s[A_PAD:A_PAD + tile, :] = glu
    aprev_s[wr] = glu[tile - A_PAD:tile]
    base = A_PAD - (CONF_CONV_W - 1)

    @pl.when(last)
    def _():
        ca_ref[...] = acat_s[base + tile:A_PAD + tile, :]

    n_chunks = tile // CONV_ROWS
    conv_rows = []

    def conv_chunks(lo, hi):
        for c in range(lo, hi):
            conv_rows.append(_causal_conv_rows(acat_s, c * CONV_ROWS, CONV_ROWS, caw_ref,
                                               CONF_CONV_W, base, v512_ref[0:1, :]))

    fillers = [lambda: _dot(h, w_in_ref[:, B0:B0 + 3 * CONV_W]),
               lambda: _dot(h, w_in_ref[:, C0:C0 + 2 * CONV_W])]
    filled = []
    blocks = []
    p = None
    for i in range(n_blocks + 1):
        if i > 0:
            lo = (i - 1) * ATTN_BLOCK
            blocks.append(_block_output(p, _per_head_copies(vcat[lo:lo + KEY_SPAN], head_masks)))
        if i < n_blocks:
            s_blk = scores(i)
        if i < len(fillers):
            filled.append(fillers[i]())
        if i == n_blocks:
            for f in fillers[len(filled):]:
                filled.append(f())
            g0 = gate_logits(0)
            g1 = gate_logits(1)
        if i < n_blocks:
            p = _block_probs(s_blk, bias_first if i == 0 else bias_rest, sink_ref)
            conv_chunks(i, i + 1)
    b_in, c_in = filled
    conv_chunks(n_blocks, n_chunks)
    attn = jnp.concatenate(blocks, axis=0).astype(BF16)
    a_conv = jnp.concatenate(conv_rows, axis=0)
    a_act = jax.nn.silu(_layer_norm(a_conv, v512_ref[1:2, :], v512_ref[2:3, :])).astype(BF16)
    attn_out = _dot(attn, w_ao_ref[...])
    a_out = _dot(a_act, w_aout_ref[...])
    g2 = gate_logits(2)

    ch = b_in[:, CONV_W:2 * CONV_W] * b_in[:, 2 * CONV_W:]
    bcat_s[0:B_PAD, :] = bprev_s[rd]
    bcat_s[B_PAD:B_PAD + tile, :] = ch
    bprev_s[wr] = ch[tile - B_PAD:tile]
    base_b = B_PAD - (SC_CONV_W - 1)
    sc = None
    for t in range(SC_CONV_W):
        term = v512_ref[5 + t:6 + t, :] * bcat_s[base_b + t:base_b + t + tile, :]
        sc = term if sc is None else sc + term

    @pl.when(last)
    def _():
        cb_ref[...] = bcat_s[base_b + tile:B_PAD + tile, :]

    merged = jax.nn.sigmoid(g0) * attn_out
    merged = merged + jax.nn.sigmoid(g1) * a_out

    zc = jax.nn.gelu(c_in)
    u = zc[:, :CONV_W]
    vv = _layer_norm(zc[:, CONV_W:], v512_ref[3:4, :], v512_ref[4:5, :])
    row = lax.broadcasted_iota(jnp.int32, (GMLP_CHUNK, 2 * GMLP_CHUNK), 0)
    col = lax.broadcasted_iota(jnp.int32, (GMLP_CHUNK, 2 * GMLP_CHUNK), 1)
    causal = (col & (GMLP_CHUNK - 1)) <= row
    n_pairs = GMLP_GROUPS // 2
    wpair = [jnp.where(causal, wpair_ref[p], 0.0).astype(BF16) for p in range(n_pairs)]
    low = lax.broadcasted_iota(jnp.int32, (1, LANES), 1) < GMLP_GROUP_DIM
    rhs_blocks = []
    for c in range(tile // GMLP_CHUNK):
        vch = vv[c * GMLP_CHUNK:(c + 1) * GMLP_CHUNK]
        for pair in range(n_pairs):
            blk = vch[:, pair * LANES:(pair + 1) * LANES]
            rhs = jnp.concatenate([jnp.where(low, blk, 0.0), jnp.where(low, 0.0, blk)], axis=0)
            rhs_blocks.append(rhs.astype(BF16))
    b_act = (b_in[:, :CONV_W] * sc).astype(BF16)
    b_out = _dot(b_act, w_bout_ref[...])
    zs = [_dot(wpair[i % n_pairs], rhs) for i, rhs in enumerate(rhs_blocks)]
    g3 = gate_logits(3)
    gated = []
    for c in range(tile // GMLP_CHUNK):
        z = jnp.concatenate(zs[c * n_pairs:(c + 1) * n_pairs], axis=1) + bsp_ref[...]
        gated.append(u[c * GMLP_CHUNK:(c + 1) * GMLP_CHUNK] * z)
    gated = jnp.concatenate(gated, axis=0).astype(BF16)
    merged = merged + jax.nn.sigmoid(g2) * b_out
    c_out = _dot(gated, w_cout_ref[...])
    merged = (merged + jax.nn.sigmoid(g3) * c_out).astype(BF16)
    x1_ref[...] = x + gate1 * _dot(merged, w_o_ref[...])


def _const_spec(shape):
    zeros = (0,) * len(shape)
    return pl.BlockSpec(shape, lambda *_: zeros, pipeline_mode=pl.Buffered(1))


def _layer_spec(stacked, layer, cols=None):
    if cols is None:
        block, col_block = stacked.shape[1:], 0
    else:
        start, width = cols
        assert start % width == 0
        block, col_block = stacked.shape[1:-1] + (width,), start // width
    index = (layer,) + (0,) * (len(block) - 1) + (col_block,)
    return pl.BlockSpec((None,) + block, lambda *_: index, pipeline_mode=pl.Buffered(1))


def _mixer_prompt(x, mod, rope, lw, layer):
    bsz, seq, _ = x.shape
    tile = PROMPT_TILE
    small = [lw["n1g"], lw["qkg"], lw["caw"], lw["v512"], lw["wpair"], lw["bsp"], lw["segm"]]
    weights = [lw["wq"], lw["w_in"], lw["w_ao"], lw["w_aout"], lw["w_bout"], lw["w_cout"],
               lw["w_o"]]
    def per_sequence(r, c):
        return pl.BlockSpec((None, r, c), lambda b, j: (b, 0, 0))

    in_specs = (
        [pl.BlockSpec((None, tile, D_MODEL), lambda b, j: (b, j, 0)),
         per_sequence(6, D_MODEL),
         pl.BlockSpec((3, tile, LANES), lambda b, j: (0, j, 0)),
         pl.BlockSpec(memory_space=pltpu.SMEM)]
        + [_layer_spec(a, layer) for a in small]
        + [_layer_spec(a, layer) for a in weights])
    out_shape = (
        jax.ShapeDtypeStruct((bsz, seq, D_MODEL), F32),
        jax.ShapeDtypeStruct((bsz, CONF_CONV_W - 1, CONV_W), F32),
        jax.ShapeDtypeStruct((bsz, SC_CONV_W - 1, CONV_W), F32),
        jax.ShapeDtypeStruct((bsz, WINDOW, KV_DIM), F32),
        jax.ShapeDtypeStruct((bsz, WINDOW, KV_DIM), F32),
    )
    out_specs = (
        pl.BlockSpec((None, tile, D_MODEL), lambda b, j: (b, j, 0)),
        per_sequence(CONF_CONV_W - 1, CONV_W), per_sequence(SC_CONV_W - 1, CONV_W),
        per_sequence(WINDOW, KV_DIM), per_sequence(WINDOW, KV_DIM),
    )
    return pl.pallas_call(
        _mixer_prompt_kernel,
        out_shape=out_shape,
        grid=(bsz, seq // tile),
        in_specs=in_specs,
        out_specs=out_specs,
        scratch_shapes=[
            pltpu.VMEM((2, ATTN_BLOCK, KV_DIM), BF16),
            pltpu.VMEM((2, ATTN_BLOCK, KV_DIM), BF16),
            pltpu.VMEM((2, A_PAD, CONV_W), F32),
            pltpu.VMEM((2, B_PAD, CONV_W), F32),
            pltpu.VMEM((A_PAD + tile, CONV_W), F32),
            pltpu.VMEM((B_PAD + tile, CONV_W), F32),
            pltpu.VMEM((ATTN_BLOCK, KEY_SPAN), F32),
        ],
        compiler_params=pltpu.CompilerParams(
            dimension_semantics=("arbitrary", "arbitrary"),
            vmem_limit_bytes=VMEM_BYTES - 4 * 1024 * 1024),
        name="mixer_prompt",
    )(x, mod, rope, lw["sinks"][layer], *small, *weights)


SAMPLE_NEW_ROWS = 16


def _sample_attn_kernel(x_ref, mod_ref, rope_ref, sink_ref, n1g_ref, qkg_ref, segm_ref,
                        ck_ref, cv_ref, wq_ref, wkv_ref, w_ao_ref,
                        ao_ref, nk_ref, nv_ref, kcat_s, vcat_s):
    nseq, tq, _ = x_ref.shape
    rows = nseq * tq
    cache_lo = SAMPLE_NEW_ROWS
    cache_hi = cache_lo + WINDOW

    x = x_ref[...]
    shift1, scale1 = mod_ref[:, 0:1, :], mod_ref[:, 1:2, :]
    h = _rms_mod(x, n1g_ref[...], scale1, shift1).reshape(rows, D_MODEL).astype(BF16)

    q, k, v = _qkv_heads(_dot(h, wq_ref[...]), _dot(h, wkv_ref[...]), qkg_ref, segm_ref[...],
                         rope_ref[0], rope_ref[1], rope_ref[2])
    k3 = k.reshape(nseq, tq, KV_DIM)
    v3 = v.reshape(nseq, tq, KV_DIM)
    keep = WINDOW - tq
    nk_ref[:, 0:keep, :] = ck_ref[:, tq:WINDOW, :]
    nk_ref[:, keep:WINDOW, :] = k3
    nv_ref[:, 0:keep, :] = cv_ref[:, tq:WINDOW, :]
    nv_ref[:, keep:WINDOW, :] = v3
    pad = jnp.zeros((nseq, SAMPLE_NEW_ROWS - tq, KV_DIM), F32)
    kcat_s[:, 0:cache_lo, :] = jnp.concatenate([k3, pad], axis=1).astype(BF16)
    vcat_s[:, 0:cache_lo, :] = jnp.concatenate([v3, pad], axis=1).astype(BF16)
    kcat_s[:, cache_lo:cache_hi, :] = ck_ref[...].astype(BF16)
    vcat_s[:, cache_lo:cache_hi, :] = cv_ref[...].astype(BF16)
    tail = jnp.zeros((nseq, KEY_SPAN - cache_hi, KV_DIM), BF16)
    kcat_s[:, cache_hi:KEY_SPAN, :] = tail
    vcat_s[:, cache_hi:KEY_SPAN, :] = tail

    n_q = N_HEADS * tq
    tt = lax.broadcasted_iota(jnp.int32, (n_q, KEY_SPAN), 0) & (tq - 1)
    ii = lax.broadcasted_iota(jnp.int32, (n_q, KEY_SPAN), 1)
    visible = ((ii < tq) & (ii <= tt)) | ((ii >= cache_lo + tt) & (ii < cache_hi))
    bias = jnp.where(visible, 0.0, NEG).astype(F32)
    cm_scaled = _kv_head_masks(HEAD_DIM ** -0.5)
    cm_plain = _kv_head_masks(1.0)
    q3 = q.reshape(nseq, tq, D_MODEL)
    qbd = jnp.concatenate([q3[:, :, r * KV_DIM:(r + 1) * KV_DIM] * cm_scaled[g]
                           for r in range(Q_REP) for g in range(N_KV_HEADS)], axis=1).astype(BF16)
    s = jnp.einsum("bqc,bkc->bqk", qbd, kcat_s[...], preferred_element_type=F32) + bias
    sink = jnp.concatenate([jnp.full((tq, 1), sink_ref[g * Q_REP + r], F32)
                            for r in range(Q_REP) for g in range(N_KV_HEADS)], axis=0)
    p = _softmax_with_sink(s, sink).astype(BF16)
    o = jnp.einsum("bqk,bkc->bqc", p, vcat_s[...], preferred_element_type=F32)
    slabs = []
    for r in range(Q_REP):
        acc = None
        for g in range(N_KV_HEADS):
            i = r * N_KV_HEADS + g
            part = o[:, i * tq:(i + 1) * tq, :] * cm_plain[g]
            acc = part if acc is None else acc + part
        slabs.append(acc)
    attn = jnp.concatenate(slabs, axis=2).reshape(rows, D_MODEL)
    ao_ref[...] = _dot(attn.astype(BF16), w_ao_ref[...]).reshape(nseq, tq, D_MODEL)


def _sample_rest_kernel(x_ref, mod_ref, ao_ref, n1g_ref, caw_ref, v512_ref, wc_ref, bsp_ref,
                        sta_ref, stb_ref, w_in_ref, w_aout_ref, w_bout_ref, w_cout_ref, w_o_ref,
                        x1_ref, na_ref, nb_ref, vv_ref, glu_s, aact_s, bcat_s):
    nseq, tq, _ = x_ref.shape
    rows = nseq * tq

    x = x_ref[...]
    shift1, scale1, gate1 = mod_ref[:, 0:1, :], mod_ref[:, 1:2, :], mod_ref[:, 2:3, :]
    h = _rms_mod(x, n1g_ref[...], scale1, shift1).reshape(rows, D_MODEL).astype(BF16)
    attn_out = ao_ref[...].reshape(rows, D_MODEL)
    merged = jax.nn.sigmoid(_dot(h, w_in_ref[:, G0:G0 + D_MODEL])) * attn_out

    a_in = _dot(h, w_in_ref[:, A0:A0 + 2 * CONV_W])
    glu = a_in[:, :CONV_W] * jax.nn.sigmoid(a_in[:, CONV_W:])
    hist_a = CONF_CONV_W - 1
    glu_s[...] = glu.reshape(nseq, tq, CONV_W)

    def conv_in(t):
        return sta_ref[t] if t < hist_a else glu_s[:, t - hist_a, :]

    for t in range(tq):
        acc = jnp.broadcast_to(v512_ref[0:1, :], (nseq, CONV_W))
        for tap in range(CONF_CONV_W):
            acc = acc + caw_ref[tap:tap + 1, :] * conv_in(t + tap)
        aact_s[:, t, :] = jax.nn.silu(_layer_norm(acc, v512_ref[1:2, :], v512_ref[2:3, :]))
    for t in range(hist_a):
        na_ref[t] = conv_in(t + tq)
    a_out = _dot(aact_s[...].reshape(rows, CONV_W).astype(BF16), w_aout_ref[...])
    merged = merged + jax.nn.sigmoid(_dot(h, w_in_ref[:, G0 + D_MODEL:G0 + 2 * D_MODEL])) * a_out

    b_in = _dot(h, w_in_ref[:, B0:B0 + 3 * CONV_W])
    hist_b = SC_CONV_W - 1
    base_b = B_PAD - hist_b
    bcat_s[:, base_b:B_PAD, :] = stb_ref[...]
    bcat_s[:, B_PAD:B_PAD + tq, :] = (b_in[:, CONV_W:2 * CONV_W]
                                      * b_in[:, 2 * CONV_W:]).reshape(nseq, tq, CONV_W)
    sc = None
    for t in range(SC_CONV_W):
        term = v512_ref[5 + t:6 + t, :] * bcat_s[:, base_b + t:base_b + t + tq, :]
        sc = term if sc is None else sc + term
    nb_ref[...] = bcat_s[:, base_b + tq:B_PAD + tq, :]
    b_out = _dot((b_in[:, :CONV_W] * sc.reshape(rows, CONV_W)).astype(BF16), w_bout_ref[...])
    merged = merged + jax.nn.sigmoid(_dot(h, w_in_ref[:, G0 + 2 * D_MODEL:G0 + 3 * D_MODEL])) * b_out

    zc = jax.nn.gelu(_dot(h, w_in_ref[:, C0:C0 + 2 * CONV_W]))
    u3 = zc[:, :CONV_W].reshape(nseq, tq, CONV_W)
    vv3 = _layer_norm(zc[:, CONV_W:], v512_ref[3:4, :], v512_ref[4:5, :]).reshape(nseq, tq, CONV_W)
    vv_ref[...] = vv3
    t_idx = lax.broadcasted_iota(jnp.int32, (tq, CONV_W), 0)
    z = jnp.broadcast_to(bsp_ref[...], (nseq, tq, CONV_W))
    for src in range(tq):
        coef = jnp.where(t_idx >= src, wc_ref[src], 0.0)
        z = z + coef * vv3[:, src:src + 1, :]
    c_out = _dot((u3 * z).reshape(rows, CONV_W).astype(BF16), w_cout_ref[...])
    merged = merged + jax.nn.sigmoid(_dot(h, w_in_ref[:, G0 + 3 * D_MODEL:G0 + 4 * D_MODEL])) * c_out

    y = _dot(merged.astype(BF16), w_o_ref[...]).reshape(nseq, tq, D_MODEL)
    x1_ref[...] = x + gate1 * y


def _seq_spec(nseq, r, c):
    return pl.BlockSpec((nseq, r, c), lambda i: (i, 0, 0))


def _sample_attention(x, mod, rope, cache_k, cache_v, lw, layer):
    nbatch, tq, _ = x.shape
    nseq = SAMPLE_ATTN_SEQS
    small = [lw["n1g"], lw["qkg"], lw["segm"]]

    def cache_spec():
        return pl.BlockSpec((None, nseq, WINDOW, KV_DIM), lambda i: (layer, i, 0, 0))

    in_specs = (
        [_seq_spec(nseq, tq, D_MODEL), _seq_spec(nseq, 6, D_MODEL),
         _const_spec(rope.shape), pl.BlockSpec(memory_space=pltpu.SMEM)]
        + [_layer_spec(a, layer) for a in small]
        + [cache_spec(), cache_spec(),
           _layer_spec(lw["wq"], layer), _layer_spec(lw["w_in"], layer, cols=(K0, 2 * KV_DIM)),
           _layer_spec(lw["w_ao"], layer)])
    out_shape = (
        jax.ShapeDtypeStruct((nbatch, tq, D_MODEL), F32),
        jax.ShapeDtypeStruct((nbatch, WINDOW, KV_DIM), F32),
        jax.ShapeDtypeStruct((nbatch, WINDOW, KV_DIM), F32),
    )
    out_specs = (_seq_spec(nseq, tq, D_MODEL), _seq_spec(nseq, WINDOW, KV_DIM),
                 _seq_spec(nseq, WINDOW, KV_DIM))
    return pl.pallas_call(
        _sample_attn_kernel,
        out_shape=out_shape,
        grid=(nbatch // nseq,),
        in_specs=in_specs,
        out_specs=out_specs,
        scratch_shapes=[
            pltpu.VMEM((nseq, KEY_SPAN, KV_DIM), BF16),
            pltpu.VMEM((nseq, KEY_SPAN, KV_DIM), BF16),
        ],
        compiler_params=pltpu.CompilerParams(
            dimension_semantics=("arbitrary",),
            vmem_limit_bytes=VMEM_BYTES - 16 * 1024 * 1024),
        name="sample_attention",
    )(x, mod, rope, lw["sinks"][layer], *small, cache_k, cache_v, lw["wq"], lw["w_in"], lw["w_ao"])


def _sample_rest(x, mod, attn_out, state_a, state_b, lw, layer):
    nbatch, tq, _ = x.shape
    nseq = SAMPLE_SEQS
    small = [lw["n1g"], lw["caw"], lw["v512"], lw["wc8"], lw["bsp8"]]
    weights = [lw["w_in"], lw["w_aout"], lw["w_bout"], lw["w_cout"], lw["w_o"]]
    hist_a = CONF_CONV_W - 1
    in_specs = (
        [_seq_spec(nseq, tq, D_MODEL), _seq_spec(nseq, 6, D_MODEL), _seq_spec(nseq, tq, D_MODEL)]
        + [_layer_spec(a, layer) for a in small]
        + [pl.BlockSpec((None, hist_a, nseq, CONV_W), lambda i: (layer, 0, i, 0)),
           pl.BlockSpec((None, nseq, SC_CONV_W - 1, CONV_W), lambda i: (layer, i, 0, 0))]
        + [_layer_spec(a, layer) for a in weights])
    out_shape = (
        jax.ShapeDtypeStruct((nbatch, tq, D_MODEL), F32),
        jax.ShapeDtypeStruct((hist_a, nbatch, CONV_W), F32),
        jax.ShapeDtypeStruct((nbatch, SC_CONV_W - 1, CONV_W), F32),
        jax.ShapeDtypeStruct((nbatch, tq, CONV_W), F32),
    )
    out_specs = (_seq_spec(nseq, tq, D_MODEL),
                 pl.BlockSpec((hist_a, nseq, CONV_W), lambda i: (0, i, 0)),
                 _seq_spec(nseq, SC_CONV_W - 1, CONV_W), _seq_spec(nseq, tq, CONV_W))
    return pl.pallas_call(
        _sample_rest_kernel,
        out_shape=out_shape,
        grid=(nbatch // nseq,),
        in_specs=in_specs,
        out_specs=out_specs,
        scratch_shapes=[
            pltpu.VMEM((nseq, tq, CONV_W), F32),
            pltpu.VMEM((nseq, tq, CONV_W), F32),
            pltpu.VMEM((nseq, B_PAD + tq, CONV_W), F32),
        ],
        compiler_params=pltpu.CompilerParams(
            dimension_semantics=("arbitrary",),
            vmem_limit_bytes=VMEM_BYTES - 8 * 1024 * 1024),
        name="sample_rest",
    )(x, mod, attn_out, *small, state_a, state_b, *weights)


def _mlp_kernel(x_ref, mod_ref, g_ref, w_up_ref, w_down_ref, o_ref):
    x = x_ref[...]
    if x.ndim == 3:
        shift2, scale2, gate2 = mod_ref[:, 3:4, :], mod_ref[:, 4:5, :], mod_ref[:, 5:6, :]
    else:
        shift2, scale2, gate2 = mod_ref[3:4, :], mod_ref[4:5, :], mod_ref[5:6, :]
    h2 = _rms_mod(x, g_ref[...], scale2, shift2)
    up = _dot(h2.reshape(-1, D_MODEL).astype(BF16), w_up_ref[...])
    act = jnp.square(jnp.maximum(up, 0.0)).astype(BF16)
    y = _dot(act, w_down_ref[...]).reshape(x.shape)
    o_ref[...] = x + gate2 * y


def _mlp(x, mod, lw, layer, per_sequence):
    if per_sequence:
        nbatch, tq, _ = x.shape
        nseq = MLP_TILE // tq
        grid = (nbatch // nseq,)
        x_spec = pl.BlockSpec((nseq, tq, D_MODEL), lambda i: (i, 0, 0))
        mod_spec = pl.BlockSpec((nseq, 6, D_MODEL), lambda i: (i, 0, 0))
    else:
        bsz, seq, _ = x.shape
        grid = (bsz, seq // MLP_TILE)
        x_spec = pl.BlockSpec((None, MLP_TILE, D_MODEL), lambda b, j: (b, j, 0))
        mod_spec = pl.BlockSpec((None, 6, D_MODEL), lambda b, j: (b, 0, 0))
    return pl.pallas_call(
        _mlp_kernel,
        out_shape=jax.ShapeDtypeStruct(x.shape, F32),
        grid=grid,
        in_specs=[x_spec, mod_spec, _layer_spec(lw["n2g"], layer),
                  _layer_spec(lw["w_up"], layer), _layer_spec(lw["w_down"], layer)],
        out_specs=x_spec,
        compiler_params=pltpu.CompilerParams(
            dimension_semantics=("arbitrary",) * len(grid),
            vmem_limit_bytes=VMEM_BYTES - 16 * 1024 * 1024),
        name="mlp_sample" if per_sequence else "mlp_prompt",
    )(x, mod, lw["n2g"], lw["w_up"], lw["w_down"])


def _rope_tables(pos):
    half = ROT_DIM // 2
    inv_freq = jnp.power(ROPE_THETA, -jnp.arange(half, dtype=F32) * 2.0 / ROT_DIM)
    ang = pos.astype(F32)[:, None] * inv_freq[None, :]
    cos, sin = jnp.cos(ang), jnp.sin(ang)
    lane = jnp.arange(LANES) % HEAD_DIM
    f = lane % half
    cos_t = jnp.where(lane < ROT_DIM, cos[:, f], 1.0)
    sin_a = jnp.where(lane < half, -sin[:, f], 0.0)
    sin_b = jnp.where((lane >= half) & (lane < ROT_DIM), sin[:, f], 0.0)
    return jnp.stack([cos_t, sin_a, sin_b]).astype(F32)


def _prepare_weights(p):
    depth = p["w_in"].shape[0]
    w_in = p["w_in"]
    wq = w_in[:, :, :D_MODEL].reshape(depth, D_MODEL, N_KV_HEADS, Q_REP, HEAD_DIM)
    wq = wq.transpose(0, 1, 3, 2, 4).reshape(depth, D_MODEL, D_MODEL).astype(BF16)
    w_ao = p["w_attn_o"].reshape(depth, N_KV_HEADS, Q_REP, HEAD_DIM, D_MODEL)
    w_ao = w_ao.transpose(0, 2, 1, 3, 4).reshape(depth, D_MODEL, D_MODEL).astype(BF16)
    ws = p["w_spatial"]
    n_pairs = GMLP_GROUPS // 2
    wpair = ws.reshape(depth, n_pairs, 2, GMLP_CHUNK, GMLP_CHUNK).transpose(0, 1, 3, 2, 4)
    wpair = wpair.reshape(depth, n_pairs, GMLP_CHUNK, 2 * GMLP_CHUNK)
    bsp = jnp.repeat(p["b_spatial"].transpose(0, 2, 1), GMLP_GROUP_DIM, axis=2)
    wc8 = jnp.repeat(ws[:, :, :DEC_SEQ, :DEC_SEQ].transpose(0, 3, 2, 1),
                     GMLP_GROUP_DIM, axis=3)
    lane_head = jnp.arange(LANES) // HEAD_DIM
    segm = jnp.where(lane_head[:, None] == lane_head[None, :], 1.0 / HEAD_DIM, 0.0).astype(BF16)
    caw = jnp.concatenate([p["conv_a_w"], jnp.zeros((depth, 1, CONV_W), F32)], axis=1)
    v512 = jnp.concatenate([p["conv_a_b"][:, None], p["ln_a_g"][:, None], p["ln_a_b"][:, None],
                            p["ln_c_g"][:, None], p["ln_c_b"][:, None], p["conv_b_w"]], axis=1)
    reps = LANES // HEAD_DIM
    return {
        "wq": wq, "w_in": w_in.astype(BF16), "w_ao": w_ao,
        "w_aout": p["w_a_out"].astype(BF16), "w_bout": p["w_b_out"].astype(BF16),
        "w_cout": p["w_c_out"].astype(BF16), "w_o": p["w_o"].astype(BF16),
        "w_up": p["w_up"].astype(BF16), "w_down": p["w_down"].astype(BF16),
        "n1g": p["norm1_g"][:, None], "n2g": p["norm2_g"][:, None],
        "qkg": jnp.stack([jnp.tile(p["q_norm_g"], (1, reps)),
                          jnp.tile(p["k_norm_g"], (1, reps))], axis=1),
        "caw": caw, "v512": v512, "wpair": wpair, "bsp": bsp, "wc8": wc8,
        "bsp8": bsp[:, :DEC_SEQ],
        "segm": jnp.broadcast_to(segm, (depth, LANES, LANES)), "sinks": p["attn_sinks"],
    }


def kernel(x_prompt, x_sample, c_prompt, c_sample, state_conv_a, state_conv_b, cache_k_win, cache_v_win, w_ada, b_ada, norm1_g, norm2_g, w_in, q_norm_g, k_norm_g, attn_sinks, w_attn_o, conv_a_w, conv_a_b, ln_a_g, ln_a_b, w_a_out, conv_b_w, w_b_out, ln_c_g, ln_c_b, w_spatial, b_spatial, w_c_out, w_o, w_up, w_down):
    params = dict(w_in=w_in, w_attn_o=w_attn_o, w_spatial=w_spatial, b_spatial=b_spatial,
                  conv_a_w=conv_a_w, conv_a_b=conv_a_b, ln_a_g=ln_a_g, ln_a_b=ln_a_b,
                  ln_c_g=ln_c_g, ln_c_b=ln_c_b, conv_b_w=conv_b_w, w_a_out=w_a_out,
                  w_b_out=w_b_out, w_c_out=w_c_out, w_o=w_o, w_up=w_up, w_down=w_down,
                  norm1_g=norm1_g, norm2_g=norm2_g, q_norm_g=q_norm_g, k_norm_g=k_norm_g,
                  attn_sinks=attn_sinks)
    depth = w_in.shape[0]
    bp, seq, _ = x_prompt.shape
    bs, tq, _ = x_sample.shape

    c_all = jnp.concatenate([c_prompt, c_sample,
                             jnp.zeros((MOD_ROWS - bp - bs, D_MODEL), F32)], axis=0)
    mod_all = _modulation(c_all, w_ada, b_ada)
    rope_p = _rope_tables(jnp.arange(seq))
    rope_s = jnp.tile(_rope_tables(PAST_LEN + jnp.arange(tq)), (1, SAMPLE_ATTN_SEQS, 1))
    state_a_tm = state_conv_a.transpose(0, 2, 1, 3)
    cache_k = cache_k_win.reshape(depth, bs, WINDOW, KV_DIM)
    cache_v = cache_v_win.reshape(depth, bs, WINDOW, KV_DIM)

    xp, xs = x_prompt, x_sample
    outs = [[] for _ in range(9)]
    lw = _prepare_weights(params)
    for l in range(depth):
        mod_p = mod_all[l, :bp].reshape(bp, 6, D_MODEL)
        mod_s = mod_all[l, bp:bp + bs].reshape(bs, 6, D_MODEL)
        xp, pa, pb, pk, pv = _mixer_prompt(xp, mod_p, rope_p, lw, l)
        xp = _mlp(xp, mod_p, lw, l, per_sequence=False)
        s_attn, sk, sv = _sample_attention(xs, mod_s, rope_s, cache_k, cache_v, lw, l)
        xs, sa, sb, sg = _sample_rest(xs, mod_s, s_attn, state_a_tm, state_conv_b, lw, l)
        sa = sa.transpose(1, 0, 2)
        xs = _mlp(xs, mod_s, lw, l, per_sequence=True)
        kv_shape_p = (bp, WINDOW, N_KV_HEADS, HEAD_DIM)
        kv_shape_s = (bs, WINDOW, N_KV_HEADS, HEAD_DIM)
        for lst, val in zip(outs, (pa, sa, pb, sb, pk.reshape(kv_shape_p), sk.reshape(kv_shape_s),
                                   pv.reshape(kv_shape_p), sv.reshape(kv_shape_s), sg)):
            lst.append(val)
    return (xp, xs) + tuple(jnp.stack(o) for o in outs)
```

```python
import jax
import jax.numpy as jnp
from jax import lax
from jax.experimental import pallas as pl
from jax.experimental.pallas import tpu as pltpu

F32 = jnp.float32
BF16 = jnp.bfloat16

D_MODEL = 1024
HEAD_DIM = 64
N_HEADS = 16
N_KV_HEADS = 4
Q_REP = 4
KV_DIM = N_KV_HEADS * HEAD_DIM
ROT_DIM = 16
ROPE_THETA = 500000.0
WINDOW = 128
ATTN_BLOCK = 128
KEY_SPAN = 2 * ATTN_BLOCK
CONV_W = 512
CONF_CONV_W = 31
SC_CONV_W = 3
GMLP_CHUNK = 128
GMLP_GROUPS = 8
GMLP_GROUP_DIM = 64
D_FF = 4 * D_MODEL
EPS = 1e-6
PAST_LEN = 16384
DEC_SEQ = 8
NEG = -1e30

K0 = D_MODEL
V0 = K0 + KV_DIM
A0 = V0 + KV_DIM
B0 = A0 + 2 * CONV_W
C0 = B0 + 3 * CONV_W
G0 = C0 + 2 * CONV_W

LANES = 128
SUBLANES = 8
VMEM_BYTES = 64 * 1024 * 1024

PROMPT_TILE = 512
MLP_TILE = 512
SAMPLE_SEQS = 32
SAMPLE_ATTN_SEQS = 16
CONV_ROWS = 64
A_PAD = 32
B_PAD = 8
MOD_ROWS = 136


def _dot(a, b):
    return jnp.dot(a, b, preferred_element_type=F32)


def _rms_mod(x, g, scale, shift):
    ms = jnp.mean(x * x, axis=-1, keepdims=True)
    y = x * lax.rsqrt(ms + EPS)
    return (y * g) * (1.0 + scale) + shift


def _layer_norm(x, g, b):
    mu = jnp.mean(x, axis=-1, keepdims=True)
    xc = x - mu
    var = jnp.mean(xc * xc, axis=-1, keepdims=True)
    return xc * lax.rsqrt(var + EPS) * g + b


def _head_norm_rope(xc, gain, seg_mean, cos_t, sin_a, sin_b):
    ms = _dot((xc * xc).astype(BF16), seg_mean)
    y = xc * lax.rsqrt(ms + EPS) * gain
    return (y * cos_t + pltpu.roll(y, LANES - ROT_DIM // 2, 1) * sin_a
            + pltpu.roll(y, ROT_DIM // 2, 1) * sin_b)


def _qkv_heads(q_raw, kv_raw, qkg_ref, seg_mean, cos_t, sin_a, sin_b):
    def chunk(src, c, gain):
        return _head_norm_rope(src[:, c * LANES:(c + 1) * LANES], gain, seg_mean,
                               cos_t, sin_a, sin_b)
    q = jnp.concatenate([chunk(q_raw, c, qkg_ref[0:1, :]) for c in range(D_MODEL // LANES)],
                        axis=1)
    k = jnp.concatenate([chunk(kv_raw, c, qkg_ref[1:2, :])
                         for c in range(KV_DIM // LANES)], axis=1)
    return q, k, kv_raw[:, KV_DIM:2 * KV_DIM]


def _kv_head_masks(scale):
    lane = lax.broadcasted_iota(jnp.int32, (1, KV_DIM), 1)
    return [jnp.where((lane >= g * HEAD_DIM) & (lane < (g + 1) * HEAD_DIM), scale, 0.0).astype(F32)
            for g in range(N_KV_HEADS)]


def _softmax_with_sink(s, sink):
    m = jnp.maximum(jnp.max(s, axis=-1, keepdims=True), sink)
    e = jnp.exp(s - m)
    den = jnp.sum(e, axis=-1, keepdims=True) + jnp.exp(sink - m)
    return e * (1.0 / den)


def _causal_conv_rows(hist_ref, row0, rows, w_ref, n_taps, base, bias):
    out = jnp.broadcast_to(bias, (rows, hist_ref.shape[1]))
    for phase in range(SUBLANES):
        span = rows if phase == 0 else rows + SUBLANES
        part = None
        for tap in range(n_taps):
            offset = base + tap
            if offset % SUBLANES != phase:
                continue
            lo = row0 + offset - phase
            term = w_ref[tap:tap + 1, :] * hist_ref[lo:lo + span, :]
            part = term if part is None else part + term
        if part is not None:
            out = out + part[phase:phase + rows]
    return out


def _mod_kernel(c_ref, w_ref, b_ref, o_ref):
    c = c_ref[...]
    s = (c * jax.nn.sigmoid(c)).astype(BF16)
    o_ref[...] = _dot(s, w_ref[...].astype(BF16)) + b_ref[...]


def _modulation(c_all, w_ada, b_ada):
    depth = w_ada.shape[0]
    n_chunks = w_ada.shape[2] // D_MODEL
    return pl.pallas_call(
        _mod_kernel,
        out_shape=jax.ShapeDtypeStruct((depth, MOD_ROWS, n_chunks * D_MODEL), F32),
        grid=(depth, n_chunks),
        in_specs=[
            pl.BlockSpec((MOD_ROWS, D_MODEL), lambda l, j: (0, 0)),
            pl.BlockSpec((None, D_MODEL, D_MODEL), lambda l, j: (l, 0, j)),
            pl.BlockSpec((None, 1, D_MODEL), lambda l, j: (l, 0, j)),
        ],
        out_specs=pl.BlockSpec((None, MOD_ROWS, D_MODEL), lambda l, j: (l, 0, j)),
        compiler_params=pltpu.CompilerParams(
            dimension_semantics=("arbitrary", "arbitrary"),
            vmem_limit_bytes=32 * 1024 * 1024),
        name="adaln_modulation",
    )(c_all, w_ada, b_ada.reshape(depth, 1, -1))


def _per_head_copies(win, head_masks):
    return jnp.concatenate([jnp.where(m, win, jnp.zeros_like(win)) for m in head_masks], axis=0)


def _block_scores(q_blk, k_heads):
    qs = (q_blk * HEAD_DIM ** -0.5).astype(BF16)
    q_rows = jnp.concatenate([qs[:, r * KV_DIM:(r + 1) * KV_DIM] for r in range(Q_REP)], axis=0)
    return lax.dot_general(q_rows, k_heads, (((1,), (1,)), ((), ())), preferred_element_type=F32)


def _block_probs(s, bias, sink_ref):
    tq, keys = bias.shape
    rows = []
    for r in range(Q_REP):
        cols = []
        for g in range(N_KV_HEADS):
            sp = s[r * tq:(r + 1) * tq, g * keys:(g + 1) * keys] + bias
            cols.append(_softmax_with_sink(sp, sink_ref[g * Q_REP + r]).astype(BF16))
        rows.append(jnp.concatenate(cols, axis=1))
    return jnp.concatenate(rows, axis=0)


def _block_output(p, v_heads):
    tq = p.shape[0] // Q_REP
    o = _dot(p, v_heads)
    return jnp.concatenate([o[r * tq:(r + 1) * tq] for r in range(Q_REP)], axis=1)


def _mixer_prompt_kernel(x_ref, mod_ref, rope_ref, rope0_ref, sink_ref, n1g_ref, qkg_ref, caw_ref, v512_ref,
                         wpair_ref, bsp_ref, segm_ref,
                         wq_ref, w_in_ref, w_ao_ref, w_aout_ref, w_bout_ref, w_cout_ref, w_o_ref,
                         x1_ref, ca_ref, cb_ref, ko_ref, vo_ref,
                         kprev_s, vprev_s, aprev_s, bprev_s, acat_s, bcat_s, bias_s):
    tile = x_ref.shape[0]
    j = pl.program_id(1)
    last = j == pl.num_programs(1) - 1
    rd = j & 1
    wr = 1 - rd

    @pl.when(j == 0)
    def _():
        kprev_s[0] = jnp.zeros((ATTN_BLOCK, KV_DIM), BF16)
        vprev_s[0] = jnp.zeros((ATTN_BLOCK, KV_DIM), BF16)
        aprev_s[0] = jnp.zeros((A_PAD, CONV_W), F32)
        bprev_s[0] = jnp.zeros((B_PAD, CONV_W), F32)

    x = x_ref[...]
    shift1, scale1, gate1 = mod_ref[0:1, :], mod_ref[1:2, :], mod_ref[2:3, :]
    h = _rms_mod(x, n1g_ref[...], scale1, shift1).astype(BF16)

    def gate_logits(branch):
        lo = G0 + branch * D_MODEL
        return _dot(h, w_in_ref[:, lo:lo + D_MODEL])

    q_raw = _dot(h, wq_ref[...])
    kv_raw = _dot(h, w_in_ref[:, K0:A0])
    a_in = _dot(h, w_in_ref[:, A0:A0 + 2 * CONV_W])
    cos_l, sin_l = rope_ref[0], rope_ref[1]
    cos_0, sin_0 = rope0_ref[0:1, :], rope0_ref[1:2, :]
    cos_t, sin_a, sin_b = _rope_lane_tables(cos_0 * cos_l - sin_0 * sin_l,
                                            sin_0 * cos_l + cos_0 * sin_l)
    q, k, v = _qkv_heads(q_raw, kv_raw, qkg_ref, segm_ref[...], cos_t, sin_a, sin_b)
    kcat = jnp.concatenate([kprev_s[rd], k.astype(BF16)], axis=0)
    vcat = jnp.concatenate([vprev_s[rd], v.astype(BF16)], axis=0)

    @pl.when(last)
    def _():
        ko_ref[...] = k[tile - WINDOW:tile]
        vo_ref[...] = v[tile - WINDOW:tile]

    tt = lax.broadcasted_iota(jnp.int32, (ATTN_BLOCK, KEY_SPAN), 0)
    jj = lax.broadcasted_iota(jnp.int32, (ATTN_BLOCK, KEY_SPAN), 1)
    dist = jj - tt
    bias_rest = jnp.where((dist >= 0) & (dist <= WINDOW), 0.0, NEG).astype(F32)

    @pl.when(j == 0)
    def _():
        bias_s[...] = jnp.where(jj < ATTN_BLOCK, NEG, bias_rest)

    @pl.when(j == 1)
    def _():
        bias_s[...] = bias_rest

    bias_first = bias_s[...]
    lane = lax.broadcasted_iota(jnp.int32, (1, KV_DIM), 1)
    head_masks = [(lane >= g * HEAD_DIM) & (lane < (g + 1) * HEAD_DIM) for g in range(N_KV_HEADS)]
    kprev_s[wr] = kcat[tile:tile + ATTN_BLOCK]
    vprev_s[wr] = vcat[tile:tile + ATTN_BLOCK]
    n_blocks = tile // ATTN_BLOCK

    def scores(i):
        lo = i * ATTN_BLOCK
        return _block_scores(q[lo:lo + ATTN_BLOCK],
                             _per_head_copies(kcat[lo:lo + KEY_SPAN], head_masks))

    glu = a_in[:, :CONV_W] * jax.nn.sigmoid(a_in[:, CONV_W:])
    acat_s[0:A_PAD, :] = aprev_s[rd]
    acat_s[A_PAD:A_PAD + tile, :] = glu
    aprev_s[wr] = glu[tile - A_PAD:tile]
    base = A_PAD - (CONF_CONV_W - 1)

    @pl.when(last)
    def _():
        ca_ref[...] = acat_s[base + tile:A_PAD + tile, :]

    n_chunks = tile // CONV_ROWS
    conv_rows = []

    def conv_chunks(lo, hi):
        for c in range(lo, hi):
            conv_rows.append(_causal_conv_rows(acat_s, c * CONV_ROWS, CONV_ROWS, caw_ref,
                                               CONF_CONV_W, base, v512_ref[0:1, :]))

    fillers = [lambda: _dot(h, w_in_ref[:, B0:B0 + 3 * CONV_W]),
               lambda: _dot(h, w_in_ref[:, C0:C0 + 2 * CONV_W])]
    filled = []
    blocks = []
    p = None
    for i in range(n_blocks + 1):
        if i > 0:
            lo = (i - 1) * ATTN_BLOCK
            blocks.append(_block_output(p, _per_head_copies(vcat[lo:lo + KEY_SPAN], head_masks)))
        if i < n_blocks:
            s_blk = scores(i)
        if i < len(fillers):
            filled.append(fillers[i]())
        if i == n_blocks:
            for f in fillers[len(filled):]:
                filled.append(f())
            g0 = gate_logits(0)
            g1 = gate_logits(1)
        if i < n_blocks:
            p = _block_probs(s_blk, bias_first if i == 0 else bias_rest, sink_ref)
            conv_chunks(i, i + 1)
    b_in, c_in = filled
    conv_chunks(n_blocks, n_chunks)
    attn = jnp.concatenate(blocks, axis=0).astype(BF16)
    a_conv = jnp.concatenate(conv_rows, axis=0)
    a_act = jax.nn.silu(_layer_norm(a_conv, v512_ref[1:2, :], v512_ref[2:3, :])).astype(BF16)
    attn_out = _dot(attn, w_ao_ref[...])
    a_out = _dot(a_act, w_aout_ref[...])
    g2 = gate_logits(2)

    ch = b_in[:, CONV_W:2 * CONV_W] * b_in[:, 2 * CONV_W:]
    bcat_s[0:B_PAD, :] = bprev_s[rd]
    bcat_s[B_PAD:B_PAD + tile, :] = ch
    bprev_s[wr] = ch[tile - B_PAD:tile]
    base_b = B_PAD - (SC_CONV_W - 1)
    sc = None
    for t in range(SC_CONV_W):
        term = v512_ref[5 + t:6 + t, :] * bcat_s[base_b + t:base_b + t + tile, :]
        sc = term if sc is None else sc + term

    @pl.when(last)
    def _():
        cb_ref[...] = bcat_s[base_b + tile:B_PAD + tile, :]

    merged = jax.nn.sigmoid(g0) * attn_out
    merged = merged + jax.nn.sigmoid(g1) * a_out

    zc = jax.nn.gelu(c_in)
    u = zc[:, :CONV_W]
    vv = _layer_norm(zc[:, CONV_W:], v512_ref[3:4, :], v512_ref[4:5, :])
    row = lax.broadcasted_iota(jnp.int32, (GMLP_CHUNK, 2 * GMLP_CHUNK), 0)
    col = lax.broadcasted_iota(jnp.int32, (GMLP_CHUNK, 2 * GMLP_CHUNK), 1)
    causal = (col & (GMLP_CHUNK - 1)) <= row
    n_pairs = GMLP_GROUPS // 2
    wpair = [jnp.where(causal, wpair_ref[p], 0.0).astype(BF16) for p in range(n_pairs)]
    low = lax.broadcasted_iota(jnp.int32, (1, LANES), 1) < GMLP_GROUP_DIM
    rhs_blocks = []
    for c in range(tile // GMLP_CHUNK):
        vch = vv[c * GMLP_CHUNK:(c + 1) * GMLP_CHUNK]
        for pair in range(n_pairs):
            blk = vch[:, pair * LANES:(pair + 1) * LANES]
            rhs = jnp.concatenate([jnp.where(low, blk, 0.0), jnp.where(low, 0.0, blk)], axis=0)
            rhs_blocks.append(rhs.astype(BF16))
    b_act = (b_in[:, :CONV_W] * sc).astype(BF16)
    b_out = _dot(b_act, w_bout_ref[...])
    zs = [_dot(wpair[i % n_pairs], rhs) for i, rhs in enumerate(rhs_blocks)]
    g3 = gate_logits(3)
    gated = []
    for c in range(tile // GMLP_CHUNK):
        z = jnp.concatenate(zs[c * n_pairs:(c + 1) * n_pairs], axis=1) + bsp_ref[...]
        gated.append(u[c * GMLP_CHUNK:(c + 1) * GMLP_CHUNK] * z)
    gated = jnp.concatenate(gated, axis=0).astype(BF16)
    merged = merged + jax.nn.sigmoid(g2) * b_out
    c_out = _dot(gated, w_cout_ref[...])
    merged = (merged + jax.nn.sigmoid(g3) * c_out).astype(BF16)
    x1_ref[...] = x + gate1 * _dot(merged, w_o_ref[...])


def _const_spec(shape):
    zeros = (0,) * len(shape)
    return pl.BlockSpec(shape, lambda *_: zeros, pipeline_mode=pl.Buffered(1))


def _layer_spec(stacked, layer, cols=None):
    if cols is None:
        block, col_block = stacked.shape[1:], 0
    else:
        start, width = cols
        assert start % width == 0
        block, col_block = stacked.shape[1:-1] + (width,), start // width
    index = (layer,) + (0,) * (len(block) - 1) + (col_block,)
    return pl.BlockSpec((None,) + block, lambda *_: index, pipeline_mode=pl.Buffered(1))


def _mixer_prompt(x, mod, rope_local, rope_tile, lw, layer):
    bsz, seq, _ = x.shape
    tile = PROMPT_TILE
    small = [lw["n1g"], lw["qkg"], lw["caw"], lw["v512"], lw["wpair"], lw["bsp"], lw["segm"]]
    weights = [lw["wq"], lw["w_in"], lw["w_ao"], lw["w_aout"], lw["w_bout"], lw["w_cout"],
               lw["w_o"]]
    def per_sequence(r, c):
        return pl.BlockSpec((None, r, c), lambda b, j: (b, 0, 0))

    in_specs = (
        [pl.BlockSpec((None, tile, D_MODEL), lambda b, j: (b, j, 0)),
         per_sequence(6, D_MODEL),
         _const_spec(rope_local.shape),
         pl.BlockSpec((None, 2, LANES), lambda b, j: (j, 0, 0)),
         pl.BlockSpec(memory_space=pltpu.SMEM)]
        + [_layer_spec(a, layer) for a in small]
        + [_layer_spec(a, layer) for a in weights])
    out_shape = (
        jax.ShapeDtypeStruct((bsz, seq, D_MODEL), F32),
        jax.ShapeDtypeStruct((bsz, CONF_CONV_W - 1, CONV_W), F32),
        jax.ShapeDtypeStruct((bsz, SC_CONV_W - 1, CONV_W), F32),
        jax.ShapeDtypeStruct((bsz, WINDOW, KV_DIM), F32),
        jax.ShapeDtypeStruct((bsz, WINDOW, KV_DIM), F32),
    )
    out_specs = (
        pl.BlockSpec((None, tile, D_MODEL), lambda b, j: (b, j, 0)),
        per_sequence(CONF_CONV_W - 1, CONV_W), per_sequence(SC_CONV_W - 1, CONV_W),
        per_sequence(WINDOW, KV_DIM), per_sequence(WINDOW, KV_DIM),
    )
    return pl.pallas_call(
        _mixer_prompt_kernel,
        out_shape=out_shape,
        grid=(bsz, seq // tile),
        in_specs=in_specs,
        out_specs=out_specs,
        scratch_shapes=[
            pltpu.VMEM((2, ATTN_BLOCK, KV_DIM), BF16),
            pltpu.VMEM((2, ATTN_BLOCK, KV_DIM), BF16),
            pltpu.VMEM((2, A_PAD, CONV_W), F32),
            pltpu.VMEM((2, B_PAD, CONV_W), F32),
            pltpu.VMEM((A_PAD + tile, CONV_W), F32),
            pltpu.VMEM((B_PAD + tile, CONV_W), F32),
            pltpu.VMEM((ATTN_BLOCK, KEY_SPAN), F32),
        ],
        compiler_params=pltpu.CompilerParams(
            dimension_semantics=("arbitrary", "arbitrary"),
            vmem_limit_bytes=VMEM_BYTES - 4 * 1024 * 1024),
        name="mixer_prompt",
    )(x, mod, rope_local, rope_tile, lw["sinks"][layer], *small, *weights)


SAMPLE_NEW_ROWS = 16


def _sample_attn_kernel(x_ref, mod_ref, rope_ref, sink_ref, n1g_ref, qkg_ref, segm_ref,
                        ck_ref, cv_ref, wq_ref, wkv_ref, w_ao_ref,
                        ao_ref, nk_ref, nv_ref, kcat_s, vcat_s):
    nseq, tq, _ = x_ref.shape
    rows = nseq * tq
    cache_lo = SAMPLE_NEW_ROWS
    cache_hi = cache_lo + WINDOW

    x = x_ref[...]
    shift1, scale1 = mod_ref[:, 0:1, :], mod_ref[:, 1:2, :]
    h = _rms_mod(x, n1g_ref[...], scale1, shift1).reshape(rows, D_MODEL).astype(BF16)

    q, k, v = _qkv_heads(_dot(h, wq_ref[...]), _dot(h, wkv_ref[...]), qkg_ref, segm_ref[...],
                         *_rope_lane_tables(rope_ref[0], rope_ref[1]))
    k3 = k.reshape(nseq, tq, KV_DIM)
    v3 = v.reshape(nseq, tq, KV_DIM)
    keep = WINDOW - tq
    nk_ref[:, 0:keep, :] = ck_ref[:, tq:WINDOW, :]
    nk_ref[:, keep:WINDOW, :] = k3
    nv_ref[:, 0:keep, :] = cv_ref[:, tq:WINDOW, :]
    nv_ref[:, keep:WINDOW, :] = v3
    pad = jnp.zeros((nseq, SAMPLE_NEW_ROWS - tq, KV_DIM), F32)
    kcat_s[:, 0:cache_lo, :] = jnp.concatenate([k3, pad], axis=1).astype(BF16)
    vcat_s[:, 0:cache_lo, :] = jnp.concatenate([v3, pad], axis=1).astype(BF16)
    kcat_s[:, cache_lo:cache_hi, :] = ck_ref[...].astype(BF16)
    vcat_s[:, cache_lo:cache_hi, :] = cv_ref[...].astype(BF16)
    tail = jnp.zeros((nseq, KEY_SPAN - cache_hi, KV_DIM), BF16)
    kcat_s[:, cache_hi:KEY_SPAN, :] = tail
    vcat_s[:, cache_hi:KEY_SPAN, :] = tail

    n_q = N_HEADS * tq
    tt = lax.broadcasted_iota(jnp.int32, (n_q, KEY_SPAN), 0) & (tq - 1)
    ii = lax.broadcasted_iota(jnp.int32, (n_q, KEY_SPAN), 1)
    visible = ((ii < tq) & (ii <= tt)) | ((ii >= cache_lo + tt) & (ii < cache_hi))
    bias = jnp.where(visible, 0.0, NEG).astype(F32)
    cm_scaled = _kv_head_masks(HEAD_DIM ** -0.5)
    cm_plain = _kv_head_masks(1.0)
    q3 = q.reshape(nseq, tq, D_MODEL)
    qbd = jnp.concatenate([q3[:, :, r * KV_DIM:(r + 1) * KV_DIM] * cm_scaled[g]
                           for r in range(Q_REP) for g in range(N_KV_HEADS)], axis=1).astype(BF16)
    s = jnp.einsum("bqc,bkc->bqk", qbd, kcat_s[...], preferred_element_type=F32) + bias
    sink = jnp.concatenate([jnp.full((tq, 1), sink_ref[g * Q_REP + r], F32)
                            for r in range(Q_REP) for g in range(N_KV_HEADS)], axis=0)
    p = _softmax_with_sink(s, sink).astype(BF16)
    o = jnp.einsum("bqk,bkc->bqc", p, vcat_s[...], preferred_element_type=F32)
    slabs = []
    for r in range(Q_REP):
        acc = None
        for g in range(N_KV_HEADS):
            i = r * N_KV_HEADS + g
            part = o[:, i * tq:(i + 1) * tq, :] * cm_plain[g]
            acc = part if acc is None else acc + part
        slabs.append(acc)
    attn = jnp.concatenate(slabs, axis=2).reshape(rows, D_MODEL)
    ao_ref[...] = _dot(attn.astype(BF16), w_ao_ref[...]).reshape(nseq, tq, D_MODEL)


def _sample_rest_kernel(x_ref, mod_ref, ao_ref, n1g_ref, caw_ref, v512_ref, wc_ref, bsp_ref,
                        sta_ref, stb_ref, w_in_ref, w_aout_ref, w_bout_ref, w_cout_ref, w_o_ref,
                        x1_ref, na_ref, nb_ref, vv_ref, glu_s, aact_s, bcat_s):
    nseq, tq, _ = x_ref.shape
    rows = nseq * tq

    x = x_ref[...]
    shift1, scale1, gate1 = mod_ref[:, 0:1, :], mod_ref[:, 1:2, :], mod_ref[:, 2:3, :]
    h = _rms_mod(x, n1g_ref[...], scale1, shift1).reshape(rows, D_MODEL).astype(BF16)
    attn_out = ao_ref[...].reshape(rows, D_MODEL)
    merged = jax.nn.sigmoid(_dot(h, w_in_ref[:, G0:G0 + D_MODEL])) * attn_out

    a_in = _dot(h, w_in_ref[:, A0:A0 + 2 * CONV_W])
    glu = a_in[:, :CONV_W] * jax.nn.sigmoid(a_in[:, CONV_W:])
    hist_a = CONF_CONV_W - 1
    glu_s[...] = glu.reshape(nseq, tq, CONV_W)

    def conv_in(t):
        return sta_ref[t] if t < hist_a else glu_s[:, t - hist_a, :]

    for t in range(tq):
        acc = jnp.broadcast_to(v512_ref[0:1, :], (nseq, CONV_W))
        for tap in range(CONF_CONV_W):
            acc = acc + caw_ref[tap:tap + 1, :] * conv_in(t + tap)
        aact_s[:, t, :] = jax.nn.silu(_layer_norm(acc, v512_ref[1:2, :], v512_ref[2:3, :]))
    for t in range(hist_a):
        na_ref[t] = conv_in(t + tq)
    a_out = _dot(aact_s[...].reshape(rows, CONV_W).astype(BF16), w_aout_ref[...])
    merged = merged + jax.nn.sigmoid(_dot(h, w_in_ref[:, G0 + D_MODEL:G0 + 2 * D_MODEL])) * a_out

    b_in = _dot(h, w_in_ref[:, B0:B0 + 3 * CONV_W])
    hist_b = SC_CONV_W - 1
    base_b = B_PAD - hist_b
    bcat_s[:, base_b:B_PAD, :] = stb_ref[...]
    bcat_s[:, B_PAD:B_PAD + tq, :] = (b_in[:, CONV_W:2 * CONV_W]
                                      * b_in[:, 2 * CONV_W:]).reshape(nseq, tq, CONV_W)
    sc = None
    for t in range(SC_CONV_W):
        term = v512_ref[5 + t:6 + t, :] * bcat_s[:, base_b + t:base_b + t + tq, :]
        sc = term if sc is None else sc + term
    nb_ref[...] = bcat_s[:, base_b + tq:B_PAD + tq, :]
    b_out = _dot((b_in[:, :CONV_W] * sc.reshape(rows, CONV_W)).astype(BF16), w_bout_ref[...])
    merged = merged + jax.nn.sigmoid(_dot(h, w_in_ref[:, G0 + 2 * D_MODEL:G0 + 3 * D_MODEL])) * b_out

    zc = jax.nn.gelu(_dot(h, w_in_ref[:, C0:C0 + 2 * CONV_W]))
    u3 = zc[:, :CONV_W].reshape(nseq, tq, CONV_W)
    vv3 = _layer_norm(zc[:, CONV_W:], v512_ref[3:4, :], v512_ref[4:5, :]).reshape(nseq, tq, CONV_W)
    vv_ref[...] = vv3
    t_idx = lax.broadcasted_iota(jnp.int32, (tq, CONV_W), 0)
    z = jnp.broadcast_to(bsp_ref[...], (nseq, tq, CONV_W))
    for src in range(tq):
        coef = jnp.where(t_idx >= src, wc_ref[src], 0.0)
        z = z + coef * vv3[:, src:src + 1, :]
    c_out = _dot((u3 * z).reshape(rows, CONV_W).astype(BF16), w_cout_ref[...])
    merged = merged + jax.nn.sigmoid(_dot(h, w_in_ref[:, G0 + 3 * D_MODEL:G0 + 4 * D_MODEL])) * c_out

    y = _dot(merged.astype(BF16), w_o_ref[...]).reshape(nseq, tq, D_MODEL)
    x1_ref[...] = x + gate1 * y


def _seq_spec(nseq, r, c):
    return pl.BlockSpec((nseq, r, c), lambda i: (i, 0, 0))


def _sample_attention(x, mod, rope, cache_k, cache_v, lw, layer):
    nbatch, tq, _ = x.shape
    nseq = SAMPLE_ATTN_SEQS
    small = [lw["n1g"], lw["qkg"], lw["segm"]]

    def cache_spec():
        return pl.BlockSpec((None, nseq, WINDOW, KV_DIM), lambda i: (layer, i, 0, 0))

    in_specs = (
        [_seq_spec(nseq, tq, D_MODEL), _seq_spec(nseq, 6, D_MODEL),
         _const_spec(rope.shape), pl.BlockSpec(memory_space=pltpu.SMEM)]
        + [_layer_spec(a, layer) for a in small]
        + [cache_spec(), cache_spec(),
           _layer_spec(lw["wq"], layer), _layer_spec(lw["w_in"], layer, cols=(K0, 2 * KV_DIM)),
           _layer_spec(lw["w_ao"], layer)])
    out_shape = (
        jax.ShapeDtypeStruct((nbatch, tq, D_MODEL), F32),
        jax.ShapeDtypeStruct((nbatch, WINDOW, KV_DIM), F32),
        jax.ShapeDtypeStruct((nbatch, WINDOW, KV_DIM), F32),
    )
    out_specs = (_seq_spec(nseq, tq, D_MODEL), _seq_spec(nseq, WINDOW, KV_DIM),
                 _seq_spec(nseq, WINDOW, KV_DIM))
    return pl.pallas_call(
        _sample_attn_kernel,
        out_shape=out_shape,
        grid=(nbatch // nseq,),
        in_specs=in_specs,
        out_specs=out_specs,
        scratch_shapes=[
            pltpu.VMEM((nseq, KEY_SPAN, KV_DIM), BF16),
            pltpu.VMEM((nseq, KEY_SPAN, KV_DIM), BF16),
        ],
        compiler_params=pltpu.CompilerParams(
            dimension_semantics=("arbitrary",),
            vmem_limit_bytes=VMEM_BYTES - 16 * 1024 * 1024),
        name="sample_attention",
    )(x, mod, rope, lw["sinks"][layer], *small, cache_k, cache_v, lw["wq"], lw["w_in"], lw["w_ao"])


def _sample_rest(x, mod, attn_out, state_a, state_b, lw, layer):
    nbatch, tq, _ = x.shape
    nseq = SAMPLE_SEQS
    small = [lw["n1g"], lw["caw"], lw["v512"], lw["wc8"], lw["bsp8"]]
    weights = [lw["w_in"], lw["w_aout"], lw["w_bout"], lw["w_cout"], lw["w_o"]]
    hist_a = CONF_CONV_W - 1
    in_specs = (
        [_seq_spec(nseq, tq, D_MODEL), _seq_spec(nseq, 6, D_MODEL), _seq_spec(nseq, tq, D_MODEL)]
        + [_layer_spec(a, layer) for a in small]
        + [pl.BlockSpec((None, hist_a, nseq, CONV_W), lambda i: (layer, 0, i, 0)),
           pl.BlockSpec((None, nseq, SC_CONV_W - 1, CONV_W), lambda i: (layer, i, 0, 0))]
        + [_layer_spec(a, layer) for a in weights])
    out_shape = (
        jax.ShapeDtypeStruct((nbatch, tq, D_MODEL), F32),
        jax.ShapeDtypeStruct((hist_a, nbatch, CONV_W), F32),
        jax.ShapeDtypeStruct((nbatch, SC_CONV_W - 1, CONV_W), F32),
        jax.ShapeDtypeStruct((nbatch, tq, CONV_W), F32),
    )
    out_specs = (_seq_spec(nseq, tq, D_MODEL),
                 pl.BlockSpec((hist_a, nseq, CONV_W), lambda i: (0, i, 0)),
                 _seq_spec(nseq, SC_CONV_W - 1, CONV_W), _seq_spec(nseq, tq, CONV_W))
    return pl.pallas_call(
        _sample_rest_kernel,
        out_shape=out_shape,
        grid=(nbatch // nseq,),
        in_specs=in_specs,
        out_specs=out_specs,
        scratch_shapes=[
            pltpu.VMEM((nseq, tq, CONV_W), F32),
            pltpu.VMEM((nseq, tq, CONV_W), F32),
            pltpu.VMEM((nseq, B_PAD + tq, CONV_W), F32),
        ],
        compiler_params=pltpu.CompilerParams(
            dimension_semantics=("arbitrary",),
            vmem_limit_bytes=VMEM_BYTES - 8 * 1024 * 1024),
        name="sample_rest",
    )(x, mod, attn_out, *small, state_a, state_b, *weights)


def _mlp_kernel(x_ref, mod_ref, g_ref, w_up_ref, w_down_ref, o_ref):
    x = x_ref[...]
    if x.ndim == 3:
        shift2, scale2, gate2 = mod_ref[:, 3:4, :], mod_ref[:, 4:5, :], mod_ref[:, 5:6, :]
    else:
        shift2, scale2, gate2 = mod_ref[3:4, :], mod_ref[4:5, :], mod_ref[5:6, :]
    h2 = _rms_mod(x, g_ref[...], scale2, shift2)
    up = _dot(h2.reshape(-1, D_MODEL).astype(BF16), w_up_ref[...])
    act = jnp.square(jnp.maximum(up, 0.0)).astype(BF16)
    y = _dot(act, w_down_ref[...]).reshape(x.shape)
    o_ref[...] = x + gate2 * y


def _mlp(x, mod, lw, layer, per_sequence):
    if per_sequence:
        nbatch, tq, _ = x.shape
        nseq = MLP_TILE // tq
        grid = (nbatch // nseq,)
        x_spec = pl.BlockSpec((nseq, tq, D_MODEL), lambda i: (i, 0, 0))
        mod_spec = pl.BlockSpec((nseq, 6, D_MODEL), lambda i: (i, 0, 0))
    else:
        bsz, seq, _ = x.shape
        grid = (bsz, seq // MLP_TILE)
        x_spec = pl.BlockSpec((None, MLP_TILE, D_MODEL), lambda b, j: (b, j, 0))
        mod_spec = pl.BlockSpec((None, 6, D_MODEL), lambda b, j: (b, 0, 0))
    return pl.pallas_call(
        _mlp_kernel,
        out_shape=jax.ShapeDtypeStruct(x.shape, F32),
        grid=grid,
        in_specs=[x_spec, mod_spec, _layer_spec(lw["n2g"], layer),
                  _layer_spec(lw["w_up"], layer), _layer_spec(lw["w_down"], layer)],
        out_specs=x_spec,
        compiler_params=pltpu.CompilerParams(
            dimension_semantics=("arbitrary",) * len(grid),
            vmem_limit_bytes=VMEM_BYTES - 16 * 1024 * 1024),
        name="mlp_sample" if per_sequence else "mlp_prompt",
    )(x, mod, lw["n2g"], lw["w_up"], lw["w_down"])


def _rope_cos_sin(pos):
    half = ROT_DIM // 2
    freq = (jnp.arange(LANES) % half).astype(F32)
    inv_freq = jnp.power(ROPE_THETA, -freq * 2.0 / ROT_DIM)
    ang = pos.astype(F32)[:, None] * inv_freq[None, :]
    return jnp.stack([jnp.cos(ang), jnp.sin(ang)])


def _rope_lane_tables(cos, sin):
    half = ROT_DIM // 2
    lane = lax.broadcasted_iota(jnp.int32, (1, LANES), 1) & (HEAD_DIM - 1)
    cos_t = jnp.where(lane < ROT_DIM, cos, 1.0)
    sin_a = jnp.where(lane < half, -sin, 0.0)
    sin_b = jnp.where((lane >= half) & (lane < ROT_DIM), sin, 0.0)
    return cos_t, sin_a, sin_b


def _prepare_weights(p):
    depth = p["w_in"].shape[0]
    w_in = p["w_in"]
    wq = w_in[:, :, :D_MODEL].reshape(depth, D_MODEL, N_KV_HEADS, Q_REP, HEAD_DIM)
    wq = wq.transpose(0, 1, 3, 2, 4).reshape(depth, D_MODEL, D_MODEL).astype(BF16)
    w_ao = p["w_attn_o"].reshape(depth, N_KV_HEADS, Q_REP, HEAD_DIM, D_MODEL)
    w_ao = w_ao.transpose(0, 2, 1, 3, 4).reshape(depth, D_MODEL, D_MODEL).astype(BF16)
    ws = p["w_spatial"]
    n_pairs = GMLP_GROUPS // 2
    wpair = ws.reshape(depth, n_pairs, 2, GMLP_CHUNK, GMLP_CHUNK).transpose(0, 1, 3, 2, 4)
    wpair = wpair.reshape(depth, n_pairs, GMLP_CHUNK, 2 * GMLP_CHUNK)
    bsp = jnp.repeat(p["b_spatial"].transpose(0, 2, 1), GMLP_GROUP_DIM, axis=2)
    wc8 = jnp.repeat(ws[:, :, :DEC_SEQ, :DEC_SEQ].transpose(0, 3, 2, 1),
                     GMLP_GROUP_DIM, axis=3)
    lane_head = jnp.arange(LANES) // HEAD_DIM
    segm = jnp.where(lane_head[:, None] == lane_head[None, :], 1.0 / HEAD_DIM, 0.0).astype(BF16)
    caw = jnp.concatenate([p["conv_a_w"], jnp.zeros((depth, 1, CONV_W), F32)], axis=1)
    v512 = jnp.concatenate([p["conv_a_b"][:, None], p["ln_a_g"][:, None], p["ln_a_b"][:, None],
                            p["ln_c_g"][:, None], p["ln_c_b"][:, None], p["conv_b_w"]], axis=1)
    reps = LANES // HEAD_DIM
    return {
        "wq": wq, "w_in": w_in.astype(BF16), "w_ao": w_ao,
        "w_aout": p["w_a_out"].astype(BF16), "w_bout": p["w_b_out"].astype(BF16),
        "w_cout": p["w_c_out"].astype(BF16), "w_o": p["w_o"].astype(BF16),
        "w_up": p["w_up"].astype(BF16), "w_down": p["w_down"].astype(BF16),
        "n1g": p["norm1_g"][:, None], "n2g": p["norm2_g"][:, None],
        "qkg": jnp.stack([jnp.tile(p["q_norm_g"], (1, reps)),
                          jnp.tile(p["k_norm_g"], (1, reps))], axis=1),
        "caw": caw, "v512": v512, "wpair": wpair, "bsp": bsp, "wc8": wc8,
        "bsp8": bsp[:, :DEC_SEQ],
        "segm": jnp.broadcast_to(segm, (depth, LANES, LANES)), "sinks": p["attn_sinks"],
    }


def kernel(x_prompt, x_sample, c_prompt, c_sample, state_conv_a, state_conv_b, cache_k_win, cache_v_win, w_ada, b_ada, norm1_g, norm2_g, w_in, q_norm_g, k_norm_g, attn_sinks, w_attn_o, conv_a_w, conv_a_b, ln_a_g, ln_a_b, w_a_out, conv_b_w, w_b_out, ln_c_g, ln_c_b, w_spatial, b_spatial, w_c_out, w_o, w_up, w_down):
    params = dict(w_in=w_in, w_attn_o=w_attn_o, w_spatial=w_spatial, b_spatial=b_spatial,
                  conv_a_w=conv_a_w, conv_a_b=conv_a_b, ln_a_g=ln_a_g, ln_a_b=ln_a_b,
                  ln_c_g=ln_c_g, ln_c_b=ln_c_b, conv_b_w=conv_b_w, w_a_out=w_a_out,
                  w_b_out=w_b_out, w_c_out=w_c_out, w_o=w_o, w_up=w_up, w_down=w_down,
                  norm1_g=norm1_g, norm2_g=norm2_g, q_norm_g=q_norm_g, k_norm_g=k_norm_g,
                  attn_sinks=attn_sinks)
    depth = w_in.shape[0]
    bp, seq, _ = x_prompt.shape
    bs, tq, _ = x_sample.shape

    c_all = jnp.concatenate([c_prompt, c_sample,
                             jnp.zeros((MOD_ROWS - bp - bs, D_MODEL), F32)], axis=0)
    mod_all = _modulation(c_all, w_ada, b_ada)
    rope_local = _rope_cos_sin(jnp.arange(PROMPT_TILE))
    rope_tile = _rope_cos_sin(jnp.arange(0, seq, PROMPT_TILE)).transpose(1, 0, 2)
    rope_s = jnp.tile(_rope_cos_sin(PAST_LEN + jnp.arange(tq)), (1, SAMPLE_ATTN_SEQS, 1))
    state_a_tm = state_conv_a.transpose(0, 2, 1, 3)
    cache_k = cache_k_win.reshape(depth, bs, WINDOW, KV_DIM)
    cache_v = cache_v_win.reshape(depth, bs, WINDOW, KV_DIM)

    xp, xs = x_prompt, x_sample
    outs = [[] for _ in range(9)]
    lw = _prepare_weights(params)
    for l in range(depth):
        mod_p = mod_all[l, :bp].reshape(bp, 6, D_MODEL)
        mod_s = mod_all[l, bp:bp + bs].reshape(bs, 6, D_MODEL)
        xp, pa, pb, pk, pv = _mixer_prompt(xp, mod_p, rope_local, rope_tile, lw, l)
        xp = _mlp(xp, mod_p, lw, l, per_sequence=False)
        s_attn, sk, sv = _sample_attention(xs, mod_s, rope_s, cache_k, cache_v, lw, l)
        xs, sa, sb, sg = _sample_rest(xs, mod_s, s_attn, state_a_tm, state_conv_b, lw, l)
        sa = sa.transpose(1, 0, 2)
        xs = _mlp(xs, mod_s, lw, l, per_sequence=True)
        kv_shape_p = (bp, WINDOW, N_KV_HEADS, HEAD_DIM)
        kv_shape_s = (bs, WINDOW, N_KV_HEADS, HEAD_DIM)
        for lst, val in zip(outs, (pa, sa, pb, sb, pk.reshape(kv_shape_p), sk.reshape(kv_shape_s),
                                   pv.reshape(kv_shape_p), sv.reshape(kv_shape_s), sg)):
            lst.append(val)
    return (xp, xs) + tuple(jnp.stack(o) for o in outs)
```

```python
import jax
import jax.numpy as jnp
from jax import lax
from jax.experimental import pallas as pl
from jax.experimental.pallas import tpu as pltpu

F32 = jnp.float32
BF16 = jnp.bfloat16

D_MODEL = 1024
HEAD_DIM = 64
N_HEADS = 16
N_KV_HEADS = 4
Q_REP = 4
KV_DIM = N_KV_HEADS * HEAD_DIM
ROT_DIM = 16
ROPE_THETA = 500000.0
WINDOW = 128
ATTN_BLOCK = 128
KEY_SPAN = 2 * ATTN_BLOCK
CONV_W = 512
CONF_CONV_W = 31
SC_CONV_W = 3
GMLP_CHUNK = 128
GMLP_GROUPS = 8
GMLP_GROUP_DIM = 64
D_FF = 4 * D_MODEL
EPS = 1e-6
PAST_LEN = 16384
DEC_SEQ = 8
NEG = -1e30

K0 = D_MODEL
V0 = K0 + KV_DIM
A0 = V0 + KV_DIM
B0 = A0 + 2 * CONV_W
C0 = B0 + 3 * CONV_W
G0 = C0 + 2 * CONV_W

LANES = 128
SUBLANES = 8
VMEM_BYTES = 64 * 1024 * 1024

PROMPT_TILE = 512
MLP_TILE = 1024
SAMPLE_SEQS = 32
SAMPLE_ATTN_SEQS = 16
CONV_ROWS = 64
A_PAD = 32
B_PAD = 8
MOD_ROWS = 136


def _dot(a, b):
    return jnp.dot(a, b, preferred_element_type=F32)


def _rms_mod(x, g, scale, shift):
    ms = jnp.mean(x * x, axis=-1, keepdims=True)
    y = x * lax.rsqrt(ms + EPS)
    return (y * g) * (1.0 + scale) + shift


def _layer_norm(x, g, b):
    mu = jnp.mean(x, axis=-1, keepdims=True)
    xc = x - mu
    var = jnp.mean(xc * xc, axis=-1, keepdims=True)
    return xc * lax.rsqrt(var + EPS) * g + b


def _head_norm_rope(xc, gain, seg_mean, cos_t, sin_a, sin_b):
    ms = _dot((xc * xc).astype(BF16), seg_mean)
    y = xc * lax.rsqrt(ms + EPS) * gain
    return (y * cos_t + pltpu.roll(y, LANES - ROT_DIM // 2, 1) * sin_a
            + pltpu.roll(y, ROT_DIM // 2, 1) * sin_b)


def _qkv_heads(q_raw, kv_raw, qkg_ref, seg_mean, cos_t, sin_a, sin_b):
    def chunk(src, c, gain):
        return _head_norm_rope(src[:, c * LANES:(c + 1) * LANES], gain, seg_mean,
                               cos_t, sin_a, sin_b)
    q = jnp.concatenate([chunk(q_raw, c, qkg_ref[0:1, :]) for c in range(D_MODEL // LANES)],
                        axis=1)
    k = jnp.concatenate([chunk(kv_raw, c, qkg_ref[1:2, :])
                         for c in range(KV_DIM // LANES)], axis=1)
    return q, k, kv_raw[:, KV_DIM:2 * KV_DIM]


def _kv_head_masks(scale):
    lane = lax.broadcasted_iota(jnp.int32, (1, KV_DIM), 1)
    return [jnp.where((lane >= g * HEAD_DIM) & (lane < (g + 1) * HEAD_DIM), scale, 0.0).astype(F32)
            for g in range(N_KV_HEADS)]


def _softmax_with_sink(s, sink):
    m = jnp.maximum(jnp.max(s, axis=-1, keepdims=True), sink)
    e = jnp.exp(s - m)
    den = jnp.sum(e, axis=-1, keepdims=True) + jnp.exp(sink - m)
    return e * (1.0 / den)


def _causal_conv_rows(hist_ref, row0, rows, w_ref, n_taps, base, bias):
    out = jnp.broadcast_to(bias, (rows, hist_ref.shape[1]))
    for phase in range(SUBLANES):
        span = rows if phase == 0 else rows + SUBLANES
        part = None
        for tap in range(n_taps):
            offset = base + tap
            if offset % SUBLANES != phase:
                continue
            lo = row0 + offset - phase
            term = w_ref[tap:tap + 1, :] * hist_ref[lo:lo + span, :]
            part = term if part is None else part + term
        if part is not None:
            out = out + part[phase:phase + rows]
    return out


def _mod_kernel(c_ref, w_ref, b_ref, o_ref):
    c = c_ref[...]
    s = (c * jax.nn.sigmoid(c)).astype(BF16)
    o_ref[...] = _dot(s, w_ref[...].astype(BF16)) + b_ref[...]


def _modulation(c_all, w_ada, b_ada):
    depth = w_ada.shape[0]
    n_chunks = w_ada.shape[2] // D_MODEL
    return pl.pallas_call(
        _mod_kernel,
        out_shape=jax.ShapeDtypeStruct((depth, MOD_ROWS, n_chunks * D_MODEL), F32),
        grid=(depth, n_chunks),
        in_specs=[
            pl.BlockSpec((MOD_ROWS, D_MODEL), lambda l, j: (0, 0)),
            pl.BlockSpec((None, D_MODEL, D_MODEL), lambda l, j: (l, 0, j)),
            pl.BlockSpec((None, 1, D_MODEL), lambda l, j: (l, 0, j)),
        ],
        out_specs=pl.BlockSpec((None, MOD_ROWS, D_MODEL), lambda l, j: (l, 0, j)),
        compiler_params=pltpu.CompilerParams(
            dimension_semantics=("arbitrary", "arbitrary"),
            vmem_limit_bytes=32 * 1024 * 1024),
        name="adaln_modulation",
    )(c_all, w_ada, b_ada.reshape(depth, 1, -1))


def _per_head_copies(win, head_masks):
    return jnp.concatenate([jnp.where(m, win, jnp.zeros_like(win)) for m in head_masks], axis=0)


def _block_scores(q_blk, k_heads):
    qs = (q_blk * HEAD_DIM ** -0.5).astype(BF16)
    q_rows = jnp.concatenate([qs[:, r * KV_DIM:(r + 1) * KV_DIM] for r in range(Q_REP)], axis=0)
    return lax.dot_general(q_rows, k_heads, (((1,), (1,)), ((), ())), preferred_element_type=F32)


def _block_probs(s, bias, sink_ref):
    tq, keys = bias.shape
    rows = []
    for r in range(Q_REP):
        cols = []
        for g in range(N_KV_HEADS):
            sp = s[r * tq:(r + 1) * tq, g * keys:(g + 1) * keys] + bias
            cols.append(_softmax_with_sink(sp, sink_ref[g * Q_REP + r]).astype(BF16))
        rows.append(jnp.concatenate(cols, axis=1))
    return jnp.concatenate(rows, axis=0)


def _block_output(p, v_heads):
    tq = p.shape[0] // Q_REP
    o = _dot(p, v_heads)
    return jnp.concatenate([o[r * tq:(r + 1) * tq] for r in range(Q_REP)], axis=1)


def _mixer_prompt_kernel(x_ref, mod_ref, rope_ref, rope0_ref, sink_ref, n1g_ref, qkg_ref, caw_ref, v512_ref,
                         wpair_ref, bsp_ref, segm_ref,
                         wq_ref, w_in_ref, w_ao_ref, w_aout_ref, w_bout_ref, w_cout_ref, w_o_ref,
                         x1_ref, ca_ref, cb_ref, ko_ref, vo_ref,
                         kprev_s, vprev_s, aprev_s, bprev_s, acat_s, bcat_s, bias_s):
    tile = x_ref.shape[0]
    j = pl.program_id(1)
    last = j == pl.num_programs(1) - 1
    rd = j & 1
    wr = 1 - rd

    @pl.when(j == 0)
    def _():
        kprev_s[0] = jnp.zeros((ATTN_BLOCK, KV_DIM), BF16)
        vprev_s[0] = jnp.zeros((ATTN_BLOCK, KV_DIM), BF16)
        aprev_s[0] = jnp.zeros((A_PAD, CONV_W), F32)
        bprev_s[0] = jnp.zeros((B_PAD, CONV_W), F32)

    x = x_ref[...]
    shift1, scale1, gate1 = mod_ref[0:1, :], mod_ref[1:2, :], mod_ref[2:3, :]
    h = _rms_mod(x, n1g_ref[...], scale1, shift1).astype(BF16)

    def gate_logits(branch):
        lo = G0 + branch * D_MODEL
        return _dot(h, w_in_ref[:, lo:lo + D_MODEL])

    q_raw = _dot(h, wq_ref[...])
    kv_raw = _dot(h, w_in_ref[:, K0:A0])
    a_in = _dot(h, w_in_ref[:, A0:A0 + 2 * CONV_W])
    cos_l, sin_l = rope_ref[0], rope_ref[1]
    cos_0, sin_0 = rope0_ref[0:1, :], rope0_ref[1:2, :]
    cos_t, sin_a, sin_b = _rope_lane_tables(cos_0 * cos_l - sin_0 * sin_l,
                                            sin_0 * cos_l + cos_0 * sin_l)
    q, k, v = _qkv_heads(q_raw, kv_raw, qkg_ref, segm_ref[...], cos_t, sin_a, sin_b)
    kcat = jnp.concatenate([kprev_s[rd], k.astype(BF16)], axis=0)
    vcat = jnp.concatenate([vprev_s[rd], v.astype(BF16)], axis=0)

    @pl.when(last)
    def _():
        ko_ref[...] = k[tile - WINDOW:tile]
        vo_ref[...] = v[tile - WINDOW:tile]

    tt = lax.broadcasted_iota(jnp.int32, (ATTN_BLOCK, KEY_SPAN), 0)
    jj = lax.broadcasted_iota(jnp.int32, (ATTN_BLOCK, KEY_SPAN), 1)
    dist = jj - tt
    bias_rest = jnp.where((dist >= 0) & (dist <= WINDOW), 0.0, NEG).astype(F32)

    @pl.when(j == 0)
    def _():
        bias_s[...] = jnp.where(jj < ATTN_BLOCK, NEG, bias_rest)

    @pl.when(j == 1)
    def _():
        bias_s[...] = bias_rest

    bias_first = bias_s[...]
    lane = lax.broadcasted_iota(jnp.int32, (1, KV_DIM), 1)
    head_masks = [(lane >= g * HEAD_DIM) & (lane < (g + 1) * HEAD_DIM) for g in range(N_KV_HEADS)]
    kprev_s[wr] = kcat[tile:tile + ATTN_BLOCK]
    vprev_s[wr] = vcat[tile:tile + ATTN_BLOCK]
    n_blocks = tile // ATTN_BLOCK

    def scores(i):
        lo = i * ATTN_BLOCK
        return _block_scores(q[lo:lo + ATTN_BLOCK],
                             _per_head_copies(kcat[lo:lo + KEY_SPAN], head_masks))

    glu = a_in[:, :CONV_W] * jax.nn.sigmoid(a_in[:, CONV_W:])
    acat_s[0:A_PAD, :] = aprev_s[rd]
    acat_s[A_PAD:A_PAD + tile, :] = glu
    aprev_s[wr] = glu[tile - A_PAD:tile]
    base = A_PAD - (CONF_CONV_W - 1)

    @pl.when(last)
    def _():
        ca_ref[...] = acat_s[base + tile:A_PAD + tile, :]

    n_chunks = tile // CONV_ROWS
    conv_rows = []

    def conv_chunks(lo, hi):
        for c in range(lo, hi):
            conv_rows.append(_causal_conv_rows(acat_s, c * CONV_ROWS, CONV_ROWS, caw_ref,
                                               CONF_CONV_W, base, v512_ref[0:1, :]))

    fillers = [lambda: _dot(h, w_in_ref[:, B0:B0 + 3 * CONV_W]),
               lambda: _dot(h, w_in_ref[:, C0:C0 + 2 * CONV_W])]
    filled = []
    blocks = []
    p = None
    for i in range(n_blocks + 1):
        if i > 0:
            lo = (i - 1) * ATTN_BLOCK
            blocks.append(_block_output(p, _per_head_copies(vcat[lo:lo + KEY_SPAN], head_masks)))
        if i < n_blocks:
            s_blk = scores(i)
        if i < len(fillers):
            filled.append(fillers[i]())
        if i == n_blocks:
            for f in fillers[len(filled):]:
                filled.append(f())
            g0 = gate_logits(0)
            g1 = gate_logits(1)
        if i < n_blocks:
            p = _block_probs(s_blk, bias_first if i == 0 else bias_rest, sink_ref)
            conv_chunks(i, i + 1)
    b_in, c_in = filled
    conv_chunks(n_blocks, n_chunks)
    attn = jnp.concatenate(blocks, axis=0).astype(BF16)
    a_conv = jnp.concatenate(conv_rows, axis=0)
    a_act = jax.nn.silu(_layer_norm(a_conv, v512_ref[1:2, :], v512_ref[2:3, :])).astype(BF16)
    attn_out = _dot(attn, w_ao_ref[...])
    a_out = _dot(a_act, w_aout_ref[...])
    g2 = gate_logits(2)

    ch = b_in[:, CONV_W:2 * CONV_W] * b_in[:, 2 * CONV_W:]
    bcat_s[0:B_PAD, :] = bprev_s[rd]
    bcat_s[B_PAD:B_PAD + tile, :] = ch
    bprev_s[wr] = ch[tile - B_PAD:tile]
    base_b = B_PAD - (SC_CONV_W - 1)
    sc = None
    for t in range(SC_CONV_W):
        term = v512_ref[5 + t:6 + t, :] * bcat_s[base_b + t:base_b + t + tile, :]
        sc = term if sc is None else sc + term

    @pl.when(last)
    def _():
        cb_ref[...] = bcat_s[base_b + tile:B_PAD + tile, :]

    merged = jax.nn.sigmoid(g0) * attn_out
    merged = merged + jax.nn.sigmoid(g1) * a_out

    zc = jax.nn.gelu(c_in)
    u = zc[:, :CONV_W]
    vv = _layer_norm(zc[:, CONV_W:], v512_ref[3:4, :], v512_ref[4:5, :])
    row = lax.broadcasted_iota(jnp.int32, (GMLP_CHUNK, 2 * GMLP_CHUNK), 0)
    col = lax.broadcasted_iota(jnp.int32, (GMLP_CHUNK, 2 * GMLP_CHUNK), 1)
    causal = (col & (GMLP_CHUNK - 1)) <= row
    n_pairs = GMLP_GROUPS // 2
    wpair = [jnp.where(causal, wpair_ref[p], 0.0).astype(BF16) for p in range(n_pairs)]
    low = lax.broadcasted_iota(jnp.int32, (1, LANES), 1) < GMLP_GROUP_DIM
    rhs_blocks = []
    for c in range(tile // GMLP_CHUNK):
        vch = vv[c * GMLP_CHUNK:(c + 1) * GMLP_CHUNK]
        for pair in range(n_pairs):
            blk = vch[:, pair * LANES:(pair + 1) * LANES]
            rhs = jnp.concatenate([jnp.where(low, blk, 0.0), jnp.where(low, 0.0, blk)], axis=0)
            rhs_blocks.append(rhs.astype(BF16))
    b_act = (b_in[:, :CONV_W] * sc).astype(BF16)
    b_out = _dot(b_act, w_bout_ref[...])
    zs = [_dot(wpair[i % n_pairs], rhs) for i, rhs in enumerate(rhs_blocks)]
    g3 = gate_logits(3)
    gated = []
    for c in range(tile // GMLP_CHUNK):
        z = jnp.concatenate(zs[c * n_pairs:(c + 1) * n_pairs], axis=1) + bsp_ref[...]
        gated.append(u[c * GMLP_CHUNK:(c + 1) * GMLP_CHUNK] * z)
    gated = jnp.concatenate(gated, axis=0).astype(BF16)
    merged = merged + jax.nn.sigmoid(g2) * b_out
    c_out = _dot(gated, w_cout_ref[...])
    merged = (merged + jax.nn.sigmoid(g3) * c_out).astype(BF16)
    x1_ref[...] = x + gate1 * _dot(merged, w_o_ref[...])


def _const_spec(shape):
    zeros = (0,) * len(shape)
    return pl.BlockSpec(shape, lambda *_: zeros, pipeline_mode=pl.Buffered(1))


def _layer_spec(stacked, layer, cols=None):
    if cols is None:
        block, col_block = stacked.shape[1:], 0
    else:
        start, width = cols
        assert start % width == 0
        block, col_block = stacked.shape[1:-1] + (width,), start // width
    index = (layer,) + (0,) * (len(block) - 1) + (col_block,)
    return pl.BlockSpec((None,) + block, lambda *_: index, pipeline_mode=pl.Buffered(1))


def _mixer_prompt(x, mod, rope_local, rope_tile, lw, layer):
    bsz, seq, _ = x.shape
    tile = PROMPT_TILE
    small = [lw["n1g"], lw["qkg"], lw["caw"], lw["v512"], lw["wpair"], lw["bsp"], lw["segm"]]
    weights = [lw["wq"], lw["w_in"], lw["w_ao"], lw["w_aout"], lw["w_bout"], lw["w_cout"],
               lw["w_o"]]
    def per_sequence(r, c):
        return pl.BlockSpec((None, r, c), lambda b, j: (b, 0, 0))

    in_specs = (
        [pl.BlockSpec((None, tile, D_MODEL), lambda b, j: (b, j, 0)),
         per_sequence(6, D_MODEL),
         _const_spec(rope_local.shape),
         pl.BlockSpec((None, 2, LANES), lambda b, j: (j, 0, 0)),
         pl.BlockSpec(memory_space=pltpu.SMEM)]
        + [_layer_spec(a, layer) for a in small]
        + [_layer_spec(a, layer) for a in weights])
    out_shape = (
        jax.ShapeDtypeStruct((bsz, seq, D_MODEL), F32),
        jax.ShapeDtypeStruct((bsz, CONF_CONV_W - 1, CONV_W), F32),
        jax.ShapeDtypeStruct((bsz, SC_CONV_W - 1, CONV_W), F32),
        jax.ShapeDtypeStruct((bsz, WINDOW, KV_DIM), F32),
        jax.ShapeDtypeStruct((bsz, WINDOW, KV_DIM), F32),
    )
    out_specs = (
        pl.BlockSpec((None, tile, D_MODEL), lambda b, j: (b, j, 0)),
        per_sequence(CONF_CONV_W - 1, CONV_W), per_sequence(SC_CONV_W - 1, CONV_W),
        per_sequence(WINDOW, KV_DIM), per_sequence(WINDOW, KV_DIM),
    )
    return pl.pallas_call(
        _mixer_prompt_kernel,
        out_shape=out_shape,
        grid=(bsz, seq // tile),
        in_specs=in_specs,
        out_specs=out_specs,
        scratch_shapes=[
            pltpu.VMEM((2, ATTN_BLOCK, KV_DIM), BF16),
            pltpu.VMEM((2, ATTN_BLOCK, KV_DIM), BF16),
            pltpu.VMEM((2, A_PAD, CONV_W), F32),
            pltpu.VMEM((2, B_PAD, CONV_W), F32),
            pltpu.VMEM((A_PAD + tile, CONV_W), F32),
            pltpu.VMEM((B_PAD + tile, CONV_W), F32),
            pltpu.VMEM((ATTN_BLOCK, KEY_SPAN), F32),
        ],
        compiler_params=pltpu.CompilerParams(
            dimension_semantics=("arbitrary", "arbitrary"),
            vmem_limit_bytes=VMEM_BYTES - 4 * 1024 * 1024),
        name="mixer_prompt",
    )(x, mod, rope_local, rope_tile, lw["sinks"][layer], *small, *weights)


SAMPLE_NEW_ROWS = 16


def _sample_attn_kernel(x_ref, mod_ref, rope_ref, sink_ref, n1g_ref, qkg_ref, segm_ref,
                        ck_ref, cv_ref, wq_ref, wkv_ref, w_ao_ref,
                        ao_ref, nk_ref, nv_ref, kcat_s, vcat_s):
    nseq, tq, _ = x_ref.shape
    rows = nseq * tq
    cache_lo = SAMPLE_NEW_ROWS
    cache_hi = cache_lo + WINDOW

    x = x_ref[...]
    shift1, scale1 = mod_ref[:, 0:1, :], mod_ref[:, 1:2, :]
    h = _rms_mod(x, n1g_ref[...], scale1, shift1).reshape(rows, D_MODEL).astype(BF16)

    q, k, v = _qkv_heads(_dot(h, wq_ref[...]), _dot(h, wkv_ref[...]), qkg_ref, segm_ref[...],
                         *_rope_lane_tables(rope_ref[0], rope_ref[1]))
    k3 = k.reshape(nseq, tq, KV_DIM)
    v3 = v.reshape(nseq, tq, KV_DIM)
    keep = WINDOW - tq
    nk_ref[:, 0:keep, :] = ck_ref[:, tq:WINDOW, :]
    nk_ref[:, keep:WINDOW, :] = k3
    nv_ref[:, 0:keep, :] = cv_ref[:, tq:WINDOW, :]
    nv_ref[:, keep:WINDOW, :] = v3
    pad = jnp.zeros((nseq, SAMPLE_NEW_ROWS - tq, KV_DIM), F32)
    kcat_s[:, 0:cache_lo, :] = jnp.concatenate([k3, pad], axis=1).astype(BF16)
    vcat_s[:, 0:cache_lo, :] = jnp.concatenate([v3, pad], axis=1).astype(BF16)
    kcat_s[:, cache_lo:cache_hi, :] = ck_ref[...].astype(BF16)
    vcat_s[:, cache_lo:cache_hi, :] = cv_ref[...].astype(BF16)
    tail = jnp.zeros((nseq, KEY_SPAN - cache_hi, KV_DIM), BF16)
    kcat_s[:, cache_hi:KEY_SPAN, :] = tail
    vcat_s[:, cache_hi:KEY_SPAN, :] = tail

    n_q = N_HEADS * tq
    tt = lax.broadcasted_iota(jnp.int32, (n_q, KEY_SPAN), 0) & (tq - 1)
    ii = lax.broadcasted_iota(jnp.int32, (n_q, KEY_SPAN), 1)
    visible = ((ii < tq) & (ii <= tt)) | ((ii >= cache_lo + tt) & (ii < cache_hi))
    bias = jnp.where(visible, 0.0, NEG).astype(F32)
    cm_scaled = _kv_head_masks(HEAD_DIM ** -0.5)
    cm_plain = _kv_head_masks(1.0)
    q3 = q.reshape(nseq, tq, D_MODEL)
    qbd = jnp.concatenate([q3[:, :, r * KV_DIM:(r + 1) * KV_DIM] * cm_scaled[g]
                           for r in range(Q_REP) for g in range(N_KV_HEADS)], axis=1).astype(BF16)
    s = jnp.einsum("bqc,bkc->bqk", qbd, kcat_s[...], preferred_element_type=F32) + bias
    sink = jnp.concatenate([jnp.full((tq, 1), sink_ref[g * Q_REP + r], F32)
                            for r in range(Q_REP) for g in range(N_KV_HEADS)], axis=0)
    p = _softmax_with_sink(s, sink).astype(BF16)
    o = jnp.einsum("bqk,bkc->bqc", p, vcat_s[...], preferred_element_type=F32)
    slabs = []
    for r in range(Q_REP):
        acc = None
        for g in range(N_KV_HEADS):
            i = r * N_KV_HEADS + g
            part = o[:, i * tq:(i + 1) * tq, :] * cm_plain[g]
            acc = part if acc is None else acc + part
        slabs.append(acc)
    attn = jnp.concatenate(slabs, axis=2).reshape(rows, D_MODEL)
    ao_ref[...] = _dot(attn.astype(BF16), w_ao_ref[...]).reshape(nseq, tq, D_MODEL)


def _sample_rest_kernel(x_ref, mod_ref, ao_ref, n1g_ref, caw_ref, v512_ref, wc_ref, bsp_ref,
                        sta_ref, stb_ref, w_in_ref, w_aout_ref, w_bout_ref, w_cout_ref, w_o_ref,
                        x1_ref, na_ref, nb_ref, vv_ref, glu_s, aact_s, bcat_s):
    nseq, tq, _ = x_ref.shape
    rows = nseq * tq

    x = x_ref[...]
    shift1, scale1, gate1 = mod_ref[:, 0:1, :], mod_ref[:, 1:2, :], mod_ref[:, 2:3, :]
    h = _rms_mod(x, n1g_ref[...], scale1, shift1).reshape(rows, D_MODEL).astype(BF16)
    attn_out = ao_ref[...].reshape(rows, D_MODEL)
    merged = jax.nn.sigmoid(_dot(h, w_in_ref[:, G0:G0 + D_MODEL])) * attn_out

    a_in = _dot(h, w_in_ref[:, A0:A0 + 2 * CONV_W])
    glu = a_in[:, :CONV_W] * jax.nn.sigmoid(a_in[:, CONV_W:])
    hist_a = CONF_CONV_W - 1
    glu_s[...] = glu.reshape(nseq, tq, CONV_W)

    def conv_in(t):
        return sta_ref[t] if t < hist_a else glu_s[:, t - hist_a, :]

    for t in range(tq):
        acc = jnp.broadcast_to(v512_ref[0:1, :], (nseq, CONV_W))
        for tap in range(CONF_CONV_W):
            acc = acc + caw_ref[tap:tap + 1, :] * conv_in(t + tap)
        aact_s[:, t, :] = jax.nn.silu(_layer_norm(acc, v512_ref[1:2, :], v512_ref[2:3, :]))
    for t in range(hist_a):
        na_ref[t] = conv_in(t + tq)
    a_out = _dot(aact_s[...].reshape(rows, CONV_W).astype(BF16), w_aout_ref[...])
    merged = merged + jax.nn.sigmoid(_dot(h, w_in_ref[:, G0 + D_MODEL:G0 + 2 * D_MODEL])) * a_out

    b_in = _dot(h, w_in_ref[:, B0:B0 + 3 * CONV_W])
    hist_b = SC_CONV_W - 1
    base_b = B_PAD - hist_b
    bcat_s[:, base_b:B_PAD, :] = stb_ref[...]
    bcat_s[:, B_PAD:B_PAD + tq, :] = (b_in[:, CONV_W:2 * CONV_W]
                                      * b_in[:, 2 * CONV_W:]).reshape(nseq, tq, CONV_W)
    sc = None
    for t in range(SC_CONV_W):
        term = v512_ref[5 + t:6 + t, :] * bcat_s[:, base_b + t:base_b + t + tq, :]
        sc = term if sc is None else sc + term
    nb_ref[...] = bcat_s[:, base_b + tq:B_PAD + tq, :]
    b_out = _dot((b_in[:, :CONV_W] * sc.reshape(rows, CONV_W)).astype(BF16), w_bout_ref[...])
    merged = merged + jax.nn.sigmoid(_dot(h, w_in_ref[:, G0 + 2 * D_MODEL:G0 + 3 * D_MODEL])) * b_out

    zc = jax.nn.gelu(_dot(h, w_in_ref[:, C0:C0 + 2 * CONV_W]))
    u3 = zc[:, :CONV_W].reshape(nseq, tq, CONV_W)
    vv3 = _layer_norm(zc[:, CONV_W:], v512_ref[3:4, :], v512_ref[4:5, :]).reshape(nseq, tq, CONV_W)
    vv_ref[...] = vv3
    t_idx = lax.broadcasted_iota(jnp.int32, (tq, CONV_W), 0)
    z = jnp.broadcast_to(bsp_ref[...], (nseq, tq, CONV_W))
    for src in range(tq):
        coef = jnp.where(t_idx >= src, wc_ref[src], 0.0)
        z = z + coef * vv3[:, src:src + 1, :]
    c_out = _dot((u3 * z).reshape(rows, CONV_W).astype(BF16), w_cout_ref[...])
    merged = merged + jax.nn.sigmoid(_dot(h, w_in_ref[:, G0 + 3 * D_MODEL:G0 + 4 * D_MODEL])) * c_out

    y = _dot(merged.astype(BF16), w_o_ref[...]).reshape(nseq, tq, D_MODEL)
    x1_ref[...] = x + gate1 * y


def _seq_spec(nseq, r, c):
    return pl.BlockSpec((nseq, r, c), lambda i: (i, 0, 0))


def _sample_attention(x, mod, rope, cache_k, cache_v, lw, layer):
    nbatch, tq, _ = x.shape
    nseq = SAMPLE_ATTN_SEQS
    small = [lw["n1g"], lw["qkg"], lw["segm"]]

    def cache_spec():
        return pl.BlockSpec((None, nseq, WINDOW, KV_DIM), lambda i: (layer, i, 0, 0))

    in_specs = (
        [_seq_spec(nseq, tq, D_MODEL), _seq_spec(nseq, 6, D_MODEL),
         _const_spec(rope.shape), pl.BlockSpec(memory_space=pltpu.SMEM)]
        + [_layer_spec(a, layer) for a in small]
        + [cache_spec(), cache_spec(),
           _layer_spec(lw["wq"], layer), _layer_spec(lw["w_in"], layer, cols=(K0, 2 * KV_DIM)),
           _layer_spec(lw["w_ao"], layer)])
    out_shape = (
        jax.ShapeDtypeStruct((nbatch, tq, D_MODEL), F32),
        jax.ShapeDtypeStruct((nbatch, WINDOW, KV_DIM), F32),
        jax.ShapeDtypeStruct((nbatch, WINDOW, KV_DIM), F32),
    )
    out_specs = (_seq_spec(nseq, tq, D_MODEL), _seq_spec(nseq, WINDOW, KV_DIM),
                 _seq_spec(nseq, WINDOW, KV_DIM))
    return pl.pallas_call(
        _sample_attn_kernel,
        out_shape=out_shape,
        grid=(nbatch // nseq,),
        in_specs=in_specs,
        out_specs=out_specs,
        scratch_shapes=[
            pltpu.VMEM((nseq, KEY_SPAN, KV_DIM), BF16),
            pltpu.VMEM((nseq, KEY_SPAN, KV_DIM), BF16),
        ],
        compiler_params=pltpu.CompilerParams(
            dimension_semantics=("arbitrary",),
            vmem_limit_bytes=VMEM_BYTES - 16 * 1024 * 1024),
        name="sample_attention",
    )(x, mod, rope, lw["sinks"][layer], *small, cache_k, cache_v, lw["wq"], lw["w_in"], lw["w_ao"])


def _sample_rest(x, mod, attn_out, state_a, state_b, lw, layer):
    nbatch, tq, _ = x.shape
    nseq = SAMPLE_SEQS
    small = [lw["n1g"], lw["caw"], lw["v512"], lw["wc8"], lw["bsp8"]]
    weights = [lw["w_in"], lw["w_aout"], lw["w_bout"], lw["w_cout"], lw["w_o"]]
    hist_a = CONF_CONV_W - 1
    in_specs = (
        [_seq_spec(nseq, tq, D_MODEL), _seq_spec(nseq, 6, D_MODEL), _seq_spec(nseq, tq, D_MODEL)]
        + [_layer_spec(a, layer) for a in small]
        + [pl.BlockSpec((None, hist_a, nseq, CONV_W), lambda i: (layer, 0, i, 0)),
           pl.BlockSpec((None, nseq, SC_CONV_W - 1, CONV_W), lambda i: (layer, i, 0, 0))]
        + [_layer_spec(a, layer) for a in weights])
    out_shape = (
        jax.ShapeDtypeStruct((nbatch, tq, D_MODEL), F32),
        jax.ShapeDtypeStruct((hist_a, nbatch, CONV_W), F32),
        jax.ShapeDtypeStruct((nbatch, SC_CONV_W - 1, CONV_W), F32),
        jax.ShapeDtypeStruct((nbatch, tq, CONV_W), F32),
    )
    out_specs = (_seq_spec(nseq, tq, D_MODEL),
                 pl.BlockSpec((hist_a, nseq, CONV_W), lambda i: (0, i, 0)),
                 _seq_spec(nseq, SC_CONV_W - 1, CONV_W), _seq_spec(nseq, tq, CONV_W))
    return pl.pallas_call(
        _sample_rest_kernel,
        out_shape=out_shape,
        grid=(nbatch // nseq,),
        in_specs=in_specs,
        out_specs=out_specs,
        scratch_shapes=[
            pltpu.VMEM((nseq, tq, CONV_W), F32),
            pltpu.VMEM((nseq, tq, CONV_W), F32),
            pltpu.VMEM((nseq, B_PAD + tq, CONV_W), F32),
        ],
        compiler_params=pltpu.CompilerParams(
            dimension_semantics=("arbitrary",),
            vmem_limit_bytes=VMEM_BYTES - 8 * 1024 * 1024),
        name="sample_rest",
    )(x, mod, attn_out, *small, state_a, state_b, *weights)


def _mlp_kernel(x_ref, mod_ref, g_ref, w_up_ref, w_down_ref, o_ref):
    x = x_ref[...]
    if x.ndim == 3:
        shift2, scale2, gate2 = mod_ref[:, 3:4, :], mod_ref[:, 4:5, :], mod_ref[:, 5:6, :]
    else:
        shift2, scale2, gate2 = mod_ref[3:4, :], mod_ref[4:5, :], mod_ref[5:6, :]
    h2 = _rms_mod(x, g_ref[...], scale2, shift2)
    up = _dot(h2.reshape(-1, D_MODEL).astype(BF16), w_up_ref[...])
    act = jnp.square(jnp.maximum(up, 0.0)).astype(BF16)
    y = _dot(act, w_down_ref[...]).reshape(x.shape)
    o_ref[...] = x + gate2 * y


def _mlp(x, mod, lw, layer, per_sequence):
    if per_sequence:
        nbatch, tq, _ = x.shape
        nseq = MLP_TILE // tq
        grid = (nbatch // nseq,)
        x_spec = pl.BlockSpec((nseq, tq, D_MODEL), lambda i: (i, 0, 0))
        mod_spec = pl.BlockSpec((nseq, 6, D_MODEL), lambda i: (i, 0, 0))
    else:
        bsz, seq, _ = x.shape
        grid = (bsz, seq // MLP_TILE)
        x_spec = pl.BlockSpec((None, MLP_TILE, D_MODEL), lambda b, j: (b, j, 0))
        mod_spec = pl.BlockSpec((None, 6, D_MODEL), lambda b, j: (b, 0, 0))
    return pl.pallas_call(
        _mlp_kernel,
        out_shape=jax.ShapeDtypeStruct(x.shape, F32),
        grid=grid,
        in_specs=[x_spec, mod_spec, _layer_spec(lw["n2g"], layer),
                  _layer_spec(lw["w_up"], layer), _layer_spec(lw["w_down"], layer)],
        out_specs=x_spec,
        compiler_params=pltpu.CompilerParams(
            dimension_semantics=("arbitrary",) * len(grid),
            vmem_limit_bytes=VMEM_BYTES - 4 * 1024 * 1024),
        name="mlp_sample" if per_sequence else "mlp_prompt",
    )(x, mod, lw["n2g"], lw["w_up"], lw["w_down"])


def _rope_cos_sin(pos):
    half = ROT_DIM // 2
    freq = (jnp.arange(LANES) % half).astype(F32)
    inv_freq = jnp.power(ROPE_THETA, -freq * 2.0 / ROT_DIM)
    ang = pos.astype(F32)[:, None] * inv_freq[None, :]
    return jnp.stack([jnp.cos(ang), jnp.sin(ang)])


def _rope_lane_tables(cos, sin):
    half = ROT_DIM // 2
    lane = lax.broadcasted_iota(jnp.int32, (1, LANES), 1) & (HEAD_DIM - 1)
    cos_t = jnp.where(lane < ROT_DIM, cos, 1.0)
    sin_a = jnp.where(lane < half, -sin, 0.0)
    sin_b = jnp.where((lane >= half) & (lane < ROT_DIM), sin, 0.0)
    return cos_t, sin_a, sin_b


def _prepare_weights(p):
    depth = p["w_in"].shape[0]
    w_in = p["w_in"]
    wq = w_in[:, :, :D_MODEL].reshape(depth, D_MODEL, N_KV_HEADS, Q_REP, HEAD_DIM)
    wq = wq.transpose(0, 1, 3, 2, 4).reshape(depth, D_MODEL, D_MODEL).astype(BF16)
    w_ao = p["w_attn_o"].reshape(depth, N_KV_HEADS, Q_REP, HEAD_DIM, D_MODEL)
    w_ao = w_ao.transpose(0, 2, 1, 3, 4).reshape(depth, D_MODEL, D_MODEL).astype(BF16)
    ws = p["w_spatial"]
    n_pairs = GMLP_GROUPS // 2
    wpair = ws.reshape(depth, n_pairs, 2, GMLP_CHUNK, GMLP_CHUNK).transpose(0, 1, 3, 2, 4)
    wpair = wpair.reshape(depth, n_pairs, GMLP_CHUNK, 2 * GMLP_CHUNK)
    bsp = jnp.repeat(p["b_spatial"].transpose(0, 2, 1), GMLP_GROUP_DIM, axis=2)
    wc8 = jnp.repeat(ws[:, :, :DEC_SEQ, :DEC_SEQ].transpose(0, 3, 2, 1),
                     GMLP_GROUP_DIM, axis=3)
    lane_head = jnp.arange(LANES) // HEAD_DIM
    segm = jnp.where(lane_head[:, None] == lane_head[None, :], 1.0 / HEAD_DIM, 0.0).astype(BF16)
    caw = jnp.concatenate([p["conv_a_w"], jnp.zeros((depth, 1, CONV_W), F32)], axis=1)
    v512 = jnp.concatenate([p["conv_a_b"][:, None], p["ln_a_g"][:, None], p["ln_a_b"][:, None],
                            p["ln_c_g"][:, None], p["ln_c_b"][:, None], p["conv_b_w"]], axis=1)
    reps = LANES // HEAD_DIM
    return {
        "wq": wq, "w_in": w_in.astype(BF16), "w_ao": w_ao,
        "w_aout": p["w_a_out"].astype(BF16), "w_bout": p["w_b_out"].astype(BF16),
        "w_cout": p["w_c_out"].astype(BF16), "w_o": p["w_o"].astype(BF16),
        "w_up": p["w_up"].astype(BF16), "w_down": p["w_down"].astype(BF16),
        "n1g": p["norm1_g"][:, None], "n2g": p["norm2_g"][:, None],
        "qkg": jnp.stack([jnp.tile(p["q_norm_g"], (1, reps)),
                          jnp.tile(p["k_norm_g"], (1, reps))], axis=1),
        "caw": caw, "v512": v512, "wpair": wpair, "bsp": bsp, "wc8": wc8,
        "bsp8": bsp[:, :DEC_SEQ],
        "segm": jnp.broadcast_to(segm, (depth, LANES, LANES)), "sinks": p["attn_sinks"],
    }


def kernel(x_prompt, x_sample, c_prompt, c_sample, state_conv_a, state_conv_b, cache_k_win, cache_v_win, w_ada, b_ada, norm1_g, norm2_g, w_in, q_norm_g, k_norm_g, attn_sinks, w_attn_o, conv_a_w, conv_a_b, ln_a_g, ln_a_b, w_a_out, conv_b_w, w_b_out, ln_c_g, ln_c_b, w_spatial, b_spatial, w_c_out, w_o, w_up, w_down):
    params = dict(w_in=w_in, w_attn_o=w_attn_o, w_spatial=w_spatial, b_spatial=b_spatial,
                  conv_a_w=conv_a_w, conv_a_b=conv_a_b, ln_a_g=ln_a_g, ln_a_b=ln_a_b,
                  ln_c_g=ln_c_g, ln_c_b=ln_c_b, conv_b_w=conv_b_w, w_a_out=w_a_out,
                  w_b_out=w_b_out, w_c_out=w_c_out, w_o=w_o, w_up=w_up, w_down=w_down,
                  norm1_g=norm1_g, norm2_g=norm2_g, q_norm_g=q_norm_g, k_norm_g=k_norm_g,
                  attn_sinks=attn_sinks)
    depth = w_in.shape[0]
    bp, seq, _ = x_prompt.shape
    bs, tq, _ = x_sample.shape

    c_all = jnp.concatenate([c_prompt, c_sample,
                             jnp.zeros((MOD_ROWS - bp - bs, D_MODEL), F32)], axis=0)
    mod_all = _modulation(c_all, w_ada, b_ada)
    rope_local = _rope_cos_sin(jnp.arange(PROMPT_TILE))
    rope_tile = _rope_cos_sin(jnp.arange(0, seq, PROMPT_TILE)).transpose(1, 0, 2)
    rope_s = jnp.tile(_rope_cos_sin(PAST_LEN + jnp.arange(tq)), (1, SAMPLE_ATTN_SEQS, 1))
    state_a_tm = state_conv_a.transpose(0, 2, 1, 3)
    cache_k = cache_k_win.reshape(depth, bs, WINDOW, KV_DIM)
    cache_v = cache_v_win.reshape(depth, bs, WINDOW, KV_DIM)

    xp, xs = x_prompt, x_sample
    outs = [[] for _ in range(9)]
    lw = _prepare_weights(params)
    for l in range(depth):
        mod_p = mod_all[l, :bp].reshape(bp, 6, D_MODEL)
        mod_s = mod_all[l, bp:bp + bs].reshape(bs, 6, D_MODEL)
        xp, pa, pb, pk, pv = _mixer_prompt(xp, mod_p, rope_local, rope_tile, lw, l)
        xp = _mlp(xp, mod_p, lw, l, per_sequence=False)
        s_attn, sk, sv = _sample_attention(xs, mod_s, rope_s, cache_k, cache_v, lw, l)
        xs, sa, sb, sg = _sample_rest(xs, mod_s, s_attn, state_a_tm, state_conv_b, lw, l)
        sa = sa.transpose(1, 0, 2)
        xs = _mlp(xs, mod_s, lw, l, per_sequence=True)
        kv_shape_p = (bp, WINDOW, N_KV_HEADS, HEAD_DIM)
        kv_shape_s = (bs, WINDOW, N_KV_HEADS, HEAD_DIM)
        for lst, val in zip(outs, (pa, sa, pb, sb, pk.reshape(kv_shape_p), sk.reshape(kv_shape_s),
                                   pv.reshape(kv_shape_p), sv.reshape(kv_shape_s), sg)):
            lst.append(val)
    return (xp, xs) + tuple(jnp.stack(o) for o in outs)
```

```python
import jax
import jax.numpy as jnp
from jax import lax
from jax.experimental import pallas as pl
from jax.experimental.pallas import tpu as pltpu

F32 = jnp.float32
BF16 = jnp.bfloat16

D_MODEL = 1024
HEAD_DIM = 64
N_HEADS = 16
N_KV_HEADS = 4
Q_REP = 4
KV_DIM = N_KV_HEADS * HEAD_DIM
ROT_DIM = 16
ROPE_THETA = 500000.0
WINDOW = 128
ATTN_BLOCK = 128
KEY_SPAN = 2 * ATTN_BLOCK
CONV_W = 512
CONF_CONV_W = 31
SC_CONV_W = 3
GMLP_CHUNK = 128
GMLP_GROUPS = 8
GMLP_GROUP_DIM = 64
D_FF = 4 * D_MODEL
EPS = 1e-6
PAST_LEN = 16384
DEC_SEQ = 8
NEG = -1e30

K0 = D_MODEL
V0 = K0 + KV_DIM
A0 = V0 + KV_DIM
B0 = A0 + 2 * CONV_W
C0 = B0 + 3 * CONV_W
G0 = C0 + 2 * CONV_W

LANES = 128
SUBLANES = 8
VMEM_BYTES = 64 * 1024 * 1024

PROMPT_TILE = 512
MLP_TILE = 1024
SAMPLE_SEQS = 32
SAMPLE_ATTN_SEQS = 16
CONV_ROWS = 64
A_PAD = 32
B_PAD = 8
MOD_ROWS = 136


def _dot(a, b):
    return jnp.dot(a, b, preferred_element_type=F32)


def _rms_mod(x, g, scale, shift):
    ms = jnp.mean(x * x, axis=-1, keepdims=True)
    y = x * lax.rsqrt(ms + EPS)
    return (y * g) * (1.0 + scale) + shift


def _layer_norm(x, g, b):
    mu = jnp.mean(x, axis=-1, keepdims=True)
    xc = x - mu
    var = jnp.mean(xc * xc, axis=-1, keepdims=True)
    return xc * lax.rsqrt(var + EPS) * g + b


def _head_norm_rope(xc, gain, seg_mean, cos_t, sin_a, sin_b):
    ms = _dot((xc * xc).astype(BF16), seg_mean)
    y = xc * lax.rsqrt(ms + EPS) * gain
    return (y * cos_t + pltpu.roll(y, LANES - ROT_DIM // 2, 1) * sin_a
            + pltpu.roll(y, ROT_DIM // 2, 1) * sin_b)


def _qkv_heads(q_raw, kv_raw, qkg_ref, seg_mean, cos_t, sin_a, sin_b):
    def chunk(src, c, gain):
        return _head_norm_rope(src[:, c * LANES:(c + 1) * LANES], gain, seg_mean,
                               cos_t, sin_a, sin_b)
    q = jnp.concatenate([chunk(q_raw, c, qkg_ref[0:1, :]) for c in range(D_MODEL // LANES)],
                        axis=1)
    k = jnp.concatenate([chunk(kv_raw, c, qkg_ref[1:2, :])
                         for c in range(KV_DIM // LANES)], axis=1)
    return q, k, kv_raw[:, KV_DIM:2 * KV_DIM]


def _kv_head_masks(scale):
    lane = lax.broadcasted_iota(jnp.int32, (1, KV_DIM), 1)
    return [jnp.where((lane >= g * HEAD_DIM) & (lane < (g + 1) * HEAD_DIM), scale, 0.0).astype(F32)
            for g in range(N_KV_HEADS)]


def _softmax_with_sink(s, sink):
    m = jnp.maximum(jnp.max(s, axis=-1, keepdims=True), sink)
    e = jnp.exp(s - m)
    den = jnp.sum(e, axis=-1, keepdims=True) + jnp.exp(sink - m)
    return e * (1.0 / den)


def _causal_conv_rows(hist_ref, row0, rows, w_ref, n_taps, base, bias):
    out = jnp.broadcast_to(bias, (rows, hist_ref.shape[1]))
    for phase in range(SUBLANES):
        span = rows if phase == 0 else rows + SUBLANES
        part = None
        for tap in range(n_taps):
            offset = base + tap
            if offset % SUBLANES != phase:
                continue
            lo = row0 + offset - phase
            term = w_ref[tap:tap + 1, :] * hist_ref[lo:lo + span, :]
            part = term if part is None else part + term
        if part is not None:
            out = out + part[phase:phase + rows]
    return out


def _mod_kernel(c_ref, w_ref, b_ref, o_ref):
    c = c_ref[...]
    s = (c * jax.nn.sigmoid(c)).astype(BF16)
    o_ref[...] = _dot(s, w_ref[...].astype(BF16)) + b_ref[...]


def _modulation(c_all, w_ada, b_ada):
    depth = w_ada.shape[0]
    n_chunks = w_ada.shape[2] // D_MODEL
    return pl.pallas_call(
        _mod_kernel,
        out_shape=jax.ShapeDtypeStruct((depth, MOD_ROWS, n_chunks * D_MODEL), F32),
        grid=(depth, n_chunks),
        in_specs=[
            pl.BlockSpec((MOD_ROWS, D_MODEL), lambda l, j: (0, 0)),
            pl.BlockSpec((None, D_MODEL, D_MODEL), lambda l, j: (l, 0, j)),
            pl.BlockSpec((None, 1, D_MODEL), lambda l, j: (l, 0, j)),
        ],
        out_specs=pl.BlockSpec((None, MOD_ROWS, D_MODEL), lambda l, j: (l, 0, j)),
        compiler_params=pltpu.CompilerParams(
            dimension_semantics=("arbitrary", "arbitrary"),
            vmem_limit_bytes=32 * 1024 * 1024),
        name="adaln_modulation",
    )(c_all, w_ada, b_ada.reshape(depth, 1, -1))


def _per_head_copies(win, head_masks):
    return jnp.concatenate([jnp.where(m, win, jnp.zeros_like(win)) for m in head_masks], axis=0)


def _block_scores(q_blk, k_heads):
    qs = (q_blk * HEAD_DIM ** -0.5).astype(BF16)
    q_rows = jnp.concatenate([qs[:, r * KV_DIM:(r + 1) * KV_DIM] for r in range(Q_REP)], axis=0)
    return lax.dot_general(q_rows, k_heads, (((1,), (1,)), ((), ())), preferred_element_type=F32)


def _block_probs(s, bias, sink_ref):
    tq, keys = bias.shape
    rows = []
    for r in range(Q_REP):
        cols = []
        for g in range(N_KV_HEADS):
            sp = s[r * tq:(r + 1) * tq, g * keys:(g + 1) * keys] + bias
            cols.append(_softmax_with_sink(sp, sink_ref[g * Q_REP + r]).astype(BF16))
        rows.append(jnp.concatenate(cols, axis=1))
    return jnp.concatenate(rows, axis=0)


def _block_output(p, v_heads):
    tq = p.shape[0] // Q_REP
    o = _dot(p, v_heads)
    return jnp.concatenate([o[r * tq:(r + 1) * tq] for r in range(Q_REP)], axis=1)


def _mixer_prompt_kernel(x_ref, mod_ref, rope_ref, rope0_ref, sink_ref, n1g_ref, qkg_ref, caw_ref, v512_ref,
                         wpair_ref, bsp_ref, segm_ref,
                         wq_ref, w_in_ref, w_ao_ref, w_aout_ref, w_bout_ref, w_cout_ref, w_o_ref,
                         x1_ref, ca_ref, cb_ref, ko_ref, vo_ref,
                         kprev_s, vprev_s, aprev_s, bprev_s, acat_s, bcat_s, bias_s):
    tile = x_ref.shape[0]
    j = pl.program_id(1)
    last = j == pl.num_programs(1) - 1
    rd = j & 1
    wr = 1 - rd

    @pl.when(j == 0)
    def _():
        kprev_s[0] = jnp.zeros((ATTN_BLOCK, KV_DIM), BF16)
        vprev_s[0] = jnp.zeros((ATTN_BLOCK, KV_DIM), BF16)
        aprev_s[0] = jnp.zeros((A_PAD, CONV_W), F32)
        bprev_s[0] = jnp.zeros((B_PAD, CONV_W), F32)

    x = x_ref[...]
    shift1, scale1, gate1 = mod_ref[0:1, :], mod_ref[1:2, :], mod_ref[2:3, :]
    h = _rms_mod(x, n1g_ref[...], scale1, shift1).astype(BF16)

    def gate_logits(branch):
        lo = G0 + branch * D_MODEL
        return _dot(h, w_in_ref[:, lo:lo + D_MODEL])

    q_raw = _dot(h, wq_ref[...])
    kv_raw = _dot(h, w_in_ref[:, K0:A0])
    a_in = _dot(h, w_in_ref[:, A0:A0 + 2 * CONV_W])
    cos_l, sin_l = rope_ref[0], rope_ref[1]
    cos_0, sin_0 = rope0_ref[0:1, :], rope0_ref[1:2, :]
    cos_t, sin_a, sin_b = _rope_lane_tables(cos_0 * cos_l - sin_0 * sin_l,
                                            sin_0 * cos_l + cos_0 * sin_l)
    q, k, v = _qkv_heads(q_raw, kv_raw, qkg_ref, segm_ref[...], cos_t, sin_a, sin_b)
    kcat = jnp.concatenate([kprev_s[rd], k.astype(BF16)], axis=0)
    vcat = jnp.concatenate([vprev_s[rd], v.astype(BF16)], axis=0)

    @pl.when(last)
    def _():
        ko_ref[...] = k[tile - WINDOW:tile]
        vo_ref[...] = v[tile - WINDOW:tile]

    tt = lax.broadcasted_iota(jnp.int32, (ATTN_BLOCK, KEY_SPAN), 0)
    jj = lax.broadcasted_iota(jnp.int32, (ATTN_BLOCK, KEY_SPAN), 1)
    dist = jj - tt
    bias_rest = jnp.where((dist >= 0) & (dist <= WINDOW), 0.0, NEG).astype(F32)

    @pl.when(j == 0)
    def _():
        bias_s[...] = jnp.where(jj < ATTN_BLOCK, NEG, bias_rest)

    @pl.when(j == 1)
    def _():
        bias_s[...] = bias_rest

    bias_first = bias_s[...]
    lane = lax.broadcasted_iota(jnp.int32, (1, KV_DIM), 1)
    head_masks = [(lane >= g * HEAD_DIM) & (lane < (g + 1) * HEAD_DIM) for g in range(N_KV_HEADS)]
    kprev_s[wr] = kcat[tile:tile + ATTN_BLOCK]
    vprev_s[wr] = vcat[tile:tile + ATTN_BLOCK]
    n_blocks = tile // ATTN_BLOCK

    def scores(i):
        lo = i * ATTN_BLOCK
        return _block_scores(q[lo:lo + ATTN_BLOCK],
                             _per_head_copies(kcat[lo:lo + KEY_SPAN], head_masks))

    glu = a_in[:, :CONV_W] * jax.nn.sigmoid(a_in[:, CONV_W:])
    acat_s[0:A_PAD, :] = aprev_s[rd]
    acat_s[A_PAD:A_PAD + tile, :] = glu
    aprev_s[wr] = glu[tile - A_PAD:tile]
    base = A_PAD - (CONF_CONV_W - 1)

    @pl.when(last)
    def _():
        ca_ref[...] = acat_s[base + tile:A_PAD + tile, :]

    n_chunks = tile // CONV_ROWS
    conv_rows = []

    def conv_chunks(lo, hi):
        for c in range(lo, hi):
            conv_rows.append(_causal_conv_rows(acat_s, c * CONV_ROWS, CONV_ROWS, caw_ref,
                                               CONF_CONV_W, base, v512_ref[0:1, :]))

    fillers = [lambda: _dot(h, w_in_ref[:, B0:B0 + 3 * CONV_W]),
               lambda: _dot(h, w_in_ref[:, C0:C0 + 2 * CONV_W])]
    filled = []
    blocks = []
    p = None
    for i in range(n_blocks + 1):
        if i > 0:
            lo = (i - 1) * ATTN_BLOCK
            blocks.append(_block_output(p, _per_head_copies(vcat[lo:lo + KEY_SPAN], head_masks)))
        if i < n_blocks:
            s_blk = scores(i)
        if i < len(fillers):
            filled.append(fillers[i]())
        if i == n_blocks:
            for f in fillers[len(filled):]:
                filled.append(f())
            g0 = gate_logits(0)
            g1 = gate_logits(1)
        if i < n_blocks:
            p = _block_probs(s_blk, bias_first if i == 0 else bias_rest, sink_ref)
            conv_chunks(i, i + 1)
    b_in, c_in = filled
    conv_chunks(n_blocks, n_chunks)
    attn = jnp.concatenate(blocks, axis=0).astype(BF16)
    a_conv = jnp.concatenate(conv_rows, axis=0)
    a_act = jax.nn.silu(_layer_norm(a_conv, v512_ref[1:2, :], v512_ref[2:3, :])).astype(BF16)
    attn_out = _dot(attn, w_ao_ref[...])
    a_out = _dot(a_act, w_aout_ref[...])
    g2 = gate_logits(2)

    ch = b_in[:, CONV_W:2 * CONV_W] * b_in[:, 2 * CONV_W:]
    bcat_s[0:B_PAD, :] = bprev_s[rd]
    bcat_s[B_PAD:B_PAD + tile, :] = ch
    bprev_s[wr] = ch[tile - B_PAD:tile]
    base_b = B_PAD - (SC_CONV_W - 1)
    sc = None
    for t in range(SC_CONV_W):
        term = v512_ref[5 + t:6 + t, :] * bcat_s[base_b + t:base_b + t + tile, :]
        sc = term if sc is None else sc + term

    @pl.when(last)
    def _():
        cb_ref[...] = bcat_s[base_b + tile:B_PAD + tile, :]

    merged = jax.nn.sigmoid(g0) * attn_out
    merged = merged + jax.nn.sigmoid(g1) * a_out

    zc = jax.nn.gelu(c_in)
    u = zc[:, :CONV_W]
    vv = _layer_norm(zc[:, CONV_W:], v512_ref[3:4, :], v512_ref[4:5, :])
    row = lax.broadcasted_iota(jnp.int32, (GMLP_CHUNK, 2 * GMLP_CHUNK), 0)
    col = lax.broadcasted_iota(jnp.int32, (GMLP_CHUNK, 2 * GMLP_CHUNK), 1)
    causal = (col & (GMLP_CHUNK - 1)) <= row
    n_pairs = GMLP_GROUPS // 2
    wpair = [jnp.where(causal, wpair_ref[p], 0.0).astype(BF16) for p in range(n_pairs)]
    low = lax.broadcasted_iota(jnp.int32, (1, LANES), 1) < GMLP_GROUP_DIM
    rhs_blocks = []
    for c in range(tile // GMLP_CHUNK):
        vch = vv[c * GMLP_CHUNK:(c + 1) * GMLP_CHUNK]
        for pair in range(n_pairs):
            blk = vch[:, pair * LANES:(pair + 1) * LANES]
            rhs = jnp.concatenate([jnp.where(low, blk, 0.0), jnp.where(low, 0.0, blk)], axis=0)
            rhs_blocks.append(rhs.astype(BF16))
    b_act = (b_in[:, :CONV_W] * sc).astype(BF16)
    b_out = _dot(b_act, w_bout_ref[...])
    zs = [_dot(wpair[i % n_pairs], rhs) for i, rhs in enumerate(rhs_blocks)]
    g3 = gate_logits(3)
    gated = []
    for c in range(tile // GMLP_CHUNK):
        z = jnp.concatenate(zs[c * n_pairs:(c + 1) * n_pairs], axis=1) + bsp_ref[...]
        gated.append(u[c * GMLP_CHUNK:(c + 1) * GMLP_CHUNK] * z)
    gated = jnp.concatenate(gated, axis=0).astype(BF16)
    merged = merged + jax.nn.sigmoid(g2) * b_out
    c_out = _dot(gated, w_cout_ref[...])
    merged = (merged + jax.nn.sigmoid(g3) * c_out).astype(BF16)
    x1_ref[...] = x + gate1 * _dot(merged, w_o_ref[...])


def _const_spec(shape):
    zeros = (0,) * len(shape)
    return pl.BlockSpec(shape, lambda *_: zeros, pipeline_mode=pl.Buffered(1))


def _layer_spec(stacked, layer, cols=None):
    if cols is None:
        block, col_block = stacked.shape[1:], 0
    else:
        start, width = cols
        assert start % width == 0
        block, col_block = stacked.shape[1:-1] + (width,), start // width
    index = (layer,) + (0,) * (len(block) - 1) + (col_block,)
    return pl.BlockSpec((None,) + block, lambda *_: index, pipeline_mode=pl.Buffered(1))


def _mod_rows_spec(layer, nseq):
    return pl.BlockSpec((None, nseq, 6, D_MODEL), lambda i: (layer, i, 0, 0))


def _mod_sequence_spec(layer, row0):
    return pl.BlockSpec((None, None, 6, D_MODEL), lambda b, j: (layer, row0 + b, 0, 0))


def _mixer_prompt(x, mod, mod_row0, rope_local, rope_tile, lw, layer):
    bsz, seq, _ = x.shape
    tile = PROMPT_TILE
    small = [lw["n1g"], lw["qkg"], lw["caw"], lw["v512"], lw["wpair"], lw["bsp"], lw["segm"]]
    weights = [lw["wq"], lw["w_in"], lw["w_ao"], lw["w_aout"], lw["w_bout"], lw["w_cout"],
               lw["w_o"]]
    def per_sequence(r, c):
        return pl.BlockSpec((None, r, c), lambda b, j: (b, 0, 0))

    in_specs = (
        [pl.BlockSpec((None, tile, D_MODEL), lambda b, j: (b, j, 0)),
         _mod_sequence_spec(layer, mod_row0),
         _const_spec(rope_local.shape),
         pl.BlockSpec((None, 2, LANES), lambda b, j: (j, 0, 0)),
         pl.BlockSpec(memory_space=pltpu.SMEM)]
        + [_layer_spec(a, layer) for a in small]
        + [_layer_spec(a, layer) for a in weights])
    out_shape = (
        jax.ShapeDtypeStruct((bsz, seq, D_MODEL), F32),
        jax.ShapeDtypeStruct((bsz, CONF_CONV_W - 1, CONV_W), F32),
        jax.ShapeDtypeStruct((bsz, SC_CONV_W - 1, CONV_W), F32),
        jax.ShapeDtypeStruct((bsz, WINDOW, KV_DIM), F32),
        jax.ShapeDtypeStruct((bsz, WINDOW, KV_DIM), F32),
    )
    out_specs = (
        pl.BlockSpec((None, tile, D_MODEL), lambda b, j: (b, j, 0)),
        per_sequence(CONF_CONV_W - 1, CONV_W), per_sequence(SC_CONV_W - 1, CONV_W),
        per_sequence(WINDOW, KV_DIM), per_sequence(WINDOW, KV_DIM),
    )
    return pl.pallas_call(
        _mixer_prompt_kernel,
        out_shape=out_shape,
        grid=(bsz, seq // tile),
        in_specs=in_specs,
        out_specs=out_specs,
        scratch_shapes=[
            pltpu.VMEM((2, ATTN_BLOCK, KV_DIM), BF16),
            pltpu.VMEM((2, ATTN_BLOCK, KV_DIM), BF16),
            pltpu.VMEM((2, A_PAD, CONV_W), F32),
            pltpu.VMEM((2, B_PAD, CONV_W), F32),
            pltpu.VMEM((A_PAD + tile, CONV_W), F32),
            pltpu.VMEM((B_PAD + tile, CONV_W), F32),
            pltpu.VMEM((ATTN_BLOCK, KEY_SPAN), F32),
        ],
        compiler_params=pltpu.CompilerParams(
            dimension_semantics=("arbitrary", "arbitrary"),
            vmem_limit_bytes=VMEM_BYTES - 4 * 1024 * 1024),
        name="mixer_prompt",
    )(x, mod, rope_local, rope_tile, lw["sinks"][layer], *small, *weights)


SAMPLE_NEW_ROWS = 16


def _sample_attn_kernel(x_ref, mod_ref, rope_ref, sink_ref, n1g_ref, qkg_ref, segm_ref,
                        ck_ref, cv_ref, wq_ref, wkv_ref, w_ao_ref,
                        ao_ref, nk_ref, nv_ref, kcat_s, vcat_s):
    nseq, tq, _ = x_ref.shape
    rows = nseq * tq
    cache_lo = SAMPLE_NEW_ROWS
    cache_hi = cache_lo + WINDOW

    x = x_ref[...]
    shift1, scale1 = mod_ref[:, 0:1, :], mod_ref[:, 1:2, :]
    h = _rms_mod(x, n1g_ref[...], scale1, shift1).reshape(rows, D_MODEL).astype(BF16)

    q, k, v = _qkv_heads(_dot(h, wq_ref[...]), _dot(h, wkv_ref[...]), qkg_ref, segm_ref[...],
                         *_rope_lane_tables(rope_ref[0], rope_ref[1]))
    k3 = k.reshape(nseq, tq, KV_DIM)
    v3 = v.reshape(nseq, tq, KV_DIM)
    keep = WINDOW - tq
    nk_ref[:, 0:keep, :] = ck_ref[:, tq:WINDOW, :]
    nk_ref[:, keep:WINDOW, :] = k3
    nv_ref[:, 0:keep, :] = cv_ref[:, tq:WINDOW, :]
    nv_ref[:, keep:WINDOW, :] = v3
    pad = jnp.zeros((nseq, SAMPLE_NEW_ROWS - tq, KV_DIM), F32)
    kcat_s[:, 0:cache_lo, :] = jnp.concatenate([k3, pad], axis=1).astype(BF16)
    vcat_s[:, 0:cache_lo, :] = jnp.concatenate([v3, pad], axis=1).astype(BF16)
    kcat_s[:, cache_lo:cache_hi, :] = ck_ref[...].astype(BF16)
    vcat_s[:, cache_lo:cache_hi, :] = cv_ref[...].astype(BF16)
    tail = jnp.zeros((nseq, KEY_SPAN - cache_hi, KV_DIM), BF16)
    kcat_s[:, cache_hi:KEY_SPAN, :] = tail
    vcat_s[:, cache_hi:KEY_SPAN, :] = tail

    n_q = N_HEADS * tq
    tt = lax.broadcasted_iota(jnp.int32, (n_q, KEY_SPAN), 0) & (tq - 1)
    ii = lax.broadcasted_iota(jnp.int32, (n_q, KEY_SPAN), 1)
    visible = ((ii < tq) & (ii <= tt)) | ((ii >= cache_lo + tt) & (ii < cache_hi))
    bias = jnp.where(visible, 0.0, NEG).astype(F32)
    cm_scaled = _kv_head_masks(HEAD_DIM ** -0.5)
    cm_plain = _kv_head_masks(1.0)
    q3 = q.reshape(nseq, tq, D_MODEL)
    qbd = jnp.concatenate([q3[:, :, r * KV_DIM:(r + 1) * KV_DIM] * cm_scaled[g]
                           for r in range(Q_REP) for g in range(N_KV_HEADS)], axis=1).astype(BF16)
    s = jnp.einsum("bqc,bkc->bqk", qbd, kcat_s[...], preferred_element_type=F32) + bias
    sink = jnp.concatenate([jnp.full((tq, 1), sink_ref[g * Q_REP + r], F32)
                            for r in range(Q_REP) for g in range(N_KV_HEADS)], axis=0)
    p = _softmax_with_sink(s, sink).astype(BF16)
    o = jnp.einsum("bqk,bkc->bqc", p, vcat_s[...], preferred_element_type=F32)
    slabs = []
    for r in range(Q_REP):
        acc = None
        for g in range(N_KV_HEADS):
            i = r * N_KV_HEADS + g
            part = o[:, i * tq:(i + 1) * tq, :] * cm_plain[g]
            acc = part if acc is None else acc + part
        slabs.append(acc)
    attn = jnp.concatenate(slabs, axis=2).reshape(rows, D_MODEL)
    ao_ref[...] = _dot(attn.astype(BF16), w_ao_ref[...]).reshape(nseq, tq, D_MODEL)


def _sample_rest_kernel(x_ref, mod_ref, ao_ref, n1g_ref, caw_ref, v512_ref, wc_ref, bsp_ref,
                        sta_ref, stb_ref, w_in_ref, w_aout_ref, w_bout_ref, w_cout_ref, w_o_ref,
                        x1_ref, na_ref, nb_ref, vv_ref, glu_s, aact_s, bcat_s):
    nseq, tq, _ = x_ref.shape
    rows = nseq * tq

    x = x_ref[...]
    shift1, scale1, gate1 = mod_ref[:, 0:1, :], mod_ref[:, 1:2, :], mod_ref[:, 2:3, :]
    h = _rms_mod(x, n1g_ref[...], scale1, shift1).reshape(rows, D_MODEL).astype(BF16)
    attn_out = ao_ref[...].reshape(rows, D_MODEL)
    merged = jax.nn.sigmoid(_dot(h, w_in_ref[:, G0:G0 + D_MODEL])) * attn_out

    a_in = _dot(h, w_in_ref[:, A0:A0 + 2 * CONV_W])
    glu = a_in[:, :CONV_W] * jax.nn.sigmoid(a_in[:, CONV_W:])
    hist_a = CONF_CONV_W - 1
    glu_s[...] = glu.reshape(nseq, tq, CONV_W)

    def conv_in(t):
        return sta_ref[t] if t < hist_a else glu_s[:, t - hist_a, :]

    for t in range(tq):
        acc = jnp.broadcast_to(v512_ref[0:1, :], (nseq, CONV_W))
        for tap in range(CONF_CONV_W):
            acc = acc + caw_ref[tap:tap + 1, :] * conv_in(t + tap)
        aact_s[:, t, :] = jax.nn.silu(_layer_norm(acc, v512_ref[1:2, :], v512_ref[2:3, :]))
    for t in range(hist_a):
        na_ref[t] = conv_in(t + tq)
    a_out = _dot(aact_s[...].reshape(rows, CONV_W).astype(BF16), w_aout_ref[...])
    merged = merged + jax.nn.sigmoid(_dot(h, w_in_ref[:, G0 + D_MODEL:G0 + 2 * D_MODEL])) * a_out

    b_in = _dot(h, w_in_ref[:, B0:B0 + 3 * CONV_W])
    hist_b = SC_CONV_W - 1
    base_b = B_PAD - hist_b
    bcat_s[:, base_b:B_PAD, :] = stb_ref[...]
    bcat_s[:, B_PAD:B_PAD + tq, :] = (b_in[:, CONV_W:2 * CONV_W]
                                      * b_in[:, 2 * CONV_W:]).reshape(nseq, tq, CONV_W)
    sc = None
    for t in range(SC_CONV_W):
        term = v512_ref[5 + t:6 + t, :] * bcat_s[:, base_b + t:base_b + t + tq, :]
        sc = term if sc is None else sc + term
    nb_ref[...] = bcat_s[:, base_b + tq:B_PAD + tq, :]
    b_out = _dot((b_in[:, :CONV_W] * sc.reshape(rows, CONV_W)).astype(BF16), w_bout_ref[...])
    merged = merged + jax.nn.sigmoid(_dot(h, w_in_ref[:, G0 + 2 * D_MODEL:G0 + 3 * D_MODEL])) * b_out

    zc = jax.nn.gelu(_dot(h, w_in_ref[:, C0:C0 + 2 * CONV_W]))
    u3 = zc[:, :CONV_W].reshape(nseq, tq, CONV_W)
    vv3 = _layer_norm(zc[:, CONV_W:], v512_ref[3:4, :], v512_ref[4:5, :]).reshape(nseq, tq, CONV_W)
    vv_ref[...] = vv3
    t_idx = lax.broadcasted_iota(jnp.int32, (tq, CONV_W), 0)
    z = jnp.broadcast_to(bsp_ref[...], (nseq, tq, CONV_W))
    for src in range(tq):
        coef = jnp.where(t_idx >= src, wc_ref[src], 0.0)
        z = z + coef * vv3[:, src:src + 1, :]
    c_out = _dot((u3 * z).reshape(rows, CONV_W).astype(BF16), w_cout_ref[...])
    merged = merged + jax.nn.sigmoid(_dot(h, w_in_ref[:, G0 + 3 * D_MODEL:G0 + 4 * D_MODEL])) * c_out

    y = _dot(merged.astype(BF16), w_o_ref[...]).reshape(nseq, tq, D_MODEL)
    x1_ref[...] = x + gate1 * y


def _seq_spec(nseq, r, c):
    return pl.BlockSpec((nseq, r, c), lambda i: (i, 0, 0))


def _sample_attention(x, mod, rope, cache_k, cache_v, lw, layer):
    nbatch, tq, _ = x.shape
    nseq = SAMPLE_ATTN_SEQS
    small = [lw["n1g"], lw["qkg"], lw["segm"]]

    def cache_spec():
        return pl.BlockSpec((None, nseq, WINDOW, KV_DIM), lambda i: (layer, i, 0, 0))

    in_specs = (
        [_seq_spec(nseq, tq, D_MODEL), _mod_rows_spec(layer, nseq),
         _const_spec(rope.shape), pl.BlockSpec(memory_space=pltpu.SMEM)]
        + [_layer_spec(a, layer) for a in small]
        + [cache_spec(), cache_spec(),
           _layer_spec(lw["wq"], layer), _layer_spec(lw["w_in"], layer, cols=(K0, 2 * KV_DIM)),
           _layer_spec(lw["w_ao"], layer)])
    out_shape = (
        jax.ShapeDtypeStruct((nbatch, tq, D_MODEL), F32),
        jax.ShapeDtypeStruct((nbatch, WINDOW, KV_DIM), F32),
        jax.ShapeDtypeStruct((nbatch, WINDOW, KV_DIM), F32),
    )
    out_specs = (_seq_spec(nseq, tq, D_MODEL), _seq_spec(nseq, WINDOW, KV_DIM),
                 _seq_spec(nseq, WINDOW, KV_DIM))
    return pl.pallas_call(
        _sample_attn_kernel,
        out_shape=out_shape,
        grid=(nbatch // nseq,),
        in_specs=in_specs,
        out_specs=out_specs,
        scratch_shapes=[
            pltpu.VMEM((nseq, KEY_SPAN, KV_DIM), BF16),
            pltpu.VMEM((nseq, KEY_SPAN, KV_DIM), BF16),
        ],
        compiler_params=pltpu.CompilerParams(
            dimension_semantics=("arbitrary",),
            vmem_limit_bytes=VMEM_BYTES - 16 * 1024 * 1024),
        name="sample_attention",
    )(x, mod, rope, lw["sinks"][layer], *small, cache_k, cache_v, lw["wq"], lw["w_in"], lw["w_ao"])


def _sample_rest(x, mod, attn_out, state_a, state_b, lw, layer):
    nbatch, tq, _ = x.shape
    nseq = SAMPLE_SEQS
    small = [lw["n1g"], lw["caw"], lw["v512"], lw["wc8"], lw["bsp8"]]
    weights = [lw["w_in"], lw["w_aout"], lw["w_bout"], lw["w_cout"], lw["w_o"]]
    hist_a = CONF_CONV_W - 1
    in_specs = (
        [_seq_spec(nseq, tq, D_MODEL), _mod_rows_spec(layer, nseq), _seq_spec(nseq, tq, D_MODEL)]
        + [_layer_spec(a, layer) for a in small]
        + [pl.BlockSpec((None, hist_a, nseq, CONV_W), lambda i: (layer, 0, i, 0)),
           pl.BlockSpec((None, nseq, SC_CONV_W - 1, CONV_W), lambda i: (layer, i, 0, 0))]
        + [_layer_spec(a, layer) for a in weights])
    out_shape = (
        jax.ShapeDtypeStruct((nbatch, tq, D_MODEL), F32),
        jax.ShapeDtypeStruct((hist_a, nbatch, CONV_W), F32),
        jax.ShapeDtypeStruct((nbatch, SC_CONV_W - 1, CONV_W), F32),
        jax.ShapeDtypeStruct((nbatch, tq, CONV_W), F32),
    )
    out_specs = (_seq_spec(nseq, tq, D_MODEL),
                 pl.BlockSpec((hist_a, nseq, CONV_W), lambda i: (0, i, 0)),
                 _seq_spec(nseq, SC_CONV_W - 1, CONV_W), _seq_spec(nseq, tq, CONV_W))
    return pl.pallas_call(
        _sample_rest_kernel,
        out_shape=out_shape,
        grid=(nbatch // nseq,),
        in_specs=in_specs,
        out_specs=out_specs,
        scratch_shapes=[
            pltpu.VMEM((nseq, tq, CONV_W), F32),
            pltpu.VMEM((nseq, tq, CONV_W), F32),
            pltpu.VMEM((nseq, B_PAD + tq, CONV_W), F32),
        ],
        compiler_params=pltpu.CompilerParams(
            dimension_semantics=("arbitrary",),
            vmem_limit_bytes=VMEM_BYTES - 8 * 1024 * 1024),
        name="sample_rest",
    )(x, mod, attn_out, *small, state_a, state_b, *weights)


def _mlp_kernel(x_ref, mod_ref, g_ref, w_up_ref, w_down_ref, o_ref):
    x = x_ref[...]
    if x.ndim == 3:
        shift2, scale2, gate2 = mod_ref[:, 3:4, :], mod_ref[:, 4:5, :], mod_ref[:, 5:6, :]
    else:
        shift2, scale2, gate2 = mod_ref[3:4, :], mod_ref[4:5, :], mod_ref[5:6, :]
    h2 = _rms_mod(x, g_ref[...], scale2, shift2)
    up = _dot(h2.reshape(-1, D_MODEL).astype(BF16), w_up_ref[...])
    act = jnp.square(jnp.maximum(up, 0.0)).astype(BF16)
    y = _dot(act, w_down_ref[...]).reshape(x.shape)
    o_ref[...] = x + gate2 * y


def _mlp(x, mod, mod_row0, lw, layer, per_sequence):
    if per_sequence:
        nbatch, tq, _ = x.shape
        nseq = MLP_TILE // tq
        grid = (nbatch // nseq,)
        x_spec = pl.BlockSpec((nseq, tq, D_MODEL), lambda i: (i, 0, 0))
        mod_spec = _mod_rows_spec(layer, nseq)
    else:
        bsz, seq, _ = x.shape
        grid = (bsz, seq // MLP_TILE)
        x_spec = pl.BlockSpec((None, MLP_TILE, D_MODEL), lambda b, j: (b, j, 0))
        mod_spec = _mod_sequence_spec(layer, mod_row0)
    return pl.pallas_call(
        _mlp_kernel,
        out_shape=jax.ShapeDtypeStruct(x.shape, F32),
        grid=grid,
        in_specs=[x_spec, mod_spec, _layer_spec(lw["n2g"], layer),
                  _layer_spec(lw["w_up"], layer), _layer_spec(lw["w_down"], layer)],
        out_specs=x_spec,
        compiler_params=pltpu.CompilerParams(
            dimension_semantics=("arbitrary",) * len(grid),
            vmem_limit_bytes=VMEM_BYTES - 4 * 1024 * 1024),
        name="mlp_sample" if per_sequence else "mlp_prompt",
    )(x, mod, lw["n2g"], lw["w_up"], lw["w_down"])


def _rope_cos_sin(pos):
    half = ROT_DIM // 2
    freq = (jnp.arange(LANES) % half).astype(F32)
    inv_freq = jnp.power(ROPE_THETA, -freq * 2.0 / ROT_DIM)
    ang = pos.astype(F32)[:, None] * inv_freq[None, :]
    return jnp.stack([jnp.cos(ang), jnp.sin(ang)])


def _rope_lane_tables(cos, sin):
    half = ROT_DIM // 2
    lane = lax.broadcasted_iota(jnp.int32, (1, LANES), 1) & (HEAD_DIM - 1)
    cos_t = jnp.where(lane < ROT_DIM, cos, 1.0)
    sin_a = jnp.where(lane < half, -sin, 0.0)
    sin_b = jnp.where((lane >= half) & (lane < ROT_DIM), sin, 0.0)
    return cos_t, sin_a, sin_b


def _prepare_weights(p):
    depth = p["w_in"].shape[0]
    w_in = p["w_in"]
    wq = w_in[:, :, :D_MODEL].reshape(depth, D_MODEL, N_KV_HEADS, Q_REP, HEAD_DIM)
    wq = wq.transpose(0, 1, 3, 2, 4).reshape(depth, D_MODEL, D_MODEL).astype(BF16)
    w_ao = p["w_attn_o"].reshape(depth, N_KV_HEADS, Q_REP, HEAD_DIM, D_MODEL)
    w_ao = w_ao.transpose(0, 2, 1, 3, 4).reshape(depth, D_MODEL, D_MODEL).astype(BF16)
    ws = p["w_spatial"]
    n_pairs = GMLP_GROUPS // 2
    wpair = ws.reshape(depth, n_pairs, 2, GMLP_CHUNK, GMLP_CHUNK).transpose(0, 1, 3, 2, 4)
    wpair = wpair.reshape(depth, n_pairs, GMLP_CHUNK, 2 * GMLP_CHUNK)
    bsp = jnp.repeat(p["b_spatial"].transpose(0, 2, 1), GMLP_GROUP_DIM, axis=2)
    wc8 = jnp.repeat(ws[:, :, :DEC_SEQ, :DEC_SEQ].transpose(0, 3, 2, 1),
                     GMLP_GROUP_DIM, axis=3)
    lane_head = jnp.arange(LANES) // HEAD_DIM
    segm = jnp.where(lane_head[:, None] == lane_head[None, :], 1.0 / HEAD_DIM, 0.0).astype(BF16)
    caw = jnp.concatenate([p["conv_a_w"], jnp.zeros((depth, 1, CONV_W), F32)], axis=1)
    v512 = jnp.concatenate([p["conv_a_b"][:, None], p["ln_a_g"][:, None], p["ln_a_b"][:, None],
                            p["ln_c_g"][:, None], p["ln_c_b"][:, None], p["conv_b_w"]], axis=1)
    reps = LANES // HEAD_DIM
    return {
        "wq": wq, "w_in": w_in.astype(BF16), "w_ao": w_ao,
        "w_aout": p["w_a_out"].astype(BF16), "w_bout": p["w_b_out"].astype(BF16),
        "w_cout": p["w_c_out"].astype(BF16), "w_o": p["w_o"].astype(BF16),
        "w_up": p["w_up"].astype(BF16), "w_down": p["w_down"].astype(BF16),
        "n1g": p["norm1_g"][:, None], "n2g": p["norm2_g"][:, None],
        "qkg": jnp.stack([jnp.tile(p["q_norm_g"], (1, reps)),
                          jnp.tile(p["k_norm_g"], (1, reps))], axis=1),
        "caw": caw, "v512": v512, "wpair": wpair, "bsp": bsp, "wc8": wc8,
        "bsp8": bsp[:, :DEC_SEQ],
        "segm": jnp.broadcast_to(segm, (depth, LANES, LANES)), "sinks": p["attn_sinks"],
    }


def kernel(x_prompt, x_sample, c_prompt, c_sample, state_conv_a, state_conv_b, cache_k_win, cache_v_win, w_ada, b_ada, norm1_g, norm2_g, w_in, q_norm_g, k_norm_g, attn_sinks, w_attn_o, conv_a_w, conv_a_b, ln_a_g, ln_a_b, w_a_out, conv_b_w, w_b_out, ln_c_g, ln_c_b, w_spatial, b_spatial, w_c_out, w_o, w_up, w_down):
    params = dict(w_in=w_in, w_attn_o=w_attn_o, w_spatial=w_spatial, b_spatial=b_spatial,
                  conv_a_w=conv_a_w, conv_a_b=conv_a_b, ln_a_g=ln_a_g, ln_a_b=ln_a_b,
                  ln_c_g=ln_c_g, ln_c_b=ln_c_b, conv_b_w=conv_b_w, w_a_out=w_a_out,
                  w_b_out=w_b_out, w_c_out=w_c_out, w_o=w_o, w_up=w_up, w_down=w_down,
                  norm1_g=norm1_g, norm2_g=norm2_g, q_norm_g=q_norm_g, k_norm_g=k_norm_g,
                  attn_sinks=attn_sinks)
    depth = w_in.shape[0]
    bp, seq, _ = x_prompt.shape
    bs, tq, _ = x_sample.shape

    c_all = jnp.concatenate([c_sample, c_prompt,
                             jnp.zeros((MOD_ROWS - bp - bs, D_MODEL), F32)], axis=0)
    mod = _modulation(c_all, w_ada, b_ada).reshape(depth, MOD_ROWS, 6, D_MODEL)
    rope_local = _rope_cos_sin(jnp.arange(PROMPT_TILE))
    rope_tile = _rope_cos_sin(jnp.arange(0, seq, PROMPT_TILE)).transpose(1, 0, 2)
    rope_s = jnp.tile(_rope_cos_sin(PAST_LEN + jnp.arange(tq)), (1, SAMPLE_ATTN_SEQS, 1))
    state_a_tm = state_conv_a.transpose(0, 2, 1, 3)
    cache_k = cache_k_win.reshape(depth, bs, WINDOW, KV_DIM)
    cache_v = cache_v_win.reshape(depth, bs, WINDOW, KV_DIM)

    xp, xs = x_prompt, x_sample
    outs = [[] for _ in range(9)]
    lw = _prepare_weights(params)
    for l in range(depth):
        xp, pa, pb, pk, pv = _mixer_prompt(xp, mod, bs, rope_local, rope_tile, lw, l)
        xp = _mlp(xp, mod, bs, lw, l, per_sequence=False)
        s_attn, sk, sv = _sample_attention(xs, mod, rope_s, cache_k, cache_v, lw, l)
        xs, sa, sb, sg = _sample_rest(xs, mod, s_attn, state_a_tm, state_conv_b, lw, l)
        sa = sa.transpose(1, 0, 2)
        xs = _mlp(xs, mod, bs, lw, l, per_sequence=True)
        kv_shape_p = (bp, WINDOW, N_KV_HEADS, HEAD_DIM)
        kv_shape_s = (bs, WINDOW, N_KV_HEADS, HEAD_DIM)
        for lst, val in zip(outs, (pa, sa, pb, sb, pk.reshape(kv_shape_p), sk.reshape(kv_shape_s),
                                   pv.reshape(kv_shape_p), sv.reshape(kv_shape_s), sg)):
            lst.append(val)
    return (xp, xs) + tuple(jnp.stack(o) for o in outs)
```

```python
import jax
import jax.numpy as jnp
from jax import lax
from jax.experimental import pallas as pl
from jax.experimental.pallas import tpu as pltpu

F32 = jnp.float32
BF16 = jnp.bfloat16

D_MODEL = 1024
HEAD_DIM = 64
N_HEADS = 16
N_KV_HEADS = 4
Q_REP = 4
KV_DIM = N_KV_HEADS * HEAD_DIM
ROT_DIM = 16
ROPE_THETA = 500000.0
WINDOW = 128
ATTN_BLOCK = 128
KEY_SPAN = 2 * ATTN_BLOCK
CONV_W = 512
CONF_CONV_W = 31
SC_CONV_W = 3
GMLP_CHUNK = 128
GMLP_GROUPS = 8
GMLP_GROUP_DIM = 64
D_FF = 4 * D_MODEL
EPS = 1e-6
PAST_LEN = 16384
DEC_SEQ = 8
NEG = -1e30

K0 = D_MODEL
V0 = K0 + KV_DIM
A0 = V0 + KV_DIM
B0 = A0 + 2 * CONV_W
C0 = B0 + 3 * CONV_W
G0 = C0 + 2 * CONV_W

LANES = 128
SUBLANES = 8
VMEM_BYTES = 64 * 1024 * 1024

PROMPT_TILE = 512
MLP_TILE = 1024
SAMPLE_SEQS = 32
SAMPLE_ATTN_SEQS = 16
CONV_ROWS = 64
A_PAD = 32
B_PAD = 8
MOD_ROWS = 136


def _dot(a, b):
    return jnp.dot(a, b, preferred_element_type=F32)


def _rms_mod(x, g, scale, shift):
    ms = jnp.mean(x * x, axis=-1, keepdims=True)
    y = x * lax.rsqrt(ms + EPS)
    return (y * g) * (1.0 + scale) + shift


def _layer_norm(x, g, b):
    mu = jnp.mean(x, axis=-1, keepdims=True)
    xc = x - mu
    var = jnp.mean(xc * xc, axis=-1, keepdims=True)
    return xc * lax.rsqrt(var + EPS) * g + b


def _head_norm_rope(xc, gain, seg_mean, cos_t, sin_a, sin_b):
    ms = _dot((xc * xc).astype(BF16), seg_mean)
    y = xc * lax.rsqrt(ms + EPS) * gain
    return (y * cos_t + pltpu.roll(y, LANES - ROT_DIM // 2, 1) * sin_a
            + pltpu.roll(y, ROT_DIM // 2, 1) * sin_b)


def _qkv_heads(q_raw, kv_raw, qkg_ref, seg_mean, cos_t, sin_a, sin_b):
    def chunk(src, c, gain):
        return _head_norm_rope(src[:, c * LANES:(c + 1) * LANES], gain, seg_mean,
                               cos_t, sin_a, sin_b)
    q = jnp.concatenate([chunk(q_raw, c, qkg_ref[0:1, :]) for c in range(D_MODEL // LANES)],
                        axis=1)
    k = jnp.concatenate([chunk(kv_raw, c, qkg_ref[1:2, :])
                         for c in range(KV_DIM // LANES)], axis=1)
    return q, k, kv_raw[:, KV_DIM:2 * KV_DIM]


def _kv_head_masks(scale):
    lane = lax.broadcasted_iota(jnp.int32, (1, KV_DIM), 1)
    return [jnp.where((lane >= g * HEAD_DIM) & (lane < (g + 1) * HEAD_DIM), scale, 0.0).astype(F32)
            for g in range(N_KV_HEADS)]


def _softmax_with_sink(s, sink):
    m = jnp.maximum(jnp.max(s, axis=-1, keepdims=True), sink)
    e = jnp.exp(s - m)
    den = jnp.sum(e, axis=-1, keepdims=True) + jnp.exp(sink - m)
    return e * (1.0 / den)


def _causal_conv_rows(hist_ref, row0, rows, w_ref, n_taps, base, bias):
    out = jnp.broadcast_to(bias, (rows, hist_ref.shape[1]))
    for phase in range(SUBLANES):
        span = rows if phase == 0 else rows + SUBLANES
        part = None
        for tap in range(n_taps):
            offset = base + tap
            if offset % SUBLANES != phase:
                continue
            lo = row0 + offset - phase
            term = w_ref[tap:tap + 1, :] * hist_ref[lo:lo + span, :]
            part = term if part is None else part + term
        if part is not None:
            out = out + part[phase:phase + rows]
    return out


def _mod_kernel(c_ref, w_ref, b_ref, o_ref):
    c = c_ref[...]
    s = (c * jax.nn.sigmoid(c)).astype(BF16)
    o_ref[...] = _dot(s, w_ref[...].astype(BF16)) + b_ref[...]


def _modulation(c_all, w_ada, b_ada):
    depth = w_ada.shape[0]
    n_chunks = w_ada.shape[2] // D_MODEL
    return pl.pallas_call(
        _mod_kernel,
        out_shape=jax.ShapeDtypeStruct((depth, MOD_ROWS, n_chunks * D_MODEL), F32),
        grid=(depth, n_chunks),
        in_specs=[
            pl.BlockSpec((MOD_ROWS, D_MODEL), lambda l, j: (0, 0)),
            pl.BlockSpec((None, D_MODEL, D_MODEL), lambda l, j: (l, 0, j)),
            pl.BlockSpec((None, 1, D_MODEL), lambda l, j: (l, 0, j)),
        ],
        out_specs=pl.BlockSpec((None, MOD_ROWS, D_MODEL), lambda l, j: (l, 0, j)),
        compiler_params=pltpu.CompilerParams(
            dimension_semantics=("arbitrary", "arbitrary"),
            vmem_limit_bytes=32 * 1024 * 1024),
        name="adaln_modulation",
    )(c_all, w_ada, b_ada.reshape(depth, 1, -1))


def _per_head_copies(win, head_masks):
    return jnp.concatenate([jnp.where(m, win, jnp.zeros_like(win)) for m in head_masks], axis=0)


def _block_scores(q_blk, k_heads):
    qs = (q_blk * HEAD_DIM ** -0.5).astype(BF16)
    q_rows = jnp.concatenate([qs[:, r * KV_DIM:(r + 1) * KV_DIM] for r in range(Q_REP)], axis=0)
    return lax.dot_general(q_rows, k_heads, (((1,), (1,)), ((), ())), preferred_element_type=F32)


def _block_probs(s, bias, sink_ref):
    tq, keys = bias.shape
    rows = []
    for r in range(Q_REP):
        cols = []
        for g in range(N_KV_HEADS):
            sp = s[r * tq:(r + 1) * tq, g * keys:(g + 1) * keys] + bias
            cols.append(_softmax_with_sink(sp, sink_ref[g * Q_REP + r]).astype(BF16))
        rows.append(jnp.concatenate(cols, axis=1))
    return jnp.concatenate(rows, axis=0)


def _block_output(p, v_heads):
    tq = p.shape[0] // Q_REP
    o = _dot(p, v_heads)
    return jnp.concatenate([o[r * tq:(r + 1) * tq] for r in range(Q_REP)], axis=1)


def _mixer_prompt_kernel(x_ref, mod_ref, rope_ref, rope0_ref, sink_ref, n1g_ref, qkg_ref, caw_ref, v512_ref,
                         wpair_ref, bsp_ref, segm_ref,
                         wq_ref, w_in_ref, w_ao_ref, w_aout_ref, w_bout_ref, w_cout_ref, w_o_ref,
                         x1_ref, ca_ref, cb_ref, ko_ref, vo_ref,
                         kprev_s, vprev_s, aprev_s, bprev_s, acat_s, bcat_s, bias_s):
    tile = x_ref.shape[0]
    j = pl.program_id(1)
    last = j == pl.num_programs(1) - 1
    rd = j & 1
    wr = 1 - rd

    @pl.when(j == 0)
    def _():
        kprev_s[0] = jnp.zeros((ATTN_BLOCK, KV_DIM), BF16)
        vprev_s[0] = jnp.zeros((ATTN_BLOCK, KV_DIM), BF16)
        aprev_s[0] = jnp.zeros((A_PAD, CONV_W), F32)
        bprev_s[0] = jnp.zeros((B_PAD, CONV_W), F32)

    x = x_ref[...]
    shift1, scale1, gate1 = mod_ref[0:1, :], mod_ref[1:2, :], mod_ref[2:3, :]
    h = _rms_mod(x, n1g_ref[...], scale1, shift1).astype(BF16)

    def gate_logits(branch):
        lo = G0 + branch * D_MODEL
        return _dot(h, w_in_ref[:, lo:lo + D_MODEL])

    q_raw = _dot(h, wq_ref[...])
    kv_raw = _dot(h, w_in_ref[:, K0:A0])
    a_in = _dot(h, w_in_ref[:, A0:A0 + 2 * CONV_W])
    cos_l, sin_l = rope_ref[0], rope_ref[1]
    cos_0, sin_0 = rope0_ref[0:1, :], rope0_ref[1:2, :]
    cos_t, sin_a, sin_b = _rope_lane_tables(cos_0 * cos_l - sin_0 * sin_l,
                                            sin_0 * cos_l + cos_0 * sin_l)
    q, k, v = _qkv_heads(q_raw, kv_raw, qkg_ref, segm_ref[...], cos_t, sin_a, sin_b)
    kcat = jnp.concatenate([kprev_s[rd], k.astype(BF16)], axis=0)
    vcat = jnp.concatenate([vprev_s[rd], v.astype(BF16)], axis=0)

    @pl.when(last)
    def _():
        ko_ref[...] = k[tile - WINDOW:tile]
        vo_ref[...] = v[tile - WINDOW:tile]

    tt = lax.broadcasted_iota(jnp.int32, (ATTN_BLOCK, KEY_SPAN), 0)
    jj = lax.broadcasted_iota(jnp.int32, (ATTN_BLOCK, KEY_SPAN), 1)
    dist = jj - tt
    bias_rest = jnp.where((dist >= 0) & (dist <= WINDOW), 0.0, NEG).astype(F32)

    @pl.when(j == 0)
    def _():
        bias_s[...] = jnp.where(jj < ATTN_BLOCK, NEG, bias_rest)

    @pl.when(j == 1)
    def _():
        bias_s[...] = bias_rest

    bias_first = bias_s[...]
    lane = lax.broadcasted_iota(jnp.int32, (1, KV_DIM), 1)
    head_masks = [(lane >= g * HEAD_DIM) & (lane < (g + 1) * HEAD_DIM) for g in range(N_KV_HEADS)]
    kprev_s[wr] = kcat[tile:tile + ATTN_BLOCK]
    vprev_s[wr] = vcat[tile:tile + ATTN_BLOCK]
    n_blocks = tile // ATTN_BLOCK

    def scores(i):
        lo = i * ATTN_BLOCK
        return _block_scores(q[lo:lo + ATTN_BLOCK],
                             _per_head_copies(kcat[lo:lo + KEY_SPAN], head_masks))

    glu = a_in[:, :CONV_W] * jax.nn.sigmoid(a_in[:, CONV_W:])
    acat_s[0:A_PAD, :] = aprev_s[rd]
    acat_s[A_PAD:A_PAD + tile, :] = glu
    aprev_s[wr] = glu[tile - A_PAD:tile]
    base = A_PAD - (CONF_CONV_W - 1)

    @pl.when(last)
    def _():
        ca_ref[...] = acat_s[base + tile:A_PAD + tile, :]

    n_chunks = tile // CONV_ROWS
    conv_rows = []

    def conv_chunks(lo, hi):
        for c in range(lo, hi):
            conv_rows.append(_causal_conv_rows(acat_s, c * CONV_ROWS, CONV_ROWS, caw_ref,
                                               CONF_CONV_W, base, v512_ref[0:1, :]))

    fillers = [lambda: _dot(h, w_in_ref[:, B0:B0 + 3 * CONV_W]),
               lambda: _dot(h, w_in_ref[:, C0:C0 + 2 * CONV_W]),
               lambda: gate_logits(0), lambda: gate_logits(1)]
    filled = []
    blocks = []
    p = None
    for i in range(n_blocks + 1):
        if i > 0:
            lo = (i - 1) * ATTN_BLOCK
            blocks.append(_block_output(p, _per_head_copies(vcat[lo:lo + KEY_SPAN], head_masks)))
        if i < n_blocks:
            s_blk = scores(i)
        if i < len(fillers):
            filled.append(fillers[i]())
        if i == n_blocks:
            for f in fillers[len(filled):]:
                filled.append(f())
        if i < n_blocks:
            p = _block_probs(s_blk, bias_first if i == 0 else bias_rest, sink_ref)
            conv_chunks(i, i + 1)
    b_in, c_in, g0, g1 = filled
    conv_chunks(n_blocks, n_chunks)
    attn = jnp.concatenate(blocks, axis=0).astype(BF16)
    a_conv = jnp.concatenate(conv_rows, axis=0)
    a_act = jax.nn.silu(_layer_norm(a_conv, v512_ref[1:2, :], v512_ref[2:3, :])).astype(BF16)
    attn_out = _dot(attn, w_ao_ref[...])
    a_out = _dot(a_act, w_aout_ref[...])
    g2 = gate_logits(2)

    ch = b_in[:, CONV_W:2 * CONV_W] * b_in[:, 2 * CONV_W:]
    bcat_s[0:B_PAD, :] = bprev_s[rd]
    bcat_s[B_PAD:B_PAD + tile, :] = ch
    bprev_s[wr] = ch[tile - B_PAD:tile]
    base_b = B_PAD - (SC_CONV_W - 1)
    sc = None
    for t in range(SC_CONV_W):
        term = v512_ref[5 + t:6 + t, :] * bcat_s[base_b + t:base_b + t + tile, :]
        sc = term if sc is None else sc + term

    @pl.when(last)
    def _():
        cb_ref[...] = bcat_s[base_b + tile:B_PAD + tile, :]

    merged = jax.nn.sigmoid(g0) * attn_out
    merged = merged + jax.nn.sigmoid(g1) * a_out

    zc = jax.nn.gelu(c_in)
    u = zc[:, :CONV_W]
    vv = _layer_norm(zc[:, CONV_W:], v512_ref[3:4, :], v512_ref[4:5, :])
    row = lax.broadcasted_iota(jnp.int32, (GMLP_CHUNK, 2 * GMLP_CHUNK), 0)
    col = lax.broadcasted_iota(jnp.int32, (GMLP_CHUNK, 2 * GMLP_CHUNK), 1)
    causal = (col & (GMLP_CHUNK - 1)) <= row
    n_pairs = GMLP_GROUPS // 2
    wpair = [jnp.where(causal, wpair_ref[p], 0.0).astype(BF16) for p in range(n_pairs)]
    low = lax.broadcasted_iota(jnp.int32, (1, LANES), 1) < GMLP_GROUP_DIM
    rhs_blocks = []
    for c in range(tile // GMLP_CHUNK):
        vch = vv[c * GMLP_CHUNK:(c + 1) * GMLP_CHUNK]
        for pair in range(n_pairs):
            blk = vch[:, pair * LANES:(pair + 1) * LANES]
            rhs = jnp.concatenate([jnp.where(low, blk, 0.0), jnp.where(low, 0.0, blk)], axis=0)
            rhs_blocks.append(rhs.astype(BF16))
    b_act = (b_in[:, :CONV_W] * sc).astype(BF16)
    b_out = _dot(b_act, w_bout_ref[...])
    zs = [_dot(wpair[i % n_pairs], rhs) for i, rhs in enumerate(rhs_blocks)]
    g3 = gate_logits(3)
    gated = []
    for c in range(tile // GMLP_CHUNK):
        z = jnp.concatenate(zs[c * n_pairs:(c + 1) * n_pairs], axis=1) + bsp_ref[...]
        gated.append(u[c * GMLP_CHUNK:(c + 1) * GMLP_CHUNK] * z)
    gated = jnp.concatenate(gated, axis=0).astype(BF16)
    merged = merged + jax.nn.sigmoid(g2) * b_out
    c_out = _dot(gated, w_cout_ref[...])
    merged = (merged + jax.nn.sigmoid(g3) * c_out).astype(BF16)
    x1_ref[...] = x + gate1 * _dot(merged, w_o_ref[...])


def _const_spec(shape):
    zeros = (0,) * len(shape)
    return pl.BlockSpec(shape, lambda *_: zeros, pipeline_mode=pl.Buffered(1))


def _layer_spec(stacked, layer, cols=None):
    if cols is None:
        block, col_block = stacked.shape[1:], 0
    else:
        start, width = cols
        assert start % width == 0
        block, col_block = stacked.shape[1:-1] + (width,), start // width
    index = (layer,) + (0,) * (len(block) - 1) + (col_block,)
    return pl.BlockSpec((None,) + block, lambda *_: index, pipeline_mode=pl.Buffered(1))


def _mod_rows_spec(layer, nseq):
    return pl.BlockSpec((None, nseq, 6, D_MODEL), lambda i: (layer, i, 0, 0))


def _mod_sequence_spec(layer, row0):
    return pl.BlockSpec((None, None, 6, D_MODEL), lambda b, j: (layer, row0 + b, 0, 0))


def _mixer_prompt(x, mod, mod_row0, rope_local, rope_tile, lw, layer):
    bsz, seq, _ = x.shape
    tile = PROMPT_TILE
    small = [lw["n1g"], lw["qkg"], lw["caw"], lw["v512"], lw["wpair"], lw["bsp"], lw["segm"]]
    weights = [lw["wq"], lw["w_in"], lw["w_ao"], lw["w_aout"], lw["w_bout"], lw["w_cout"],
               lw["w_o"]]
    def per_sequence(r, c):
        return pl.BlockSpec((None, r, c), lambda b, j: (b, 0, 0))

    in_specs = (
        [pl.BlockSpec((None, tile, D_MODEL), lambda b, j: (b, j, 0)),
         _mod_sequence_spec(layer, mod_row0),
         _const_spec(rope_local.shape),
         pl.BlockSpec((None, 2, LANES), lambda b, j: (j, 0, 0)),
         pl.BlockSpec(memory_space=pltpu.SMEM)]
        + [_layer_spec(a, layer) for a in small]
        + [_layer_spec(a, layer) for a in weights])
    out_shape = (
        jax.ShapeDtypeStruct((bsz, seq, D_MODEL), F32),
        jax.ShapeDtypeStruct((bsz, CONF_CONV_W - 1, CONV_W), F32),
        jax.ShapeDtypeStruct((bsz, SC_CONV_W - 1, CONV_W), F32),
        jax.ShapeDtypeStruct((bsz, WINDOW, KV_DIM), F32),
        jax.ShapeDtypeStruct((bsz, WINDOW, KV_DIM), F32),
    )
    out_specs = (
        pl.BlockSpec((None, tile, D_MODEL), lambda b, j: (b, j, 0)),
        per_sequence(CONF_CONV_W - 1, CONV_W), per_sequence(SC_CONV_W - 1, CONV_W),
        per_sequence(WINDOW, KV_DIM), per_sequence(WINDOW, KV_DIM),
    )
    return pl.pallas_call(
        _mixer_prompt_kernel,
        out_shape=out_shape,
        grid=(bsz, seq // tile),
        in_specs=in_specs,
        out_specs=out_specs,
        scratch_shapes=[
            pltpu.VMEM((2, ATTN_BLOCK, KV_DIM), BF16),
            pltpu.VMEM((2, ATTN_BLOCK, KV_DIM), BF16),
            pltpu.VMEM((2, A_PAD, CONV_W), F32),
            pltpu.VMEM((2, B_PAD, CONV_W), F32),
            pltpu.VMEM((A_PAD + tile, CONV_W), F32),
            pltpu.VMEM((B_PAD + tile, CONV_W), F32),
            pltpu.VMEM((ATTN_BLOCK, KEY_SPAN), F32),
        ],
        compiler_params=pltpu.CompilerParams(
            dimension_semantics=("arbitrary", "arbitrary"),
            vmem_limit_bytes=VMEM_BYTES - 4 * 1024 * 1024),
        name="mixer_prompt",
    )(x, mod, rope_local, rope_tile, lw["sinks"][layer], *small, *weights)


SAMPLE_NEW_ROWS = 16


def _sample_attn_kernel(x_ref, mod_ref, rope_ref, sink_ref, n1g_ref, qkg_ref, segm_ref,
                        ck_ref, cv_ref, wq_ref, wkv_ref, w_ao_ref,
                        ao_ref, nk_ref, nv_ref, kcat_s, vcat_s):
    nseq, tq, _ = x_ref.shape
    rows = nseq * tq
    cache_lo = SAMPLE_NEW_ROWS
    cache_hi = cache_lo + WINDOW

    x = x_ref[...]
    shift1, scale1 = mod_ref[:, 0:1, :], mod_ref[:, 1:2, :]
    h = _rms_mod(x, n1g_ref[...], scale1, shift1).reshape(rows, D_MODEL).astype(BF16)

    q, k, v = _qkv_heads(_dot(h, wq_ref[...]), _dot(h, wkv_ref[...]), qkg_ref, segm_ref[...],
                         *_rope_lane_tables(rope_ref[0], rope_ref[1]))
    k3 = k.reshape(nseq, tq, KV_DIM)
    v3 = v.reshape(nseq, tq, KV_DIM)
    keep = WINDOW - tq
    nk_ref[:, 0:keep, :] = ck_ref[:, tq:WINDOW, :]
    nk_ref[:, keep:WINDOW, :] = k3
    nv_ref[:, 0:keep, :] = cv_ref[:, tq:WINDOW, :]
    nv_ref[:, keep:WINDOW, :] = v3
    pad = jnp.zeros((nseq, SAMPLE_NEW_ROWS - tq, KV_DIM), F32)
    kcat_s[:, 0:cache_lo, :] = jnp.concatenate([k3, pad], axis=1).astype(BF16)
    vcat_s[:, 0:cache_lo, :] = jnp.concatenate([v3, pad], axis=1).astype(BF16)
    kcat_s[:, cache_lo:cache_hi, :] = ck_ref[...].astype(BF16)
    vcat_s[:, cache_lo:cache_hi, :] = cv_ref[...].astype(BF16)
    tail = jnp.zeros((nseq, KEY_SPAN - cache_hi, KV_DIM), BF16)
    kcat_s[:, cache_hi:KEY_SPAN, :] = tail
    vcat_s[:, cache_hi:KEY_SPAN, :] = tail

    n_q = N_HEADS * tq
    tt = lax.broadcasted_iota(jnp.int32, (n_q, KEY_SPAN), 0) & (tq - 1)
    ii = lax.broadcasted_iota(jnp.int32, (n_q, KEY_SPAN), 1)
    visible = ((ii < tq) & (ii <= tt)) | ((ii >= cache_lo + tt) & (ii < cache_hi))
    bias = jnp.where(visible, 0.0, NEG).astype(F32)
    cm_scaled = _kv_head_masks(HEAD_DIM ** -0.5)
    cm_plain = _kv_head_masks(1.0)
    q3 = q.reshape(nseq, tq, D_MODEL)
    qbd = jnp.concatenate([q3[:, :, r * KV_DIM:(r + 1) * KV_DIM] * cm_scaled[g]
                           for r in range(Q_REP) for g in range(N_KV_HEADS)], axis=1).astype(BF16)
    s = jnp.einsum("bqc,bkc->bqk", qbd, kcat_s[...], preferred_element_type=F32) + bias
    sink = jnp.concatenate([jnp.full((tq, 1), sink_ref[g * Q_REP + r], F32)
                            for r in range(Q_REP) for g in range(N_KV_HEADS)], axis=0)
    p = _softmax_with_sink(s, sink).astype(BF16)
    o = jnp.einsum("bqk,bkc->bqc", p, vcat_s[...], preferred_element_type=F32)
    slabs = []
    for r in range(Q_REP):
        acc = None
        for g in range(N_KV_HEADS):
            i = r * N_KV_HEADS + g
            part = o[:, i * tq:(i + 1) * tq, :] * cm_plain[g]
            acc = part if acc is None else acc + part
        slabs.append(acc)
    attn = jnp.concatenate(slabs, axis=2).reshape(rows, D_MODEL)
    ao_ref[...] = _dot(attn.astype(BF16), w_ao_ref[...]).reshape(nseq, tq, D_MODEL)


def _sample_rest_kernel(x_ref, mod_ref, ao_ref, n1g_ref, caw_ref, v512_ref, wc_ref, bsp_ref,
                        sta_ref, stb_ref, w_in_ref, w_aout_ref, w_bout_ref, w_cout_ref, w_o_ref,
                        x1_ref, na_ref, nb_ref, vv_ref, glu_s, aact_s, bcat_s):
    nseq, tq, _ = x_ref.shape
    rows = nseq * tq

    x = x_ref[...]
    shift1, scale1, gate1 = mod_ref[:, 0:1, :], mod_ref[:, 1:2, :], mod_ref[:, 2:3, :]
    h = _rms_mod(x, n1g_ref[...], scale1, shift1).reshape(rows, D_MODEL).astype(BF16)
    attn_out = ao_ref[...].reshape(rows, D_MODEL)
    merged = jax.nn.sigmoid(_dot(h, w_in_ref[:, G0:G0 + D_MODEL])) * attn_out

    a_in = _dot(h, w_in_ref[:, A0:A0 + 2 * CONV_W])
    glu = a_in[:, :CONV_W] * jax.nn.sigmoid(a_in[:, CONV_W:])
    hist_a = CONF_CONV_W - 1
    glu_s[...] = glu.reshape(nseq, tq, CONV_W)

    def conv_in(t):
        return sta_ref[t] if t < hist_a else glu_s[:, t - hist_a, :]

    for t in range(tq):
        acc = jnp.broadcast_to(v512_ref[0:1, :], (nseq, CONV_W))
        for tap in range(CONF_CONV_W):
            acc = acc + caw_ref[tap:tap + 1, :] * conv_in(t + tap)
        aact_s[:, t, :] = jax.nn.silu(_layer_norm(acc, v512_ref[1:2, :], v512_ref[2:3, :]))
    for t in range(hist_a):
        na_ref[t] = conv_in(t + tq)
    a_out = _dot(aact_s[...].reshape(rows, CONV_W).astype(BF16), w_aout_ref[...])
    merged = merged + jax.nn.sigmoid(_dot(h, w_in_ref[:, G0 + D_MODEL:G0 + 2 * D_MODEL])) * a_out

    b_in = _dot(h, w_in_ref[:, B0:B0 + 3 * CONV_W])
    hist_b = SC_CONV_W - 1
    base_b = B_PAD - hist_b
    bcat_s[:, base_b:B_PAD, :] = stb_ref[...]
    bcat_s[:, B_PAD:B_PAD + tq, :] = (b_in[:, CONV_W:2 * CONV_W]
                                      * b_in[:, 2 * CONV_W:]).reshape(nseq, tq, CONV_W)
    sc = None
    for t in range(SC_CONV_W):
        term = v512_ref[5 + t:6 + t, :] * bcat_s[:, base_b + t:base_b + t + tq, :]
        sc = term if sc is None else sc + term
    nb_ref[...] = bcat_s[:, base_b + tq:B_PAD + tq, :]
    b_out = _dot((b_in[:, :CONV_W] * sc.reshape(rows, CONV_W)).astype(BF16), w_bout_ref[...])
    merged = merged + jax.nn.sigmoid(_dot(h, w_in_ref[:, G0 + 2 * D_MODEL:G0 + 3 * D_MODEL])) * b_out

    zc = jax.nn.gelu(_dot(h, w_in_ref[:, C0:C0 + 2 * CONV_W]))
    u3 = zc[:, :CONV_W].reshape(nseq, tq, CONV_W)
    vv3 = _layer_norm(zc[:, CONV_W:], v512_ref[3:4, :], v512_ref[4:5, :]).reshape(nseq, tq, CONV_W)
    vv_ref[...] = vv3
    t_idx = lax.broadcasted_iota(jnp.int32, (tq, CONV_W), 0)
    z = jnp.broadcast_to(bsp_ref[...], (nseq, tq, CONV_W))
    for src in range(tq):
        coef = jnp.where(t_idx >= src, wc_ref[src], 0.0)
        z = z + coef * vv3[:, src:src + 1, :]
    c_out = _dot((u3 * z).reshape(rows, CONV_W).astype(BF16), w_cout_ref[...])
    merged = merged + jax.nn.sigmoid(_dot(h, w_in_ref[:, G0 + 3 * D_MODEL:G0 + 4 * D_MODEL])) * c_out

    y = _dot(merged.astype(BF16), w_o_ref[...]).reshape(nseq, tq, D_MODEL)
    x1_ref[...] = x + gate1 * y


def _seq_spec(nseq, r, c):
    return pl.BlockSpec((nseq, r, c), lambda i: (i, 0, 0))


def _sample_attention(x, mod, rope, cache_k, cache_v, lw, layer):
    nbatch, tq, _ = x.shape
    nseq = SAMPLE_ATTN_SEQS
    small = [lw["n1g"], lw["qkg"], lw["segm"]]

    def cache_spec():
        return pl.BlockSpec((None, nseq, WINDOW, KV_DIM), lambda i: (layer, i, 0, 0))

    in_specs = (
        [_seq_spec(nseq, tq, D_MODEL), _mod_rows_spec(layer, nseq),
         _const_spec(rope.shape), pl.BlockSpec(memory_space=pltpu.SMEM)]
        + [_layer_spec(a, layer) for a in small]
        + [cache_spec(), cache_spec(),
           _layer_spec(lw["wq"], layer), _layer_spec(lw["w_in"], layer, cols=(K0, 2 * KV_DIM)),
           _layer_spec(lw["w_ao"], layer)])
    out_shape = (
        jax.ShapeDtypeStruct((nbatch, tq, D_MODEL), F32),
        jax.ShapeDtypeStruct((nbatch, WINDOW, KV_DIM), F32),
        jax.ShapeDtypeStruct((nbatch, WINDOW, KV_DIM), F32),
    )
    out_specs = (_seq_spec(nseq, tq, D_MODEL), _seq_spec(nseq, WINDOW, KV_DIM),
                 _seq_spec(nseq, WINDOW, KV_DIM))
    return pl.pallas_call(
        _sample_attn_kernel,
        out_shape=out_shape,
        grid=(nbatch // nseq,),
        in_specs=in_specs,
        out_specs=out_specs,
        scratch_shapes=[
            pltpu.VMEM((nseq, KEY_SPAN, KV_DIM), BF16),
            pltpu.VMEM((nseq, KEY_SPAN, KV_DIM), BF16),
        ],
        compiler_params=pltpu.CompilerParams(
            dimension_semantics=("arbitrary",),
            vmem_limit_bytes=VMEM_BYTES - 16 * 1024 * 1024),
        name="sample_attention",
    )(x, mod, rope, lw["sinks"][layer], *small, cache_k, cache_v, lw["wq"], lw["w_in"], lw["w_ao"])


def _sample_rest(x, mod, attn_out, state_a, state_b, lw, layer):
    nbatch, tq, _ = x.shape
    nseq = SAMPLE_SEQS
    small = [lw["n1g"], lw["caw"], lw["v512"], lw["wc8"], lw["bsp8"]]
    weights = [lw["w_in"], lw["w_aout"], lw["w_bout"], lw["w_cout"], lw["w_o"]]
    hist_a = CONF_CONV_W - 1
    in_specs = (
        [_seq_spec(nseq, tq, D_MODEL), _mod_rows_spec(layer, nseq), _seq_spec(nseq, tq, D_MODEL)]
        + [_layer_spec(a, layer) for a in small]
        + [pl.BlockSpec((None, hist_a, nseq, CONV_W), lambda i: (layer, 0, i, 0)),
           pl.BlockSpec((None, nseq, SC_CONV_W - 1, CONV_W), lambda i: (layer, i, 0, 0))]
        + [_layer_spec(a, layer) for a in weights])
    out_shape = (
        jax.ShapeDtypeStruct((nbatch, tq, D_MODEL), F32),
        jax.ShapeDtypeStruct((hist_a, nbatch, CONV_W), F32),
        jax.ShapeDtypeStruct((nbatch, SC_CONV_W - 1, CONV_W), F32),
        jax.ShapeDtypeStruct((nbatch, tq, CONV_W), F32),
    )
    out_specs = (_seq_spec(nseq, tq, D_MODEL),
                 pl.BlockSpec((hist_a, nseq, CONV_W), lambda i: (0, i, 0)),
                 _seq_spec(nseq, SC_CONV_W - 1, CONV_W), _seq_spec(nseq, tq, CONV_W))
    return pl.pallas_call(
        _sample_rest_kernel,
        out_shape=out_shape,
        grid=(nbatch // nseq,),
        in_specs=in_specs,
        out_specs=out_specs,
        scratch_shapes=[
            pltpu.VMEM((nseq, tq, CONV_W), F32),
            pltpu.VMEM((nseq, tq, CONV_W), F32),
            pltpu.VMEM((nseq, B_PAD + tq, CONV_W), F32),
        ],
        compiler_params=pltpu.CompilerParams(
            dimension_semantics=("arbitrary",),
            vmem_limit_bytes=VMEM_BYTES - 8 * 1024 * 1024),
        name="sample_rest",
    )(x, mod, attn_out, *small, state_a, state_b, *weights)


def _mlp_kernel(x_ref, mod_ref, g_ref, w_up_ref, w_down_ref, o_ref):
    x = x_ref[...]
    if x.ndim == 3:
        shift2, scale2, gate2 = mod_ref[:, 3:4, :], mod_ref[:, 4:5, :], mod_ref[:, 5:6, :]
    else:
        shift2, scale2, gate2 = mod_ref[3:4, :], mod_ref[4:5, :], mod_ref[5:6, :]
    h2 = _rms_mod(x, g_ref[...], scale2, shift2)
    up = _dot(h2.reshape(-1, D_MODEL).astype(BF16), w_up_ref[...])
    act = jnp.square(jnp.maximum(up, 0.0)).astype(BF16)
    y = _dot(act, w_down_ref[...]).reshape(x.shape)
    o_ref[...] = x + gate2 * y


def _mlp(x, mod, mod_row0, lw, layer, per_sequence):
    if per_sequence:
        nbatch, tq, _ = x.shape
        nseq = MLP_TILE // tq
        grid = (nbatch // nseq,)
        x_spec = pl.BlockSpec((nseq, tq, D_MODEL), lambda i: (i, 0, 0))
        mod_spec = _mod_rows_spec(layer, nseq)
    else:
        bsz, seq, _ = x.shape
        grid = (bsz, seq // MLP_TILE)
        x_spec = pl.BlockSpec((None, MLP_TILE, D_MODEL), lambda b, j: (b, j, 0))
        mod_spec = _mod_sequence_spec(layer, mod_row0)
    return pl.pallas_call(
        _mlp_kernel,
        out_shape=jax.ShapeDtypeStruct(x.shape, F32),
        grid=grid,
        in_specs=[x_spec, mod_spec, _layer_spec(lw["n2g"], layer),
                  _layer_spec(lw["w_up"], layer), _layer_spec(lw["w_down"], layer)],
        out_specs=x_spec,
        compiler_params=pltpu.CompilerParams(
            dimension_semantics=("arbitrary",) * len(grid),
            vmem_limit_bytes=VMEM_BYTES - 4 * 1024 * 1024),
        name="mlp_sample" if per_sequence else "mlp_prompt",
    )(x, mod, lw["n2g"], lw["w_up"], lw["w_down"])


def _rope_cos_sin(pos):
    half = ROT_DIM // 2
    freq = (jnp.arange(LANES) % half).astype(F32)
    inv_freq = jnp.power(ROPE_THETA, -freq * 2.0 / ROT_DIM)
    ang = pos.astype(F32)[:, None] * inv_freq[None, :]
    return jnp.stack([jnp.cos(ang), jnp.sin(ang)])


def _rope_lane_tables(cos, sin):
    half = ROT_DIM // 2
    lane = lax.broadcasted_iota(jnp.int32, (1, LANES), 1) & (HEAD_DIM - 1)
    cos_t = jnp.where(lane < ROT_DIM, cos, 1.0)
    sin_a = jnp.where(lane < half, -sin, 0.0)
    sin_b = jnp.where((lane >= half) & (lane < ROT_DIM), sin, 0.0)
    return cos_t, sin_a, sin_b


def _prepare_weights(p):
    depth = p["w_in"].shape[0]
    w_in = p["w_in"]
    wq = w_in[:, :, :D_MODEL].reshape(depth, D_MODEL, N_KV_HEADS, Q_REP, HEAD_DIM)
    wq = wq.transpose(0, 1, 3, 2, 4).reshape(depth, D_MODEL, D_MODEL).astype(BF16)
    w_ao = p["w_attn_o"].reshape(depth, N_KV_HEADS, Q_REP, HEAD_DIM, D_MODEL)
    w_ao = w_ao.transpose(0, 2, 1, 3, 4).reshape(depth, D_MODEL, D_MODEL).astype(BF16)
    ws = p["w_spatial"]
    n_pairs = GMLP_GROUPS // 2
    wpair = ws.reshape(depth, n_pairs, 2, GMLP_CHUNK, GMLP_CHUNK).transpose(0, 1, 3, 2, 4)
    wpair = wpair.reshape(depth, n_pairs, GMLP_CHUNK, 2 * GMLP_CHUNK)
    bsp = jnp.repeat(p["b_spatial"].transpose(0, 2, 1), GMLP_GROUP_DIM, axis=2)
    wc8 = jnp.repeat(ws[:, :, :DEC_SEQ, :DEC_SEQ].transpose(0, 3, 2, 1),
                     GMLP_GROUP_DIM, axis=3)
    lane_head = jnp.arange(LANES) // HEAD_DIM
    segm = jnp.where(lane_head[:, None] == lane_head[None, :], 1.0 / HEAD_DIM, 0.0).astype(BF16)
    caw = jnp.concatenate([p["conv_a_w"], jnp.zeros((depth, 1, CONV_W), F32)], axis=1)
    v512 = jnp.concatenate([p["conv_a_b"][:, None], p["ln_a_g"][:, None], p["ln_a_b"][:, None],
                            p["ln_c_g"][:, None], p["ln_c_b"][:, None], p["conv_b_w"]], axis=1)
    reps = LANES // HEAD_DIM
    return {
        "wq": wq, "w_in": w_in.astype(BF16), "w_ao": w_ao,
        "w_aout": p["w_a_out"].astype(BF16), "w_bout": p["w_b_out"].astype(BF16),
        "w_cout": p["w_c_out"].astype(BF16), "w_o": p["w_o"].astype(BF16),
        "w_up": p["w_up"].astype(BF16), "w_down": p["w_down"].astype(BF16),
        "n1g": p["norm1_g"][:, None], "n2g": p["norm2_g"][:, None],
        "qkg": jnp.stack([jnp.tile(p["q_norm_g"], (1, reps)),
                          jnp.tile(p["k_norm_g"], (1, reps))], axis=1),
        "caw": caw, "v512": v512, "wpair": wpair, "bsp": bsp, "wc8": wc8,
        "bsp8": bsp[:, :DEC_SEQ],
        "segm": jnp.broadcast_to(segm, (depth, LANES, LANES)), "sinks": p["attn_sinks"],
    }


def kernel(x_prompt, x_sample, c_prompt, c_sample, state_conv_a, state_conv_b, cache_k_win, cache_v_win, w_ada, b_ada, norm1_g, norm2_g, w_in, q_norm_g, k_norm_g, attn_sinks, w_attn_o, conv_a_w, conv_a_b, ln_a_g, ln_a_b, w_a_out, conv_b_w, w_b_out, ln_c_g, ln_c_b, w_spatial, b_spatial, w_c_out, w_o, w_up, w_down):
    params = dict(w_in=w_in, w_attn_o=w_attn_o, w_spatial=w_spatial, b_spatial=b_spatial,
                  conv_a_w=conv_a_w, conv_a_b=conv_a_b, ln_a_g=ln_a_g, ln_a_b=ln_a_b,
                  ln_c_g=ln_c_g, ln_c_b=ln_c_b, conv_b_w=conv_b_w, w_a_out=w_a_out,
                  w_b_out=w_b_out, w_c_out=w_c_out, w_o=w_o, w_up=w_up, w_down=w_down,
                  norm1_g=norm1_g, norm2_g=norm2_g, q_norm_g=q_norm_g, k_norm_g=k_norm_g,
                  attn_sinks=attn_sinks)
    depth = w_in.shape[0]
    bp, seq, _ = x_prompt.shape
    bs, tq, _ = x_sample.shape

    c_all = jnp.concatenate([c_sample, c_prompt,
                             jnp.zeros((MOD_ROWS - bp - bs, D_MODEL), F32)], axis=0)
    mod = _modulation(c_all, w_ada, b_ada).reshape(depth, MOD_ROWS, 6, D_MODEL)
    rope_local = _rope_cos_sin(jnp.arange(PROMPT_TILE))
    rope_tile = _rope_cos_sin(jnp.arange(0, seq, PROMPT_TILE)).transpose(1, 0, 2)
    rope_s = jnp.tile(_rope_cos_sin(PAST_LEN + jnp.arange(tq)), (1, SAMPLE_ATTN_SEQS, 1))
    state_a_tm = state_conv_a.transpose(0, 2, 1, 3)
    cache_k = cache_k_win.reshape(depth, bs, WINDOW, KV_DIM)
    cache_v = cache_v_win.reshape(depth, bs, WINDOW, KV_DIM)

    xp, xs = x_prompt, x_sample
    outs = [[] for _ in range(9)]
    lw = _prepare_weights(params)
    for l in range(depth):
        xp, pa, pb, pk, pv = _mixer_prompt(xp, mod, bs, rope_local, rope_tile, lw, l)
        xp = _mlp(xp, mod, bs, lw, l, per_sequence=False)
        s_attn, sk, sv = _sample_attention(xs, mod, rope_s, cache_k, cache_v, lw, l)
        xs, sa, sb, sg = _sample_rest(xs, mod, s_attn, state_a_tm, state_conv_b, lw, l)
        sa = sa.transpose(1, 0, 2)
        xs = _mlp(xs, mod, bs, lw, l, per_sequence=True)
        kv_shape_p = (bp, WINDOW, N_KV_HEADS, HEAD_DIM)
        kv_shape_s = (bs, WINDOW, N_KV_HEADS, HEAD_DIM)
        for lst, val in zip(outs, (pa, sa, pb, sb, pk.reshape(kv_shape_p), sk.reshape(kv_shape_s),
                                   pv.reshape(kv_shape_p), sv.reshape(kv_shape_s), sg)):
            lst.append(val)
    return (xp, xs) + tuple(jnp.stack(o) for o in outs)
```

```python
import jax
import jax.numpy as jnp
from jax import lax
from jax.experimental import pallas as pl
from jax.experimental.pallas import tpu as pltpu

F32 = jnp.float32
BF16 = jnp.bfloat16

D_MODEL = 1024
HEAD_DIM = 64
N_HEADS = 16
N_KV_HEADS = 4
Q_REP = 4
KV_DIM = N_KV_HEADS * HEAD_DIM
ROT_DIM = 16
ROPE_THETA = 500000.0
WINDOW = 128
ATTN_BLOCK = 128
KEY_SPAN = 2 * ATTN_BLOCK
CONV_W = 512
CONF_CONV_W = 31
SC_CONV_W = 3
GMLP_CHUNK = 128
GMLP_GROUPS = 8
GMLP_GROUP_DIM = 64
D_FF = 4 * D_MODEL
EPS = 1e-6
PAST_LEN = 16384
DEC_SEQ = 8
NEG = -1e30

K0 = D_MODEL
V0 = K0 + KV_DIM
A0 = V0 + KV_DIM
B0 = A0 + 2 * CONV_W
C0 = B0 + 3 * CONV_W
G0 = C0 + 2 * CONV_W

LANES = 128
SUBLANES = 8
VMEM_BYTES = 64 * 1024 * 1024

PROMPT_TILE = 512
MLP_TILE = 1024
SAMPLE_MLP_TILE = 512
SAMPLE_SEQS = 32
SAMPLE_ATTN_SEQS = 16
CONV_ROWS = 64
A_PAD = 32
B_PAD = 8
MOD_ROWS = 136


def _dot(a, b):
    return jnp.dot(a, b, preferred_element_type=F32)


def _rms_mod(x, g, scale, shift):
    ms = jnp.mean(x * x, axis=-1, keepdims=True)
    y = x * lax.rsqrt(ms + EPS)
    return (y * g) * (1.0 + scale) + shift


def _layer_norm(x, g, b):
    mu = jnp.mean(x, axis=-1, keepdims=True)
    xc = x - mu
    var = jnp.mean(xc * xc, axis=-1, keepdims=True)
    return xc * lax.rsqrt(var + EPS) * g + b


def _head_norm_rope(xc, gain, seg_mean, cos_t, sin_a, sin_b):
    ms = _dot((xc * xc).astype(BF16), seg_mean)
    y = xc * lax.rsqrt(ms + EPS) * gain
    return (y * cos_t + pltpu.roll(y, LANES - ROT_DIM // 2, 1) * sin_a
            + pltpu.roll(y, ROT_DIM // 2, 1) * sin_b)


def _qkv_heads(q_raw, kv_raw, qkg_ref, seg_mean, cos_t, sin_a, sin_b):
    def chunk(src, c, gain):
        return _head_norm_rope(src[:, c * LANES:(c + 1) * LANES], gain, seg_mean,
                               cos_t, sin_a, sin_b)
    q = jnp.concatenate([chunk(q_raw, c, qkg_ref[0:1, :]) for c in range(D_MODEL // LANES)],
                        axis=1)
    k = jnp.concatenate([chunk(kv_raw, c, qkg_ref[1:2, :])
                         for c in range(KV_DIM // LANES)], axis=1)
    return q, k, kv_raw[:, KV_DIM:2 * KV_DIM]


def _kv_head_masks(scale):
    lane = lax.broadcasted_iota(jnp.int32, (1, KV_DIM), 1)
    return [jnp.where((lane >= g * HEAD_DIM) & (lane < (g + 1) * HEAD_DIM), scale, 0.0).astype(F32)
            for g in range(N_KV_HEADS)]


def _softmax_with_sink(s, sink):
    m = jnp.maximum(jnp.max(s, axis=-1, keepdims=True), sink)
    e = jnp.exp(s - m)
    den = jnp.sum(e, axis=-1, keepdims=True) + jnp.exp(sink - m)
    return e * (1.0 / den)


def _causal_conv_rows(hist_ref, row0, rows, w_ref, n_taps, base, bias):
    out = jnp.broadcast_to(bias, (rows, hist_ref.shape[1]))
    for phase in range(SUBLANES):
        span = rows if phase == 0 else rows + SUBLANES
        part = None
        for tap in range(n_taps):
            offset = base + tap
            if offset % SUBLANES != phase:
                continue
            lo = row0 + offset - phase
            term = w_ref[tap:tap + 1, :] * hist_ref[lo:lo + span, :]
            part = term if part is None else part + term
        if part is not None:
            out = out + part[phase:phase + rows]
    return out


def _mod_kernel(c_ref, w_ref, b_ref, o_ref):
    c = c_ref[...]
    s = (c * jax.nn.sigmoid(c)).astype(BF16)
    o_ref[...] = _dot(s, w_ref[...].astype(BF16)) + b_ref[...]


def _modulation(c_all, w_ada, b_ada):
    depth = w_ada.shape[0]
    n_chunks = w_ada.shape[2] // D_MODEL
    return pl.pallas_call(
        _mod_kernel,
        out_shape=jax.ShapeDtypeStruct((depth, MOD_ROWS, n_chunks * D_MODEL), F32),
        grid=(depth, n_chunks),
        in_specs=[
            pl.BlockSpec((MOD_ROWS, D_MODEL), lambda l, j: (0, 0)),
            pl.BlockSpec((None, D_MODEL, D_MODEL), lambda l, j: (l, 0, j)),
            pl.BlockSpec((None, 1, D_MODEL), lambda l, j: (l, 0, j)),
        ],
        out_specs=pl.BlockSpec((None, MOD_ROWS, D_MODEL), lambda l, j: (l, 0, j)),
        compiler_params=pltpu.CompilerParams(
            dimension_semantics=("arbitrary", "arbitrary"),
            vmem_limit_bytes=32 * 1024 * 1024),
        name="adaln_modulation",
    )(c_all, w_ada, b_ada.reshape(depth, 1, -1))


def _per_head_copies(win, head_masks):
    return jnp.concatenate([jnp.where(m, win, jnp.zeros_like(win)) for m in head_masks], axis=0)


def _block_scores(q_blk, k_heads):
    qs = (q_blk * HEAD_DIM ** -0.5).astype(BF16)
    q_rows = jnp.concatenate([qs[:, r * KV_DIM:(r + 1) * KV_DIM] for r in range(Q_REP)], axis=0)
    return lax.dot_general(q_rows, k_heads, (((1,), (1,)), ((), ())), preferred_element_type=F32)


def _block_probs(s, bias, sink_ref):
    tq, keys = bias.shape
    rows = []
    for r in range(Q_REP):
        cols = []
        for g in range(N_KV_HEADS):
            sp = s[r * tq:(r + 1) * tq, g * keys:(g + 1) * keys] + bias
            cols.append(_softmax_with_sink(sp, sink_ref[g * Q_REP + r]).astype(BF16))
        rows.append(jnp.concatenate(cols, axis=1))
    return jnp.concatenate(rows, axis=0)


def _block_output(p, v_heads):
    tq = p.shape[0] // Q_REP
    o = _dot(p, v_heads)
    return jnp.concatenate([o[r * tq:(r + 1) * tq] for r in range(Q_REP)], axis=1)


def _mixer_prompt_kernel(x_ref, mod_ref, rope_ref, rope0_ref, sink_ref, n1g_ref, qkg_ref, caw_ref, v512_ref,
                         wpair_ref, bsp_ref, segm_ref,
                         wq_ref, w_in_ref, w_ao_ref, w_aout_ref, w_bout_ref, w_cout_ref, w_o_ref,
                         x1_ref, ca_ref, cb_ref, ko_ref, vo_ref,
                         kprev_s, vprev_s, aprev_s, bprev_s, acat_s, bcat_s, bias_s):
    tile = x_ref.shape[0]
    j = pl.program_id(1)
    last = j == pl.num_programs(1) - 1
    rd = j & 1
    wr = 1 - rd

    @pl.when(j == 0)
    def _():
        kprev_s[0] = jnp.zeros((ATTN_BLOCK, KV_DIM), BF16)
        vprev_s[0] = jnp.zeros((ATTN_BLOCK, KV_DIM), BF16)
        aprev_s[0] = jnp.zeros((A_PAD, CONV_W), F32)
        bprev_s[0] = jnp.zeros((B_PAD, CONV_W), F32)

    x = x_ref[...]
    shift1, scale1, gate1 = mod_ref[0:1, :], mod_ref[1:2, :], mod_ref[2:3, :]
    h = _rms_mod(x, n1g_ref[...], scale1, shift1).astype(BF16)

    def gate_logits(branch):
        lo = G0 + branch * D_MODEL
        return _dot(h, w_in_ref[:, lo:lo + D_MODEL])

    q_raw = _dot(h, wq_ref[...])
    kv_raw = _dot(h, w_in_ref[:, K0:A0])
    a_in = _dot(h, w_in_ref[:, A0:A0 + 2 * CONV_W])
    cos_l, sin_l = rope_ref[0], rope_ref[1]
    cos_0, sin_0 = rope0_ref[0:1, :], rope0_ref[1:2, :]
    cos_t, sin_a, sin_b = _rope_lane_tables(cos_0 * cos_l - sin_0 * sin_l,
                                            sin_0 * cos_l + cos_0 * sin_l)
    q, k, v = _qkv_heads(q_raw, kv_raw, qkg_ref, segm_ref[...], cos_t, sin_a, sin_b)
    kcat = jnp.concatenate([kprev_s[rd], k.astype(BF16)], axis=0)
    vcat = jnp.concatenate([vprev_s[rd], v.astype(BF16)], axis=0)

    @pl.when(last)
    def _():
        ko_ref[...] = k[tile - WINDOW:tile]
        vo_ref[...] = v[tile - WINDOW:tile]

    tt = lax.broadcasted_iota(jnp.int32, (ATTN_BLOCK, KEY_SPAN), 0)
    jj = lax.broadcasted_iota(jnp.int32, (ATTN_BLOCK, KEY_SPAN), 1)
    dist = jj - tt
    bias_rest = jnp.where((dist >= 0) & (dist <= WINDOW), 0.0, NEG).astype(F32)

    @pl.when(j == 0)
    def _():
        bias_s[...] = jnp.where(jj < ATTN_BLOCK, NEG, bias_rest)

    @pl.when(j == 1)
    def _():
        bias_s[...] = bias_rest

    bias_first = bias_s[...]
    lane = lax.broadcasted_iota(jnp.int32, (1, KV_DIM), 1)
    head_masks = [(lane >= g * HEAD_DIM) & (lane < (g + 1) * HEAD_DIM) for g in range(N_KV_HEADS)]
    kprev_s[wr] = kcat[tile:tile + ATTN_BLOCK]
    vprev_s[wr] = vcat[tile:tile + ATTN_BLOCK]
    n_blocks = tile // ATTN_BLOCK

    def scores(i):
        lo = i * ATTN_BLOCK
        return _block_scores(q[lo:lo + ATTN_BLOCK],
                             _per_head_copies(kcat[lo:lo + KEY_SPAN], head_masks))

    glu = a_in[:, :CONV_W] * jax.nn.sigmoid(a_in[:, CONV_W:])
    acat_s[0:A_PAD, :] = aprev_s[rd]
    acat_s[A_PAD:A_PAD + tile, :] = glu
    aprev_s[wr] = glu[tile - A_PAD:tile]
    base = A_PAD - (CONF_CONV_W - 1)

    @pl.when(last)
    def _():
        ca_ref[...] = acat_s[base + tile:A_PAD + tile, :]

    n_chunks = tile // CONV_ROWS
    conv_rows = []

    def conv_chunks(lo, hi):
        for c in range(lo, hi):
            conv_rows.append(_causal_conv_rows(acat_s, c * CONV_ROWS, CONV_ROWS, caw_ref,
                                               CONF_CONV_W, base, v512_ref[0:1, :]))

    fillers = [lambda: _dot(h, w_in_ref[:, B0:B0 + 3 * CONV_W]),
               lambda: _dot(h, w_in_ref[:, C0:C0 + 2 * CONV_W])]
    filled = []
    blocks = []
    p = None
    for i in range(n_blocks + 1):
        if i > 0:
            lo = (i - 1) * ATTN_BLOCK
            blocks.append(_block_output(p, _per_head_copies(vcat[lo:lo + KEY_SPAN], head_masks)))
        if i < n_blocks:
            s_blk = scores(i)
        if i < len(fillers):
            filled.append(fillers[i]())
        if i == n_blocks:
            for f in fillers[len(filled):]:
                filled.append(f())
            g0 = gate_logits(0)
            g1 = gate_logits(1)
        if i < n_blocks:
            p = _block_probs(s_blk, bias_first if i == 0 else bias_rest, sink_ref)
            conv_chunks(i, i + 1)
    b_in, c_in = filled
    conv_chunks(n_blocks, n_chunks)
    attn = jnp.concatenate(blocks, axis=0).astype(BF16)
    a_conv = jnp.concatenate(conv_rows, axis=0)
    a_act = jax.nn.silu(_layer_norm(a_conv, v512_ref[1:2, :], v512_ref[2:3, :])).astype(BF16)
    attn_out = _dot(attn, w_ao_ref[...])
    a_out = _dot(a_act, w_aout_ref[...])
    g2 = gate_logits(2)

    ch = b_in[:, CONV_W:2 * CONV_W] * b_in[:, 2 * CONV_W:]
    bcat_s[0:B_PAD, :] = bprev_s[rd]
    bcat_s[B_PAD:B_PAD + tile, :] = ch
    bprev_s[wr] = ch[tile - B_PAD:tile]
    base_b = B_PAD - (SC_CONV_W - 1)
    sc = None
    for t in range(SC_CONV_W):
        term = v512_ref[5 + t:6 + t, :] * bcat_s[base_b + t:base_b + t + tile, :]
        sc = term if sc is None else sc + term

    @pl.when(last)
    def _():
        cb_ref[...] = bcat_s[base_b + tile:B_PAD + tile, :]

    merged = jax.nn.sigmoid(g0) * attn_out
    merged = merged + jax.nn.sigmoid(g1) * a_out

    zc = jax.nn.gelu(c_in)
    u = zc[:, :CONV_W]
    vv = _layer_norm(zc[:, CONV_W:], v512_ref[3:4, :], v512_ref[4:5, :])
    row = lax.broadcasted_iota(jnp.int32, (GMLP_CHUNK, 2 * GMLP_CHUNK), 0)
    col = lax.broadcasted_iota(jnp.int32, (GMLP_CHUNK, 2 * GMLP_CHUNK), 1)
    causal = (col & (GMLP_CHUNK - 1)) <= row
    n_pairs = GMLP_GROUPS // 2
    wpair = [jnp.where(causal, wpair_ref[p], 0.0).astype(BF16) for p in range(n_pairs)]
    low = lax.broadcasted_iota(jnp.int32, (1, LANES), 1) < GMLP_GROUP_DIM
    rhs_blocks = []
    for c in range(tile // GMLP_CHUNK):
        vch = vv[c * GMLP_CHUNK:(c + 1) * GMLP_CHUNK]
        for pair in range(n_pairs):
            blk = vch[:, pair * LANES:(pair + 1) * LANES]
            rhs = jnp.concatenate([jnp.where(low, blk, 0.0), jnp.where(low, 0.0, blk)], axis=0)
            rhs_blocks.append(rhs.astype(BF16))
    b_act = (b_in[:, :CONV_W] * sc).astype(BF16)
    b_out = _dot(b_act, w_bout_ref[...])
    zs = [_dot(wpair[i % n_pairs], rhs) for i, rhs in enumerate(rhs_blocks)]
    g3 = gate_logits(3)
    gated = []
    for c in range(tile // GMLP_CHUNK):
        z = jnp.concatenate(zs[c * n_pairs:(c + 1) * n_pairs], axis=1) + bsp_ref[...]
        gated.append(u[c * GMLP_CHUNK:(c + 1) * GMLP_CHUNK] * z)
    gated = jnp.concatenate(gated, axis=0).astype(BF16)
    merged = merged + jax.nn.sigmoid(g2) * b_out
    c_out = _dot(gated, w_cout_ref[...])
    merged = (merged + jax.nn.sigmoid(g3) * c_out).astype(BF16)
    x1_ref[...] = x + gate1 * _dot(merged, w_o_ref[...])


def _const_spec(shape):
    zeros = (0,) * len(shape)
    return pl.BlockSpec(shape, lambda *_: zeros, pipeline_mode=pl.Buffered(1))


def _layer_spec(stacked, layer, cols=None):
    if cols is None:
        block, col_block = stacked.shape[1:], 0
    else:
        start, width = cols
        assert start % width == 0
        block, col_block = stacked.shape[1:-1] + (width,), start // width
    index = (layer,) + (0,) * (len(block) - 1) + (col_block,)
    return pl.BlockSpec((None,) + block, lambda *_: index, pipeline_mode=pl.Buffered(1))


def _mod_rows_spec(layer, nseq):
    return pl.BlockSpec((None, nseq, 6, D_MODEL), lambda i: (layer, i, 0, 0))


def _mod_sequence_spec(layer, row0):
    return pl.BlockSpec((None, None, 6, D_MODEL), lambda b, j: (layer, row0 + b, 0, 0))


def _mixer_prompt(x, mod, mod_row0, rope_local, rope_tile, lw, layer):
    bsz, seq, _ = x.shape
    tile = PROMPT_TILE
    small = [lw["n1g"], lw["qkg"], lw["caw"], lw["v512"], lw["wpair"], lw["bsp"], lw["segm"]]
    weights = [lw["wq"], lw["w_in"], lw["w_ao"], lw["w_aout"], lw["w_bout"], lw["w_cout"],
               lw["w_o"]]
    def per_sequence(r, c):
        return pl.BlockSpec((None, r, c), lambda b, j: (b, 0, 0))

    in_specs = (
        [pl.BlockSpec((None, tile, D_MODEL), lambda b, j: (b, j, 0)),
         _mod_sequence_spec(layer, mod_row0),
         _const_spec(rope_local.shape),
         pl.BlockSpec((None, 2, LANES), lambda b, j: (j, 0, 0)),
         pl.BlockSpec(memory_space=pltpu.SMEM)]
        + [_layer_spec(a, layer) for a in small]
        + [_layer_spec(a, layer) for a in weights])
    out_shape = (
        jax.ShapeDtypeStruct((bsz, seq, D_MODEL), F32),
        jax.ShapeDtypeStruct((bsz, CONF_CONV_W - 1, CONV_W), F32),
        jax.ShapeDtypeStruct((bsz, SC_CONV_W - 1, CONV_W), F32),
        jax.ShapeDtypeStruct((bsz, WINDOW, KV_DIM), F32),
        jax.ShapeDtypeStruct((bsz, WINDOW, KV_DIM), F32),
    )
    out_specs = (
        pl.BlockSpec((None, tile, D_MODEL), lambda b, j: (b, j, 0)),
        per_sequence(CONF_CONV_W - 1, CONV_W), per_sequence(SC_CONV_W - 1, CONV_W),
        per_sequence(WINDOW, KV_DIM), per_sequence(WINDOW, KV_DIM),
    )
    return pl.pallas_call(
        _mixer_prompt_kernel,
        out_shape=out_shape,
        grid=(bsz, seq // tile),
        in_specs=in_specs,
        out_specs=out_specs,
        scratch_shapes=[
            pltpu.VMEM((2, ATTN_BLOCK, KV_DIM), BF16),
            pltpu.VMEM((2, ATTN_BLOCK, KV_DIM), BF16),
            pltpu.VMEM((2, A_PAD, CONV_W), F32),
            pltpu.VMEM((2, B_PAD, CONV_W), F32),
            pltpu.VMEM((A_PAD + tile, CONV_W), F32),
            pltpu.VMEM((B_PAD + tile, CONV_W), F32),
            pltpu.VMEM((ATTN_BLOCK, KEY_SPAN), F32),
        ],
        compiler_params=pltpu.CompilerParams(
            dimension_semantics=("arbitrary", "arbitrary"),
            vmem_limit_bytes=VMEM_BYTES - 4 * 1024 * 1024),
        name="mixer_prompt",
    )(x, mod, rope_local, rope_tile, lw["sinks"][layer], *small, *weights)


SAMPLE_NEW_ROWS = 16


def _sample_attn_kernel(x_ref, mod_ref, rope_ref, sink_ref, n1g_ref, qkg_ref, segm_ref,
                        ck_ref, cv_ref, wq_ref, wkv_ref, w_ao_ref,
                        ao_ref, nk_ref, nv_ref, kcat_s, vcat_s):
    nseq, tq, _ = x_ref.shape
    rows = nseq * tq
    cache_lo = SAMPLE_NEW_ROWS
    cache_hi = cache_lo + WINDOW

    x = x_ref[...]
    shift1, scale1 = mod_ref[:, 0:1, :], mod_ref[:, 1:2, :]
    h = _rms_mod(x, n1g_ref[...], scale1, shift1).reshape(rows, D_MODEL).astype(BF16)

    q, k, v = _qkv_heads(_dot(h, wq_ref[...]), _dot(h, wkv_ref[...]), qkg_ref, segm_ref[...],
                         *_rope_lane_tables(rope_ref[0], rope_ref[1]))
    k3 = k.reshape(nseq, tq, KV_DIM)
    v3 = v.reshape(nseq, tq, KV_DIM)
    keep = WINDOW - tq
    nk_ref[:, 0:keep, :] = ck_ref[:, tq:WINDOW, :]
    nk_ref[:, keep:WINDOW, :] = k3
    nv_ref[:, 0:keep, :] = cv_ref[:, tq:WINDOW, :]
    nv_ref[:, keep:WINDOW, :] = v3
    pad = jnp.zeros((nseq, SAMPLE_NEW_ROWS - tq, KV_DIM), F32)
    kcat_s[:, 0:cache_lo, :] = jnp.concatenate([k3, pad], axis=1).astype(BF16)
    vcat_s[:, 0:cache_lo, :] = jnp.concatenate([v3, pad], axis=1).astype(BF16)
    kcat_s[:, cache_lo:cache_hi, :] = ck_ref[...].astype(BF16)
    vcat_s[:, cache_lo:cache_hi, :] = cv_ref[...].astype(BF16)
    @pl.when(pl.program_id(0) == 0)
    def _():
        tail = jnp.zeros((nseq, KEY_SPAN - cache_hi, KV_DIM), BF16)
        kcat_s[:, cache_hi:KEY_SPAN, :] = tail
        vcat_s[:, cache_hi:KEY_SPAN, :] = tail

    n_q = N_HEADS * tq
    tt = lax.broadcasted_iota(jnp.int32, (n_q, KEY_SPAN), 0) & (tq - 1)
    ii = lax.broadcasted_iota(jnp.int32, (n_q, KEY_SPAN), 1)
    visible = ((ii < tq) & (ii <= tt)) | ((ii >= cache_lo + tt) & (ii < cache_hi))
    bias = jnp.where(visible, 0.0, NEG).astype(F32)
    cm_scaled = _kv_head_masks(HEAD_DIM ** -0.5)
    cm_plain = _kv_head_masks(1.0)
    q3 = q.reshape(nseq, tq, D_MODEL)
    qbd = jnp.concatenate([q3[:, :, r * KV_DIM:(r + 1) * KV_DIM] * cm_scaled[g]
                           for r in range(Q_REP) for g in range(N_KV_HEADS)], axis=1).astype(BF16)
    s = jnp.einsum("bqc,bkc->bqk", qbd, kcat_s[...], preferred_element_type=F32) + bias
    sink = jnp.concatenate([jnp.full((tq, 1), sink_ref[g * Q_REP + r], F32)
                            for r in range(Q_REP) for g in range(N_KV_HEADS)], axis=0)
    p = _softmax_with_sink(s, sink).astype(BF16)
    o = jnp.einsum("bqk,bkc->bqc", p, vcat_s[...], preferred_element_type=F32)
    slabs = []
    for r in range(Q_REP):
        acc = None
        for g in range(N_KV_HEADS):
            i = r * N_KV_HEADS + g
            part = o[:, i * tq:(i + 1) * tq, :] * cm_plain[g]
            acc = part if acc is None else acc + part
        slabs.append(acc)
    attn = jnp.concatenate(slabs, axis=2).reshape(rows, D_MODEL)
    ao_ref[...] = _dot(attn.astype(BF16), w_ao_ref[...]).reshape(nseq, tq, D_MODEL)


def _sample_rest_kernel(x_ref, mod_ref, ao_ref, n1g_ref, caw_ref, v512_ref, wc_ref, bsp_ref,
                        sta_ref, stb_ref, w_in_ref, w_aout_ref, w_bout_ref, w_cout_ref, w_o_ref,
                        x1_ref, na_ref, nb_ref, vv_ref, glu_s, aact_s, bcat_s):
    nseq, tq, _ = x_ref.shape
    rows = nseq * tq

    x = x_ref[...]
    shift1, scale1, gate1 = mod_ref[:, 0:1, :], mod_ref[:, 1:2, :], mod_ref[:, 2:3, :]
    h = _rms_mod(x, n1g_ref[...], scale1, shift1).reshape(rows, D_MODEL).astype(BF16)
    attn_out = ao_ref[...].reshape(rows, D_MODEL)
    merged = jax.nn.sigmoid(_dot(h, w_in_ref[:, G0:G0 + D_MODEL])) * attn_out

    a_in = _dot(h, w_in_ref[:, A0:A0 + 2 * CONV_W])
    glu = a_in[:, :CONV_W] * jax.nn.sigmoid(a_in[:, CONV_W:])
    hist_a = CONF_CONV_W - 1
    glu_s[...] = glu.reshape(nseq, tq, CONV_W)

    def conv_in(t):
        return sta_ref[t] if t < hist_a else glu_s[:, t - hist_a, :]

    for t in range(tq):
        acc = jnp.broadcast_to(v512_ref[0:1, :], (nseq, CONV_W))
        for tap in range(CONF_CONV_W):
            acc = acc + caw_ref[tap:tap + 1, :] * conv_in(t + tap)
        aact_s[:, t, :] = jax.nn.silu(_layer_norm(acc, v512_ref[1:2, :], v512_ref[2:3, :]))
    for t in range(hist_a):
        na_ref[t] = conv_in(t + tq)
    a_out = _dot(aact_s[...].reshape(rows, CONV_W).astype(BF16), w_aout_ref[...])
    merged = merged + jax.nn.sigmoid(_dot(h, w_in_ref[:, G0 + D_MODEL:G0 + 2 * D_MODEL])) * a_out

    b_in = _dot(h, w_in_ref[:, B0:B0 + 3 * CONV_W])
    hist_b = SC_CONV_W - 1
    base_b = B_PAD - hist_b
    bcat_s[:, base_b:B_PAD, :] = stb_ref[...]
    bcat_s[:, B_PAD:B_PAD + tq, :] = (b_in[:, CONV_W:2 * CONV_W]
                                      * b_in[:, 2 * CONV_W:]).reshape(nseq, tq, CONV_W)
    sc = None
    for t in range(SC_CONV_W):
        term = v512_ref[5 + t:6 + t, :] * bcat_s[:, base_b + t:base_b + t + tq, :]
        sc = term if sc is None else sc + term
    nb_ref[...] = bcat_s[:, base_b + tq:B_PAD + tq, :]
    b_out = _dot((b_in[:, :CONV_W] * sc.reshape(rows, CONV_W)).astype(BF16), w_bout_ref[...])
    merged = merged + jax.nn.sigmoid(_dot(h, w_in_ref[:, G0 + 2 * D_MODEL:G0 + 3 * D_MODEL])) * b_out

    zc = jax.nn.gelu(_dot(h, w_in_ref[:, C0:C0 + 2 * CONV_W]))
    u3 = zc[:, :CONV_W].reshape(nseq, tq, CONV_W)
    vv3 = _layer_norm(zc[:, CONV_W:], v512_ref[3:4, :], v512_ref[4:5, :]).reshape(nseq, tq, CONV_W)
    vv_ref[...] = vv3
    t_idx = lax.broadcasted_iota(jnp.int32, (tq, CONV_W), 0)
    z = jnp.broadcast_to(bsp_ref[...], (nseq, tq, CONV_W))
    for src in range(tq):
        coef = jnp.where(t_idx >= src, wc_ref[src], 0.0)
        z = z + coef * vv3[:, src:src + 1, :]
    c_out = _dot((u3 * z).reshape(rows, CONV_W).astype(BF16), w_cout_ref[...])
    merged = merged + jax.nn.sigmoid(_dot(h, w_in_ref[:, G0 + 3 * D_MODEL:G0 + 4 * D_MODEL])) * c_out

    y = _dot(merged.astype(BF16), w_o_ref[...]).reshape(nseq, tq, D_MODEL)
    x1_ref[...] = x + gate1 * y


def _seq_spec(nseq, r, c):
    return pl.BlockSpec((nseq, r, c), lambda i: (i, 0, 0))


def _sample_attention(x, mod, rope, cache_k, cache_v, lw, layer):
    nbatch, tq, _ = x.shape
    nseq = SAMPLE_ATTN_SEQS
    small = [lw["n1g"], lw["qkg"], lw["segm"]]

    def cache_spec():
        return pl.BlockSpec((None, nseq, WINDOW, KV_DIM), lambda i: (layer, i, 0, 0))

    in_specs = (
        [_seq_spec(nseq, tq, D_MODEL), _mod_rows_spec(layer, nseq),
         _const_spec(rope.shape), pl.BlockSpec(memory_space=pltpu.SMEM)]
        + [_layer_spec(a, layer) for a in small]
        + [cache_spec(), cache_spec(),
           _layer_spec(lw["wq"], layer), _layer_spec(lw["w_in"], layer, cols=(K0, 2 * KV_DIM)),
           _layer_spec(lw["w_ao"], layer)])
    out_shape = (
        jax.ShapeDtypeStruct((nbatch, tq, D_MODEL), F32),
        jax.ShapeDtypeStruct((nbatch, WINDOW, KV_DIM), F32),
        jax.ShapeDtypeStruct((nbatch, WINDOW, KV_DIM), F32),
    )
    out_specs = (_seq_spec(nseq, tq, D_MODEL), _seq_spec(nseq, WINDOW, KV_DIM),
                 _seq_spec(nseq, WINDOW, KV_DIM))
    return pl.pallas_call(
        _sample_attn_kernel,
        out_shape=out_shape,
        grid=(nbatch // nseq,),
        in_specs=in_specs,
        out_specs=out_specs,
        scratch_shapes=[
            pltpu.VMEM((nseq, KEY_SPAN, KV_DIM), BF16),
            pltpu.VMEM((nseq, KEY_SPAN, KV_DIM), BF16),
        ],
        compiler_params=pltpu.CompilerParams(
            dimension_semantics=("arbitrary",),
            vmem_limit_bytes=VMEM_BYTES - 16 * 1024 * 1024),
        name="sample_attention",
    )(x, mod, rope, lw["sinks"][layer], *small, cache_k, cache_v, lw["wq"], lw["w_in"], lw["w_ao"])


def _sample_rest(x, mod, attn_out, state_a, state_b, lw, layer):
    nbatch, tq, _ = x.shape
    nseq = SAMPLE_SEQS
    small = [lw["n1g"], lw["caw"], lw["v512"], lw["wc8"], lw["bsp8"]]
    weights = [lw["w_in"], lw["w_aout"], lw["w_bout"], lw["w_cout"], lw["w_o"]]
    hist_a = CONF_CONV_W - 1
    in_specs = (
        [_seq_spec(nseq, tq, D_MODEL), _mod_rows_spec(layer, nseq), _seq_spec(nseq, tq, D_MODEL)]
        + [_layer_spec(a, layer) for a in small]
        + [pl.BlockSpec((None, hist_a, nseq, CONV_W), lambda i: (layer, 0, i, 0)),
           pl.BlockSpec((None, nseq, SC_CONV_W - 1, CONV_W), lambda i: (layer, i, 0, 0))]
        + [_layer_spec(a, layer) for a in weights])
    out_shape = (
        jax.ShapeDtypeStruct((nbatch, tq, D_MODEL), F32),
        jax.ShapeDtypeStruct((hist_a, nbatch, CONV_W), F32),
        jax.ShapeDtypeStruct((nbatch, SC_CONV_W - 1, CONV_W), F32),
        jax.ShapeDtypeStruct((nbatch, tq, CONV_W), F32),
    )
    out_specs = (_seq_spec(nseq, tq, D_MODEL),
                 pl.BlockSpec((hist_a, nseq, CONV_W), lambda i: (0, i, 0)),
                 _seq_spec(nseq, SC_CONV_W - 1, CONV_W), _seq_spec(nseq, tq, CONV_W))
    return pl.pallas_call(
        _sample_rest_kernel,
        out_shape=out_shape,
        grid=(nbatch // nseq,),
        in_specs=in_specs,
        out_specs=out_specs,
        scratch_shapes=[
            pltpu.VMEM((nseq, tq, CONV_W), F32),
            pltpu.VMEM((nseq, tq, CONV_W), F32),
            pltpu.VMEM((nseq, B_PAD + tq, CONV_W), F32),
        ],
        compiler_params=pltpu.CompilerParams(
            dimension_semantics=("arbitrary",),
            vmem_limit_bytes=VMEM_BYTES - 8 * 1024 * 1024),
        name="sample_rest",
    )(x, mod, attn_out, *small, state_a, state_b, *weights)


def _mlp_kernel(x_ref, mod_ref, g_ref, w_up_ref, w_down_ref, o_ref):
    x = x_ref[...]
    if x.ndim == 3:
        shift2, scale2, gate2 = mod_ref[:, 3:4, :], mod_ref[:, 4:5, :], mod_ref[:, 5:6, :]
    else:
        shift2, scale2, gate2 = mod_ref[3:4, :], mod_ref[4:5, :], mod_ref[5:6, :]
    h2 = _rms_mod(x, g_ref[...], scale2, shift2)
    up = _dot(h2.reshape(-1, D_MODEL).astype(BF16), w_up_ref[...])
    act = jnp.square(jnp.maximum(up, 0.0)).astype(BF16)
    y = _dot(act, w_down_ref[...]).reshape(x.shape)
    o_ref[...] = x + gate2 * y


def _mlp(x, mod, mod_row0, lw, layer, per_sequence):
    if per_sequence:
        nbatch, tq, _ = x.shape
        nseq = SAMPLE_MLP_TILE // tq
        grid = (nbatch // nseq,)
        x_spec = pl.BlockSpec((nseq, tq, D_MODEL), lambda i: (i, 0, 0))
        mod_spec = _mod_rows_spec(layer, nseq)
    else:
        bsz, seq, _ = x.shape
        grid = (bsz, seq // MLP_TILE)
        x_spec = pl.BlockSpec((None, MLP_TILE, D_MODEL), lambda b, j: (b, j, 0))
        mod_spec = _mod_sequence_spec(layer, mod_row0)
    return pl.pallas_call(
        _mlp_kernel,
        out_shape=jax.ShapeDtypeStruct(x.shape, F32),
        grid=grid,
        in_specs=[x_spec, mod_spec, _layer_spec(lw["n2g"], layer),
                  _layer_spec(lw["w_up"], layer), _layer_spec(lw["w_down"], layer)],
        out_specs=x_spec,
        compiler_params=pltpu.CompilerParams(
            dimension_semantics=("arbitrary",) * len(grid),
            vmem_limit_bytes=VMEM_BYTES - 4 * 1024 * 1024),
        name="mlp_sample" if per_sequence else "mlp_prompt",
    )(x, mod, lw["n2g"], lw["w_up"], lw["w_down"])


def _rope_cos_sin(pos):
    half = ROT_DIM // 2
    freq = (jnp.arange(LANES) % half).astype(F32)
    inv_freq = jnp.power(ROPE_THETA, -freq * 2.0 / ROT_DIM)
    ang = pos.astype(F32)[:, None] * inv_freq[None, :]
    return jnp.stack([jnp.cos(ang), jnp.sin(ang)])


def _rope_lane_tables(cos, sin):
    half = ROT_DIM // 2
    lane = lax.broadcasted_iota(jnp.int32, (1, LANES), 1) & (HEAD_DIM - 1)
    cos_t = jnp.where(lane < ROT_DIM, cos, 1.0)
    sin_a = jnp.where(lane < half, -sin, 0.0)
    sin_b = jnp.where((lane >= half) & (lane < ROT_DIM), sin, 0.0)
    return cos_t, sin_a, sin_b


def _prepare_weights(p):
    depth = p["w_in"].shape[0]
    w_in = p["w_in"]
    wq = w_in[:, :, :D_MODEL].reshape(depth, D_MODEL, N_KV_HEADS, Q_REP, HEAD_DIM)
    wq = wq.transpose(0, 1, 3, 2, 4).reshape(depth, D_MODEL, D_MODEL).astype(BF16)
    w_ao = p["w_attn_o"].reshape(depth, N_KV_HEADS, Q_REP, HEAD_DIM, D_MODEL)
    w_ao = w_ao.transpose(0, 2, 1, 3, 4).reshape(depth, D_MODEL, D_MODEL).astype(BF16)
    ws = p["w_spatial"]
    n_pairs = GMLP_GROUPS // 2
    wpair = ws.reshape(depth, n_pairs, 2, GMLP_CHUNK, GMLP_CHUNK).transpose(0, 1, 3, 2, 4)
    wpair = wpair.reshape(depth, n_pairs, GMLP_CHUNK, 2 * GMLP_CHUNK)
    bsp = jnp.repeat(p["b_spatial"].transpose(0, 2, 1), GMLP_GROUP_DIM, axis=2)
    wc8 = jnp.repeat(ws[:, :, :DEC_SEQ, :DEC_SEQ].transpose(0, 3, 2, 1),
                     GMLP_GROUP_DIM, axis=3)
    lane_head = jnp.arange(LANES) // HEAD_DIM
    segm = jnp.where(lane_head[:, None] == lane_head[None, :], 1.0 / HEAD_DIM, 0.0).astype(BF16)
    caw = jnp.concatenate([p["conv_a_w"], jnp.zeros((depth, 1, CONV_W), F32)], axis=1)
    v512 = jnp.concatenate([p["conv_a_b"][:, None], p["ln_a_g"][:, None], p["ln_a_b"][:, None],
                            p["ln_c_g"][:, None], p["ln_c_b"][:, None], p["conv_b_w"]], axis=1)
    reps = LANES // HEAD_DIM
    return {
        "wq": wq, "w_in": w_in.astype(BF16), "w_ao": w_ao,
        "w_aout": p["w_a_out"].astype(BF16), "w_bout": p["w_b_out"].astype(BF16),
        "w_cout": p["w_c_out"].astype(BF16), "w_o": p["w_o"].astype(BF16),
        "w_up": p["w_up"].astype(BF16), "w_down": p["w_down"].astype(BF16),
        "n1g": p["norm1_g"][:, None], "n2g": p["norm2_g"][:, None],
        "qkg": jnp.stack([jnp.tile(p["q_norm_g"], (1, reps)),
                          jnp.tile(p["k_norm_g"], (1, reps))], axis=1),
        "caw": caw, "v512": v512, "wpair": wpair, "bsp": bsp, "wc8": wc8,
        "bsp8": bsp[:, :DEC_SEQ],
        "segm": jnp.broadcast_to(segm, (depth, LANES, LANES)), "sinks": p["attn_sinks"],
    }


def kernel(x_prompt, x_sample, c_prompt, c_sample, state_conv_a, state_conv_b, cache_k_win, cache_v_win, w_ada, b_ada, norm1_g, norm2_g, w_in, q_norm_g, k_norm_g, attn_sinks, w_attn_o, conv_a_w, conv_a_b, ln_a_g, ln_a_b, w_a_out, conv_b_w, w_b_out, ln_c_g, ln_c_b, w_spatial, b_spatial, w_c_out, w_o, w_up, w_down):
    params = dict(w_in=w_in, w_attn_o=w_attn_o, w_spatial=w_spatial, b_spatial=b_spatial,
                  conv_a_w=conv_a_w, conv_a_b=conv_a_b, ln_a_g=ln_a_g, ln_a_b=ln_a_b,
                  ln_c_g=ln_c_g, ln_c_b=ln_c_b, conv_b_w=conv_b_w, w_a_out=w_a_out,
                  w_b_out=w_b_out, w_c_out=w_c_out, w_o=w_o, w_up=w_up, w_down=w_down,
                  norm1_g=norm1_g, norm2_g=norm2_g, q_norm_g=q_norm_g, k_norm_g=k_norm_g,
                  attn_sinks=attn_sinks)
    depth = w_in.shape[0]
    bp, seq, _ = x_prompt.shape
    bs, tq, _ = x_sample.shape

    c_all = jnp.concatenate([c_sample, c_prompt,
                             jnp.zeros((MOD_ROWS - bp - bs, D_MODEL), F32)], axis=0)
    mod = _modulation(c_all, w_ada, b_ada).reshape(depth, MOD_ROWS, 6, D_MODEL)
    rope_local = _rope_cos_sin(jnp.arange(PROMPT_TILE))
    rope_tile = _rope_cos_sin(jnp.arange(0, seq, PROMPT_TILE)).transpose(1, 0, 2)
    rope_s = jnp.tile(_rope_cos_sin(PAST_LEN + jnp.arange(tq)), (1, SAMPLE_ATTN_SEQS, 1))
    state_a_tm = state_conv_a.transpose(0, 2, 1, 3)
    cache_k = cache_k_win.reshape(depth, bs, WINDOW, KV_DIM)
    cache_v = cache_v_win.reshape(depth, bs, WINDOW, KV_DIM)

    xp, xs = x_prompt, x_sample
    outs = [[] for _ in range(9)]
    lw = _prepare_weights(params)
    for l in range(depth):
        xp, pa, pb, pk, pv = _mixer_prompt(xp, mod, bs, rope_local, rope_tile, lw, l)
        xp = _mlp(xp, mod, bs, lw, l, per_sequence=False)
        s_attn, sk, sv = _sample_attention(xs, mod, rope_s, cache_k, cache_v, lw, l)
        xs, sa, sb, sg = _sample_rest(xs, mod, s_attn, state_a_tm, state_conv_b, lw, l)
        sa = sa.transpose(1, 0, 2)
        xs = _mlp(xs, mod, bs, lw, l, per_sequence=True)
        kv_shape_p = (bp, WINDOW, N_KV_HEADS, HEAD_DIM)
        kv_shape_s = (bs, WINDOW, N_KV_HEADS, HEAD_DIM)
        for lst, val in zip(outs, (pa, sa, pb, sb, pk.reshape(kv_shape_p), sk.reshape(kv_shape_s),
                                   pv.reshape(kv_shape_p), sv.reshape(kv_shape_s), sg)):
            lst.append(val)
    return (xp, xs) + tuple(jnp.stack(o) for o in outs)
```
